```python
import jax
import jax.numpy as jnp
from jax import lax
import numpy as np

D_MODEL = 1024
BATCH = 32
SEQ = 2048
DEPTH = 2

MEM_LEN = 256
HEAD_DIM = 64
N_HEADS_A = 6
N_HEADS_B = 6
N_HEADS_M = 4
WIDTH_A = N_HEADS_A * HEAD_DIM
WIDTH_B = N_HEADS_B * HEAD_DIM
WIDTH_M = N_HEADS_M * HEAD_DIM
MIX_WIDTH = WIDTH_A + WIDTH_B + WIDTH_M
IN_WIDTH = 3 * WIDTH_A + 3 * WIDTH_B + WIDTH_M
DILATED_PATTERNS = ((128, 1), (512, 4), (2048, 16))
DIL_BLOCK = 64
GRID_W = 64
NA_ROWS = 8
NA_COLS = 16
N_GROUPS = 4
EXPERTS_PER_GROUP = 8
N_EXPERTS = N_GROUPS * EXPERTS_PER_GROUP
TOP_K_IN_GROUP = 2
D_EXPERT = 512
EPS = 1e-6
NEG_INF = -1e30
ATTN_SCALE = HEAD_DIM ** -0.5

kernel_name = "hybrid_dilated_neighbourhood_memory_hmoe_encoder"


def rms_norm(x, gain):
    xf = x.astype(jnp.float32)
    y = xf * lax.rsqrt(jnp.mean(jnp.square(xf), axis=-1, keepdims=True) + EPS)
    return (y * gain.astype(jnp.float32)).astype(x.dtype)


def alibi_slopes(n):
    return (2.0 ** (-8.0 * np.arange(1, n + 1) / n)).astype(np.float32)


def dilated_window_attention(q, k, v, slopes, window, dilation):
    b, s, h, dh = q.shape
    radius = window // (2 * dilation)
    length = s // dilation
    n_blk = -(-length // DIL_BLOCK)
    pad = n_blk * DIL_BLOCK - length

    def to_strided(t):
        t = t.reshape(b, length, dilation, h, dh).transpose(0, 2, 3, 1, 4)
        return jnp.pad(t, ((0, 0), (0, 0), (0, 0), (0, pad), (0, 0)))

    def band(t):
        t = jnp.pad(t, ((0, 0), (0, 0), (0, 0), (DIL_BLOCK, DIL_BLOCK), (0, 0)))
        t = t.reshape(b, dilation, h, n_blk + 2, DIL_BLOCK, dh)
        return jnp.concatenate([t[:, :, :, :-2], t[:, :, :, 1:-1], t[:, :, :, 2:]], axis=4)

    q_blk = to_strided(q).reshape(b, dilation, h, n_blk, DIL_BLOCK, dh)
    k_blk = band(to_strided(k))
    v_blk = band(to_strided(v))
    q_pos = jnp.arange(n_blk * DIL_BLOCK).reshape(n_blk, DIL_BLOCK)
    k_pos = (jnp.arange(n_blk)[:, None] - 1) * DIL_BLOCK + jnp.arange(3 * DIL_BLOCK)[None, :]
    rel = jnp.abs(k_pos[:, None, :] - q_pos[:, :, None])
    valid = (rel <= radius) & (k_pos[:, None, :] >= 0) & (k_pos[:, None, :] < length)
    alibi = -slopes[:, None, None, None] * (rel * dilation).astype(jnp.float32)[None]
    scores = jnp.einsum("bdhnqc,bdhnkc->bdhnqk", q_blk, k_blk).astype(jnp.float32) * ATTN_SCALE + alibi
    scores = jnp.where(valid, scores, NEG_INF)
    m = jnp.max(scores, axis=-1, keepdims=True)
    e = jnp.exp(scores - m)
    den = jnp.sum(e, axis=-1, keepdims=True)
    lse = (m + jnp.log(den))[..., 0]
    out = jnp.einsum("bdhnqk,bdhnkc->bdhnqc", e / den, v_blk.astype(jnp.float32))
    out = out.reshape(b, dilation, h, n_blk * DIL_BLOCK, dh)[:, :, :, :length]
    out = out.transpose(0, 3, 1, 2, 4).reshape(b, s, h, dh)
    lse = lse.reshape(b, dilation, h, n_blk * DIL_BLOCK)[..., :length]
    lse = lse.transpose(0, 3, 1, 2).reshape(b, s, h)
    return out, lse


def dilated_mixture_attention(q, k, v, slopes):
    outs, lses = [], []
    for window, dilation in DILATED_PATTERNS:
        o, l = dilated_window_attention(q, k, v, slopes, window, dilation)
        outs.append(o)
        lses.append(l)
    weights = jax.nn.softmax(jnp.stack(lses, axis=0), axis=0)
    return jnp.sum(weights[..., None] * jnp.stack(outs, axis=0), axis=0)


def neighbourhood_attention(q, k, v, rpb):
    b, s, h, dh = q.shape
    rows = s // GRID_W
    kh = min(NA_ROWS, rows)

    def to_grid(t):
        return t.reshape(b, rows, GRID_W, h, dh).transpose(0, 3, 1, 2, 4)

    qg, kg, vg = to_grid(q), to_grid(k), to_grid(v)
    r = jnp.arange(rows)
    r0 = jnp.clip(r - kh // 2, 0, rows - kh)
    row_idx = r0[:, None] + jnp.arange(kh)[None, :]
    k_rows = kg[:, :, row_idx]
    v_rows = vg[:, :, row_idx]
    c = jnp.arange(GRID_W)
    c0 = jnp.clip(c - NA_COLS // 2, 0, GRID_W - NA_COLS)
    col_ok = (c[None, :] >= c0[:, None]) & (c[None, :] < c0[:, None] + NA_COLS)
    dr = row_idx - r[:, None]
    dc = jnp.clip(c[None, :] - c[:, None], -(NA_COLS - 1), NA_COLS - 1)
    bias = rpb[:, dr[:, :, None, None] + NA_ROWS - 1, dc[None, None] + NA_COLS - 1]
    bias = bias.astype(jnp.float32).transpose(0, 1, 3, 2, 4)
    scores = jnp.einsum("bhrqc,bhrjkc->bhrqjk", qg, k_rows).astype(jnp.float32) * ATTN_SCALE + bias[None]
    scores = jnp.where(col_ok[None, None, None, :, None, :], scores, NEG_INF)
    probs = jax.nn.softmax(scores.reshape(b, h, rows, GRID_W, kh * GRID_W), axis=-1).reshape(scores.shape)
    out = jnp.einsum("bhrqjk,bhrjkc->bhrqc", probs, v_rows.astype(jnp.float32))
    return out.transpose(0, 2, 3, 1, 4).reshape(b, s, h, dh)


def memory_attention(q, k, v):
    scores = jnp.einsum("bshc,bmhc->bhsm", q, k).astype(jnp.float32) * ATTN_SCALE
    probs = jax.nn.softmax(scores, axis=-1)
    return jnp.einsum("bhsm,bmhc->bshc", probs, v.astype(jnp.float32))


def hierarchical_moe(h, w_group, b_group, w_router, b_router, w_gate, w_up, w_down):
    b, s, d = h.shape
    t = h.reshape(b * s, d)
    group_logits = jnp.einsum("td,dg->tg", t, w_group).astype(jnp.float32) + b_group.astype(jnp.float32)
    group_prob = jax.nn.softmax(group_logits, axis=-1)
    g_top, g_idx = lax.top_k(group_prob, 1)
    exp_logits = jnp.einsum("td,de->te", t, w_router).astype(jnp.float32) + b_router.astype(jnp.float32)
    exp_logits = exp_logits.reshape(-1, N_GROUPS, EXPERTS_PER_GROUP)
    in_group = jnp.take_along_axis(exp_logits, g_idx[:, :, None], axis=1)[:, 0]
    e_val, e_idx = lax.top_k(in_group, TOP_K_IN_GROUP)
    e_w = jax.nn.softmax(e_val, axis=-1) * g_top
    global_idx = g_idx * EXPERTS_PER_GROUP + e_idx
    gates = jnp.sum(jax.nn.one_hot(global_idx, N_EXPERTS, dtype=jnp.float32) * e_w[..., None], axis=1)
    gates = gates.astype(t.dtype)
    out = jnp.zeros_like(t)
    for e in range(N_EXPERTS):
        hid = jax.nn.silu(t @ w_gate[e]) * (t @ w_up[e])
        out = out + gates[:, e:e + 1] * (hid @ w_down[e])
    return out.reshape(b, s, d)


def setup_inputs(seed: int = 0) -> dict:
    key = jax.random.key(seed)
    ks = jax.random.split(key, 18)
    f32 = jnp.float32

    def nrm(k, shape, scale):
        return scale * jax.random.normal(k, shape, f32)

    return {
        "x": nrm(ks[0], (BATCH, SEQ, D_MODEL), 1.0),
        "mem": nrm(ks[1], (BATCH, MEM_LEN, D_MODEL), 1.0),
        "norm_mix": 1.0 + nrm(ks[2], (DEPTH, D_MODEL), 0.05),
        "w_in": nrm(ks[3], (DEPTH, D_MODEL, IN_WIDTH), D_MODEL ** -0.5),
        "qk_gain": 1.0 + nrm(ks[4], (DEPTH, 6, HEAD_DIM), 0.05),
        "rpb": nrm(ks[5], (DEPTH, N_HEADS_B, 2 * NA_ROWS - 1, 2 * NA_COLS - 1), 0.5),
        "norm_mem": 1.0 + nrm(ks[6], (DEPTH, D_MODEL), 0.05),
        "w_mem_kv": nrm(ks[7], (DEPTH, D_MODEL, 2 * WIDTH_M), D_MODEL ** -0.5),
        "out_gain": 1.0 + nrm(ks[8], (DEPTH, MIX_WIDTH), 0.05),
        "w_out": nrm(ks[9], (DEPTH, MIX_WIDTH, D_MODEL), MIX_WIDTH ** -0.5),
        "norm_ffn": 1.0 + nrm(ks[10], (DEPTH, D_MODEL), 0.05),
        "w_group": nrm(ks[11], (DEPTH, D_MODEL, N_GROUPS), D_MODEL ** -0.5),
        "b_group": nrm(ks[12], (DEPTH, N_GROUPS), 0.01),
        "w_router": nrm(ks[13], (DEPTH, D_MODEL, N_EXPERTS), D_MODEL ** -0.5),
        "b_router": nrm(ks[14], (DEPTH, N_EXPERTS), 0.01),
        "w_gate": nrm(ks[15], (DEPTH, N_EXPERTS, D_MODEL, D_EXPERT), D_MODEL ** -0.5),
        "w_up": nrm(ks[16], (DEPTH, N_EXPERTS, D_MODEL, D_EXPERT), D_MODEL ** -0.5),
        "w_down": nrm(ks[17], (DEPTH, N_EXPERTS, D_EXPERT, D_MODEL), D_EXPERT ** -0.5),
    }


def reference(x, mem, norm_mix, w_in, qk_gain, rpb, norm_mem, w_mem_kv, out_gain, w_out,
              norm_ffn, w_group, b_group, w_router, b_router, w_gate, w_up, w_down):
    bsz, seq, _ = x.shape
    mem_len = mem.shape[1]
    slopes = jnp.asarray(alibi_slopes(N_HEADS_A))
    split_at = [3 * WIDTH_A, 3 * WIDTH_A + 3 * WIDTH_B]
    for l in range(DEPTH):
        h = rms_norm(x, norm_mix[l])
        proj = jnp.einsum("bsd,de->bse", h, w_in[l])
        a_qkv, b_qkv, m_q = jnp.split(proj, split_at, axis=-1)
        qa, ka, va = (t.reshape(bsz, seq, N_HEADS_A, HEAD_DIM) for t in jnp.split(a_qkv, 3, axis=-1))
        qn, kn, vn = (t.reshape(bsz, seq, N_HEADS_B, HEAD_DIM) for t in jnp.split(b_qkv, 3, axis=-1))
        qm = m_q.reshape(bsz, seq, N_HEADS_M, HEAD_DIM)
        mem_kv = jnp.einsum("bmd,de->bme", rms_norm(mem, norm_mem[l]), w_mem_kv[l])
        km, vm = (t.reshape(bsz, mem_len, N_HEADS_M, HEAD_DIM) for t in jnp.split(mem_kv, 2, axis=-1))

        o_a = dilated_mixture_attention(rms_norm(qa, qk_gain[l, 0]), rms_norm(ka, qk_gain[l, 1]), va, slopes)
        o_b = neighbourhood_attention(rms_norm(qn, qk_gain[l, 2]), rms_norm(kn, qk_gain[l, 3]), vn, rpb[l])
        o_m = memory_attention(rms_norm(qm, qk_gain[l, 4]), rms_norm(km, qk_gain[l, 5]), vm)

        gain = out_gain[l]
        mixed = jnp.concatenate([
            rms_norm(o_a.reshape(bsz, seq, WIDTH_A), gain[:WIDTH_A]).astype(x.dtype),
            rms_norm(o_b.reshape(bsz, seq, WIDTH_B), gain[WIDTH_A:WIDTH_A + WIDTH_B]).astype(x.dtype),
            rms_norm(o_m.reshape(bsz, seq, WIDTH_M), gain[WIDTH_A + WIDTH_B:]).astype(x.dtype),
        ], axis=-1)
        x = x + jnp.einsum("bse,ed->bsd", mixed, w_out[l])

        h = rms_norm(x, norm_ffn[l])
        x = x + hierarchical_moe(h, w_group[l], b_group[l], w_router[l], b_router[l],
                                 w_gate[l], w_up[l], w_down[l])
    return x
```

```python
import functools

import numpy as np
import jax
import jax.numpy as jnp
from jax import lax
from jax.experimental import pallas as pl
from jax.experimental.pallas import tpu as pltpu

F32 = jnp.float32
BF16 = jnp.bfloat16

D_MODEL = 1024
HEAD_DIM = 64
N_HEADS_A = 6
N_HEADS_B = 6
N_HEADS_M = 4
WIDTH_A = N_HEADS_A * HEAD_DIM
WIDTH_B = N_HEADS_B * HEAD_DIM
WIDTH_M = N_HEADS_M * HEAD_DIM
IN_WIDTH = 3 * WIDTH_A + 3 * WIDTH_B + WIDTH_M
DILATED_PATTERNS = ((128, 1), (512, 4), (2048, 16))
DIL_RADIUS = 64
GRID_W = 64
NA_ROWS = 8
NA_COLS = 16
N_GROUPS = 4
EXPERTS_PER_GROUP = 8
N_EXPERTS = N_GROUPS * EXPERTS_PER_GROUP
D_EXPERT = 512
EPS = 1e-6
ATTN_SCALE = HEAD_DIM ** -0.5
LOG2E = float(np.log2(np.e))
QSCALE = ATTN_SCALE * LOG2E

LANES = 128
SUBLANES = 8
ROW_TILES = D_MODEL // LANES
QBLK = 128
PROJ_ROWS = 512
MOE_TM = 128
VMEM_LIMIT = 56 * 1024 * 1024


def _cparams(sem):
    return pltpu.CompilerParams(dimension_semantics=sem, vmem_limit_bytes=VMEM_LIMIT)


def _proj_kernel(*refs, add):
    if add:
        x_ref, m_ref, g_ref, w_ref, xo_ref, o_ref = refs
        x = x_ref[...] + m_ref[...]
        xo_ref[...] = x
    else:
        x_ref, g_ref, w_ref, o_ref = refs
        x = x_ref[...]
    ms = jnp.mean(x * x, axis=-1, keepdims=True)
    h = (x * lax.rsqrt(ms + EPS) * g_ref[...]).astype(BF16)
    o_ref[...] = jnp.dot(h, w_ref[...], preferred_element_type=F32).astype(BF16)


def _proj(x2, moe2, gain, w):
    t = x2.shape[0]
    add = moe2 is not None
    row_spec = pl.BlockSpec((PROJ_ROWS, D_MODEL), lambda i: (i, 0))
    in_specs = [row_spec] + ([row_spec] if add else []) + [
        pl.BlockSpec((1, D_MODEL), lambda i: (0, 0)),
        pl.BlockSpec((D_MODEL, IN_WIDTH), lambda i: (0, 0)),
    ]
    proj_spec = pl.BlockSpec((PROJ_ROWS, IN_WIDTH), lambda i: (i, 0))
    proj_shape = jax.ShapeDtypeStruct((t, IN_WIDTH), BF16)
    if add:
        out_shape = (jax.ShapeDtypeStruct((t, D_MODEL), F32), proj_shape)
        out_specs = (row_spec, proj_spec)
        args = (x2, moe2, gain, w)
    else:
        out_shape, out_specs, args = proj_shape, proj_spec, (x2, gain, w)
    return pl.pallas_call(
        functools.partial(_proj_kernel, add=add),
        out_shape=out_shape, grid=(t // PROJ_ROWS,), in_specs=in_specs, out_specs=out_specs,
        compiler_params=_cparams(("parallel",)), name="proj_add" if add else "proj",
    )(*args)


def _data_mask(lane, par):
    return (lane < HEAD_DIM) if par == 0 else (lane >= HEAD_DIM)


def _ones_lane(par):
    return HEAD_DIM if par == 0 else 0


def _head_norm(x, dm, gain):
    ss = jnp.sum(jnp.where(dm, x * x, 0.0), axis=-1, keepdims=True)
    return x * lax.rsqrt(ss * (1.0 / HEAD_DIM) + EPS) * gain


def _normalise(tot, lane, par):
    den = jnp.sum(jnp.where(lane == _ones_lane(par), tot, 0.0), axis=-1, keepdims=True)
    return tot / den


def _memkv_kernel(m_ref, g_ref, w_ref, kg_ref, k_ref, v_ref):
    x = m_ref[...]
    ms = jnp.mean(x * x, axis=-1, keepdims=True)
    h = (x * lax.rsqrt(ms + EPS) * g_ref[...]).astype(BF16)
    kv = jnp.dot(h, w_ref[...], preferred_element_type=F32)
    lane = lax.broadcasted_iota(jnp.int32, (x.shape[0], LANES), 1)
    for slab in range(WIDTH_M // LANES):
        kf = kv[:, slab * LANES:(slab + 1) * LANES]
        vf = kv[:, WIDTH_M + slab * LANES:WIDTH_M + (slab + 1) * LANES]
        for par in (0, 1):
            dm = _data_mask(lane, par)
            kn = _head_norm(kf, dm, kg_ref[...])
            k_ref[2 * slab + par] = jnp.where(dm, kn, 0.0).astype(BF16)
            v_ref[2 * slab + par] = jnp.where(dm, vf, jnp.where(lane == _ones_lane(par), 1.0, 0.0)).astype(BF16)


def _memkv(mem, gain, w, kgain):
    b, m, _ = mem.shape
    kv_shape = jax.ShapeDtypeStruct((b, N_HEADS_M, m, LANES), BF16)
    kv_spec = pl.BlockSpec((None, N_HEADS_M, m, LANES), lambda i: (i, 0, 0, 0))
    return pl.pallas_call(
        _memkv_kernel, out_shape=(kv_shape, kv_shape), grid=(b,),
        in_specs=[pl.BlockSpec((None, m, D_MODEL), lambda i: (i, 0, 0)),
                  pl.BlockSpec((1, D_MODEL), lambda i: (0, 0)),
                  pl.BlockSpec((D_MODEL, 2 * WIDTH_M), lambda i: (0, 0)),
                  pl.BlockSpec((1, LANES), lambda i: (0, 0))],
        out_specs=(kv_spec, kv_spec), compiler_params=_cparams(("parallel",)), name="memkv",
    )(mem, gain, w, kgain)


PREP_ROWS = 256


def _attn_a_kernel(q_ref, k_ref, v_ref, g_ref, b1_ref, b4_ref, b16_ref, o_ref,
                   qs, ks, vs, qd, kd, vd, accd, tot):
    s_len = q_ref.shape[0]
    n_prep = s_len // PREP_ROWS
    lane_p = lax.broadcasted_iota(jnp.int32, (PREP_ROWS, LANES), 1)
    gq = g_ref[0:1, :]
    gk = g_ref[1:2, :]

    for par in (0, 1):
        def prep(c, carry, par=par):
            rows = pl.ds(pl.multiple_of(c * PREP_ROWS, PREP_ROWS), PREP_ROWS)
            dm = _data_mask(lane_p, par)
            qn = _head_norm(q_ref[rows, :].astype(F32), dm, gq)
            kn = _head_norm(k_ref[rows, :].astype(F32), dm, gk)
            vf = v_ref[rows, :].astype(F32)
            qs[rows, :] = jnp.where(dm, qn * QSCALE, 0.0)
            ks[rows, :] = jnp.where(dm, kn, 0.0)
            vs[rows, :] = jnp.where(dm, vf, jnp.where(lane_p == _ones_lane(par), 1.0, 0.0))
            return carry
        lax.fori_loop(0, n_prep, prep, 0)

        for (_, dil), b_ref in zip(DILATED_PATTERNS, (b1_ref, b4_ref, b16_ref)):
            length = s_len // dil
            nblk = length // QBLK
            if dil == 1:
                def cast(c, carry):
                    rows = pl.ds(pl.multiple_of(c * PREP_ROWS, PREP_ROWS), PREP_ROWS)
                    qd[rows, :] = qs[rows, :].astype(BF16)
                    kd[rows, :] = ks[rows, :].astype(BF16)
                    vd[rows, :] = vs[rows, :].astype(BF16)
                    return carry
                lax.fori_loop(0, n_prep, cast, 0)
            else:
                def deint(r, carry, dil=dil, length=length):
                    dst = pl.ds(pl.multiple_of(r * length, QBLK), length)
                    src = pl.ds(r, length, stride=dil)
                    qd[dst, :] = qs[src, :].astype(BF16)
                    kd[dst, :] = ks[src, :].astype(BF16)
                    vd[dst, :] = vs[src, :].astype(BF16)
                    return carry
                lax.fori_loop(0, dil, deint, 0)

            unroll = 4
            n_units = dil * nblk

            def blocks(o, carry, par=par, dil=dil, length=length, nblk=nblk, b_ref=b_ref):
                for i in range(unroll):
                    u = o * unroll + i
                    if nblk == 1:
                        base = pl.multiple_of(u * length, QBLK)
                        qrows = pl.ds(base, QBLK)
                        keys = pl.ds(base, QBLK)
                        bias = b_ref[par]
                    else:
                        r = u // nblk
                        blk = u % nblk
                        q0 = blk * QBLK
                        ws = jnp.clip(q0 - DIL_RADIUS, 0, length - 2 * QBLK)
                        tid = jnp.where(blk == 0, 0, jnp.where(blk == nblk - 1, 2, 1))
                        qrows = pl.ds(pl.multiple_of(r * length + q0, QBLK), QBLK)
                        keys = pl.ds(pl.multiple_of(r * length + ws, DIL_RADIUS), 2 * QBLK)
                        bias = b_ref[par, tid]
                    s = lax.dot_general(qd[qrows, :], kd[keys, :], (((1,), (1,)), ((), ())),
                                        preferred_element_type=F32)
                    p = jnp.exp2(s + bias).astype(BF16)
                    accd[qrows, :] = jnp.dot(p, vd[keys, :], preferred_element_type=F32)
                return carry
            lax.fori_loop(0, n_units // unroll, blocks, 0)

            if dil == 1:
                def first(c, carry, par=par):
                    rows = pl.ds(pl.multiple_of(c * PREP_ROWS, PREP_ROWS), PREP_ROWS)
                    tot[par, rows, :] = accd[rows, :]
                    return carry
                lax.fori_loop(0, n_prep, first, 0)
            else:
                def merge(r, carry, par=par, dil=dil, length=length):
                    src = pl.ds(pl.multiple_of(r * length, QBLK), length)
                    dst = pl.ds(r, length, stride=dil)
                    tot[par, dst, :] = tot[par, dst, :] + accd[src, :]
                    return carry
                lax.fori_loop(0, dil, merge, 0)

    def fin(c, carry):
        rows = pl.ds(pl.multiple_of(c * PREP_ROWS, PREP_ROWS), PREP_ROWS)
        o0 = _normalise(tot[0, rows, :], lane_p, 0)
        o1 = _normalise(tot[1, rows, :], lane_p, 1)
        o_ref[rows, :] = jnp.where(lane_p < HEAD_DIM, o0, o1).astype(BF16)
        return carry
    lax.fori_loop(0, n_prep, fin, 0)


def _attn_a(proj3, gains, b1, b4, b16):
    b, s, _ = proj3.shape
    n_pairs = WIDTH_A // LANES

    def slab(off):
        return pl.BlockSpec((None, s, LANES), lambda i, p: (i, 0, off + p))
    return pl.pallas_call(
        _attn_a_kernel, out_shape=jax.ShapeDtypeStruct((b, s, WIDTH_A), BF16), grid=(b, n_pairs),
        in_specs=[slab(0), slab(n_pairs), slab(2 * n_pairs),
                  pl.BlockSpec((2, LANES), lambda i, p: (0, 0)),
                  pl.BlockSpec((2, 3, QBLK, 2 * QBLK), lambda i, p: (p, 0, 0, 0)),
                  pl.BlockSpec((2, 3, QBLK, 2 * QBLK), lambda i, p: (p, 0, 0, 0)),
                  pl.BlockSpec((2, QBLK, QBLK), lambda i, p: (p, 0, 0))],
        out_specs=pl.BlockSpec((None, s, LANES), lambda i, p: (i, 0, p)),
        scratch_shapes=[pltpu.VMEM((s, LANES), F32)] * 3 + [pltpu.VMEM((s, LANES), BF16)] * 3
        + [pltpu.VMEM((s, LANES), F32), pltpu.VMEM((2, s, LANES), F32)],
        compiler_params=_cparams(("parallel", "parallel")), name="attn_a",
    )(proj3, proj3, proj3, gains, b1, b4, b16)


def _attn_b_kernel(q_ref, k_ref, v_ref, g_ref, bias_ref, o_ref, qb, kb, vb):
    s_len = q_ref.shape[0]
    n_prep = s_len // PREP_ROWS
    rows_total = s_len // GRID_W
    kh = min(NA_ROWS, rows_total)
    lane_p = lax.broadcasted_iota(jnp.int32, (PREP_ROWS, LANES), 1)
    lane_q = lax.broadcasted_iota(jnp.int32, (GRID_W, LANES), 1)
    gq = g_ref[0:1, :]
    gk = g_ref[1:2, :]

    for par in (0, 1):
        def prep(c, carry, par=par):
            rows = pl.ds(pl.multiple_of(c * PREP_ROWS, PREP_ROWS), PREP_ROWS)
            dm = _data_mask(lane_p, par)
            qn = _head_norm(q_ref[rows, :].astype(F32), dm, gq)
            kn = _head_norm(k_ref[rows, :].astype(F32), dm, gk)
            vf = v_ref[rows, :].astype(F32)
            qb[par, rows, :] = jnp.where(dm, qn * QSCALE, 0.0).astype(BF16)
            kb[par, rows, :] = jnp.where(dm, kn, 0.0).astype(BF16)
            vb[par, rows, :] = jnp.where(dm, vf, jnp.where(lane_p == _ones_lane(par), 1.0, 0.0)).astype(BF16)
            return carry
        lax.fori_loop(0, n_prep, prep, 0)

    unroll = 4

    def grid_rows(o, carry):
        for i in range(unroll):
            r = o * unroll + i
            r0 = jnp.clip(r - kh // 2, 0, rows_total - kh)
            typ = jnp.where(r < kh // 2, r, jnp.where(r > rows_total - kh // 2, r - (rows_total - kh), kh // 2))
            qrows = pl.ds(pl.multiple_of(r * GRID_W, GRID_W), GRID_W)
            keys = pl.ds(pl.multiple_of(r0 * GRID_W, GRID_W), kh * GRID_W)
            outs = []
            for par in (0, 1):
                s = lax.dot_general(qb[par, qrows, :], kb[par, keys, :], (((1,), (1,)), ((), ())),
                                    preferred_element_type=F32)
                p = jnp.exp2(s + bias_ref[par, typ]).astype(BF16)
                acc = jnp.dot(p, vb[par, keys, :], preferred_element_type=F32)
                outs.append(_normalise(acc, lane_q, par))
            o_ref[qrows, :] = jnp.where(lane_q < HEAD_DIM, outs[0], outs[1]).astype(BF16)
        return carry
    lax.fori_loop(0, rows_total // unroll, grid_rows, 0)


def _attn_b(proj3, gains, bias):
    b, s, _ = proj3.shape
    n_pairs = WIDTH_B // LANES
    first = 3 * WIDTH_A // LANES
    n_types, kw = bias.shape[1], bias.shape[3]

    def slab(off):
        return pl.BlockSpec((None, s, LANES), lambda i, p: (i, 0, first + off + p))
    return pl.pallas_call(
        _attn_b_kernel, out_shape=jax.ShapeDtypeStruct((b, s, WIDTH_B), BF16), grid=(b, n_pairs),
        in_specs=[slab(0), slab(n_pairs), slab(2 * n_pairs),
                  pl.BlockSpec((2, LANES), lambda i, p: (0, 0)),
                  pl.BlockSpec((2, n_types, GRID_W, kw), lambda i, p: (p, 0, 0, 0))],
        out_specs=pl.BlockSpec((None, s, LANES), lambda i, p: (i, 0, p)),
        scratch_shapes=[pltpu.VMEM((2, s, LANES), BF16)] * 3,
        compiler_params=_cparams(("parallel", "parallel")), name="attn_b",
    )(proj3, proj3, proj3, gains, bias)


MEM_QROWS = 256


def _attn_m_kernel(q_ref, k_ref, v_ref, g_ref, shift_ref, o_ref):
    s_len = q_ref.shape[0]
    lane = lax.broadcasted_iota(jnp.int32, (MEM_QROWS, LANES), 1)
    gq = g_ref[0:1, :]

    def chunk(c, carry):
        rows = pl.ds(pl.multiple_of(c * MEM_QROWS, MEM_QROWS), MEM_QROWS)
        qf = q_ref[rows, :].astype(F32)
        outs = []
        for par in (0, 1):
            dm = _data_mask(lane, par)
            qn = jnp.where(dm, _head_norm(qf, dm, gq) * QSCALE, 0.0).astype(BF16)
            s = lax.dot_general(qn, k_ref[par], (((1,), (1,)), ((), ())), preferred_element_type=F32)
            p = jnp.exp2(s - shift_ref[...]).astype(BF16)
            acc = jnp.dot(p, v_ref[par], preferred_element_type=F32)
            outs.append(_normalise(acc, lane, par))
        o_ref[rows, :] = jnp.where(lane < HEAD_DIM, outs[0], outs[1]).astype(BF16)
        return carry
    lax.fori_loop(0, s_len // MEM_QROWS, chunk, 0)


def _attn_m(proj3, kaug, vaug, gains, shift):
    b, s, _ = proj3.shape
    m = kaug.shape[2]
    n_pairs = WIDTH_M // LANES
    first = 3 * (WIDTH_A + WIDTH_B) // LANES
    kv_spec = pl.BlockSpec((None, 2, m, LANES), lambda i, p: (i, p, 0, 0))
    return pl.pallas_call(
        _attn_m_kernel, out_shape=jax.ShapeDtypeStruct((b, s, WIDTH_M), BF16), grid=(b, n_pairs),
        in_specs=[pl.BlockSpec((None, s, LANES), lambda i, p: (i, 0, first + p)), kv_spec, kv_spec,
                  pl.BlockSpec((2, LANES), lambda i, p: (0, 0)),
                  pl.BlockSpec((1, m), lambda i, p: (0, 0))],
        out_specs=pl.BlockSpec((None, s, LANES), lambda i, p: (i, 0, p)),
        compiler_params=_cparams(("parallel", "parallel")), name="attn_m",
    )(proj3, kaug, vaug, gains, shift)


def _mix_kernel(oa_ref, ob_ref, om_ref, x_ref, og_ref, wo_ref, gf_ref, wr_ref, br_ref,
                x1_ref, h3_ref, route_ref):
    rows = x_ref.shape[0]
    y = None
    off = 0
    for o_ref in (oa_ref, ob_ref, om_ref):
        width = o_ref.shape[1]
        o = o_ref[...].astype(F32)
        ms = jnp.mean(o * o, axis=-1, keepdims=True)
        mixed = (o * lax.rsqrt(ms + EPS) * og_ref[:, off:off + width]).astype(BF16)
        part = jnp.dot(mixed, wo_ref[off:off + width, :], preferred_element_type=F32)
        y = part if y is None else y + part
        off += width
    x1 = x_ref[...] + y
    x1_ref[...] = x1
    ms = jnp.mean(x1 * x1, axis=-1, keepdims=True)
    h = x1 * lax.rsqrt(ms + EPS) * gf_ref[...]
    for j in range(ROW_TILES):
        h3_ref[pl.ds(j, rows, stride=ROW_TILES), :] = h[:, j * LANES:(j + 1) * LANES]

    h_hi = h.astype(BF16)
    h_lo = (h - h_hi.astype(F32)).astype(BF16)
    logits = (jnp.dot(h_hi, wr_ref[0], preferred_element_type=F32)
              + jnp.dot(h_lo, wr_ref[0], preferred_element_type=F32)
              + jnp.dot(h_hi, wr_ref[1], preferred_element_type=F32)) + br_ref[...]
    lane = lax.broadcasted_iota(jnp.int32, (rows, LANES), 1)
    lane_f = lane.astype(F32)
    neg = -jnp.inf
    far = float(LANES)
    is_g = (lane >> 2) == (N_EXPERTS // N_GROUPS)
    gl = jnp.where(is_g, logits, neg)
    gmax = jnp.max(gl, axis=-1, keepdims=True)
    gidx = jnp.min(jnp.where(gl == gmax, lane_f, far), axis=-1, keepdims=True) - float(N_EXPERTS)
    gsum = jnp.sum(jnp.where(is_g, jnp.exp(gl - gmax), 0.0), axis=-1, keepdims=True)
    g_top = 1.0 / gsum
    in_grp = (lane >> 3).astype(F32) == gidx
    el = jnp.where(in_grp, logits, neg)
    v1 = jnp.max(el, axis=-1, keepdims=True)
    i1 = jnp.min(jnp.where(el == v1, lane_f, far), axis=-1, keepdims=True)
    el2 = jnp.where(lane_f == i1, neg, el)
    v2 = jnp.max(el2, axis=-1, keepdims=True)
    i2 = jnp.min(jnp.where(el2 == v2, lane_f, far), axis=-1, keepdims=True)
    ev = jnp.exp(v2 - v1)
    w1 = g_top / (1.0 + ev)
    w2 = g_top * ev / (1.0 + ev)
    route_ref[...] = jnp.where(lane == 0, i1, jnp.where(lane == 1, i2, jnp.where(lane == 2, w1,
                               jnp.where(lane == 3, w2, 0.0))))


def _mix(oa, ob, om, x2, out_gain, w_out, gain_ffn, w_route, b_route):
    t = x2.shape[0]

    def rows(width):
        return pl.BlockSpec((PROJ_ROWS, width), lambda i: (i, 0))

    def whole(*shape):
        return pl.BlockSpec(shape, lambda i: (0,) * len(shape))
    return pl.pallas_call(
        _mix_kernel,
        out_shape=(jax.ShapeDtypeStruct((t, D_MODEL), F32),
                   jax.ShapeDtypeStruct((t * ROW_TILES, LANES), F32),
                   jax.ShapeDtypeStruct((t, LANES), F32)),
        grid=(t // PROJ_ROWS,),
        in_specs=[rows(WIDTH_A), rows(WIDTH_B), rows(WIDTH_M), rows(D_MODEL),
                  whole(1, D_MODEL), whole(D_MODEL, D_MODEL), whole(1, D_MODEL),
                  whole(2, D_MODEL, LANES), whole(1, LANES)],
        out_specs=(rows(D_MODEL), pl.BlockSpec((PROJ_ROWS * ROW_TILES, LANES), lambda i: (i, 0)), rows(LANES)),
        compiler_params=_cparams(("parallel",)), name="mix",
    )(oa, ob, om, x2, out_gain, w_out, gain_ffn, w_route, b_route)


SCATTER_UNROLL = 8


def _moe_kernel(texp_ref, ntile_ref, src_ref, dst_ref, gate_ref, h3_ref, wgu_ref, wd_ref, o_ref,
                xt, yt, acc):
    b = pl.program_id(0)
    t = pl.program_id(1)
    n_tok = o_ref.shape[0]

    @pl.when(t == 0)
    def _():
        acc[...] = jnp.zeros_like(acc)

    @pl.when(t < ntile_ref[b])
    def _():
        for i in range(MOE_TM):
            tok = src_ref[0, i]
            xt[i * ROW_TILES:(i + 1) * ROW_TILES, :] = h3_ref[pl.ds(pl.multiple_of(tok * ROW_TILES, ROW_TILES), ROW_TILES), :]
        lhs = jnp.concatenate([xt[pl.ds(j, MOE_TM, stride=ROW_TILES), :] for j in range(ROW_TILES)], axis=1)
        gu = jnp.dot(lhs.astype(BF16), wgu_ref[...], preferred_element_type=F32)
        g = gu[:, :D_EXPERT]
        hid = (g * (1.0 / (1.0 + jnp.exp(-g)))) * gu[:, D_EXPERT:]
        y = jnp.dot(hid.astype(BF16), wd_ref[...], preferred_element_type=F32)
        for j in range(ROW_TILES):
            yt[pl.ds(j, MOE_TM, stride=ROW_TILES), :] = y[:, j * LANES:(j + 1) * LANES]
        for i0 in range(0, MOE_TM, SCATTER_UNROLL):
            vals = []
            for i in range(i0, i0 + SCATTER_UNROLL):
                rows = pl.ds(pl.multiple_of(dst_ref[0, i] * ROW_TILES, ROW_TILES), ROW_TILES)
                vals.append((rows, acc[rows, :] + gate_ref[0, i] * yt[i * ROW_TILES:(i + 1) * ROW_TILES, :]))
            for rows, val in vals:
                acc[rows, :] = val

    @pl.when(t == pl.num_programs(1) - 1)
    def _():
        for j in range(ROW_TILES):
            o_ref[:, j * LANES:(j + 1) * LANES] = acc[pl.ds(j, n_tok, stride=ROW_TILES), :]


def _moe(h3, route, wgu, wd, block_tokens):
    t = route.shape[0]
    nblk = t // block_tokens
    n_assign = 2 * block_tokens
    nt_max = n_assign // MOE_TM + N_EXPERTS

    e = route[:, 0:2].astype(jnp.int32).reshape(nblk, n_assign)
    w = route[:, 2:4].reshape(nblk, n_assign)
    tok = jnp.arange(n_assign, dtype=jnp.int32) // 2
    onehot = (e[:, :, None] == jnp.arange(N_EXPERTS, dtype=jnp.int32)[None, None, :]).astype(jnp.int32)
    csum = jnp.cumsum(onehot, axis=1)
    rank = jnp.take_along_axis(csum - onehot, e[:, :, None], axis=2)[:, :, 0]
    cnt = csum[:, -1, :]
    ntile_e = (cnt + MOE_TM - 1) // MOE_TM
    tend = jnp.cumsum(ntile_e, axis=1)
    tstart = tend - ntile_e
    ntiles = tend[:, -1]
    ppos = jnp.take_along_axis(tstart, e, axis=1) * MOE_TM + rank
    bidx = jnp.arange(nblk, dtype=jnp.int32)[:, None]
    slots = nt_max * MOE_TM
    src = jnp.zeros((nblk, slots), jnp.int32).at[bidx, ppos].set(jnp.broadcast_to(tok, (nblk, n_assign)))
    dst = jnp.full((nblk, slots), block_tokens, jnp.int32).at[bidx, ppos].set(jnp.broadcast_to(tok, (nblk, n_assign)))
    gate = jnp.zeros((nblk, slots), F32).at[bidx, ppos].set(w)
    tile_ids = jnp.arange(nt_max, dtype=jnp.int32)[None, :]
    texp = jnp.sum((tile_ids[:, :, None] >= tend[:, None, :]).astype(jnp.int32), axis=2)
    last = jnp.take_along_axis(texp, jnp.maximum(ntiles - 1, 0)[:, None], axis=1)
    texp = jnp.where(tile_ids < ntiles[:, None], texp, last).astype(jnp.int32)

    def smem_tile():
        return pl.BlockSpec((None, 1, MOE_TM), lambda b, i, *_: (b * nt_max + i, 0, 0), memory_space=pltpu.SMEM)
    grid_spec = pltpu.PrefetchScalarGridSpec(
        num_scalar_prefetch=2, grid=(nblk, nt_max),
        in_specs=[smem_tile(), smem_tile(), smem_tile(),
                  pl.BlockSpec((block_tokens * ROW_TILES, LANES), lambda b, i, *_: (b, 0)),
                  pl.BlockSpec((None, D_MODEL, 2 * D_EXPERT), lambda b, i, te, nt: (te[b * nt_max + i], 0, 0)),
                  pl.BlockSpec((None, D_EXPERT, D_MODEL), lambda b, i, te, nt: (te[b * nt_max + i], 0, 0))],
        out_specs=pl.BlockSpec((block_tokens, D_MODEL), lambda b, i, *_: (b, 0)),
        scratch_shapes=[pltpu.VMEM((MOE_TM * ROW_TILES, LANES), F32),
                        pltpu.VMEM((MOE_TM * ROW_TILES, LANES), F32),
                        pltpu.VMEM(((block_tokens + 1) * ROW_TILES, LANES), F32)])
    return pl.pallas_call(
        _moe_kernel, out_shape=jax.ShapeDtypeStruct((t, D_MODEL), F32), grid_spec=grid_spec,
        compiler_params=_cparams(("arbitrary", "arbitrary")), name="moe",
    )(texp.reshape(-1), ntiles.astype(jnp.int32),
      src.reshape(nblk * nt_max, 1, MOE_TM), dst.reshape(nblk * nt_max, 1, MOE_TM),
      gate.reshape(nblk * nt_max, 1, MOE_TM), h3, wgu, wd)


def _add_kernel(a_ref, b_ref, o_ref):
    o_ref[...] = a_ref[...] + b_ref[...]


def _add(a, b):
    t = a.shape[0]
    spec = pl.BlockSpec((PROJ_ROWS, D_MODEL), lambda i: (i, 0))
    return pl.pallas_call(_add_kernel, out_shape=jax.ShapeDtypeStruct(a.shape, a.dtype), grid=(t // PROJ_ROWS,),
                          in_specs=[spec, spec], out_specs=spec, compiler_params=_cparams(("parallel",)),
                          name="residual_add")(a, b)


def _score_bound(gq, gk):
    return (HEAD_DIM * ATTN_SCALE * LOG2E) * jnp.max(jnp.abs(gq)) * jnp.max(jnp.abs(gk))


def _alibi_tables(shift):
    slopes = (2.0 ** (-8.0 * np.arange(1, N_HEADS_A + 1) / N_HEADS_A)).astype(np.float32)
    tables = []
    for _, dil in DILATED_PATTERNS:
        if dil == DILATED_PATTERNS[-1][1]:
            offs, kw = (0,), QBLK
        else:
            offs, kw = (0, DIL_RADIUS, 2 * DIL_RADIUS), 2 * QBLK
        tiles = []
        for off in offs:
            rel = np.abs((off + np.arange(QBLK))[:, None] - np.arange(kw)[None, :])
            tiles.append(np.where(rel <= DIL_RADIUS, rel * dil, np.inf).astype(np.float32))
        dist = np.stack(tiles)
        tab = -(slopes * LOG2E)[:, None, None, None] * dist[None]
        tab = jnp.asarray(tab) - shift
        tables.append(tab[:, 0] if len(offs) == 1 else tab)
    return tables


def _rpb_table(rpb, shift, rows_total):
    kh = min(NA_ROWS, rows_total)
    c = np.arange(GRID_W)
    c0 = np.clip(c - NA_COLS // 2, 0, GRID_W - NA_COLS)
    col_ok = (c[None, :] >= c0[:, None]) & (c[None, :] < c0[:, None] + NA_COLS)
    dc = np.clip(c[None, :] - c[:, None], -(NA_COLS - 1), NA_COLS - 1) + NA_COLS - 1
    type_rows = list(range(kh // 2)) + [kh // 2] + list(range(rows_total - kh // 2 + 1, rows_total))
    tiles = []
    for r in type_rows:
        r0 = int(np.clip(r - kh // 2, 0, rows_total - kh))
        per_row = []
        for j in range(kh):
            dr = r0 + j - r + NA_ROWS - 1
            per_row.append(jnp.where(col_ok[None], rpb[:, dr][:, dc], -jnp.inf))
        tiles.append(jnp.concatenate(per_row, axis=2))
    tab = jnp.stack(tiles, axis=1) * LOG2E
    return tab - shift[:, None, None, None]


def kernel(x, mem, norm_mix, w_in, qk_gain, rpb, norm_mem, w_mem_kv, out_gain, w_out, norm_ffn, w_group,
           b_group, w_router, b_router, w_gate, w_up, w_down):
    bsz, seq, d = x.shape
    t = bsz * seq
    depth = w_in.shape[0]
    rows_total = seq // GRID_W

    x2 = x.reshape(t, d)
    moe = None
    for l in range(depth):
        gains = jnp.tile(qk_gain[l], (1, 2))
        w_in_l = w_in[l].astype(BF16)
        if moe is None:
            proj = _proj(x2, None, norm_mix[l][None, :], w_in_l)
        else:
            x2, proj = _proj(x2, moe, norm_mix[l][None, :], w_in_l)
        proj3 = proj.reshape(bsz, seq, IN_WIDTH)

        shift_a = _score_bound(qk_gain[l, 0], qk_gain[l, 1])
        b1, b4, b16 = _alibi_tables(shift_a)
        oa = _attn_a(proj3, gains[0:2], b1, b4, b16)

        shift_b = _score_bound(qk_gain[l, 2], qk_gain[l, 3]) + LOG2E * jnp.max(rpb[l], axis=(1, 2))
        ob = _attn_b(proj3, gains[2:4], _rpb_table(rpb[l], shift_b, rows_total))

        kaug, vaug = _memkv(mem, norm_mem[l][None, :], w_mem_kv[l].astype(BF16), gains[5:6])
        shift_m = jnp.full((1, mem.shape[1]), _score_bound(qk_gain[l, 4], qk_gain[l, 5]), F32)
        om = _attn_m(proj3, kaug, vaug, gains[4:6], shift_m)

        w_route = jnp.zeros((d, LANES), F32).at[:, :N_EXPERTS].set(w_router[l])
        w_route = w_route.at[:, N_EXPERTS:N_EXPERTS + N_GROUPS].set(w_group[l])
        w_route_hi = w_route.astype(BF16)
        w_route_lo = (w_route - w_route_hi.astype(F32)).astype(BF16)
        b_route = jnp.zeros((1, LANES), F32).at[0, :N_EXPERTS].set(b_router[l])
        b_route = b_route.at[0, N_EXPERTS:N_EXPERTS + N_GROUPS].set(b_group[l])
        x2, h3, route = _mix(oa.reshape(t, WIDTH_A), ob.reshape(t, WIDTH_B), om.reshape(t, WIDTH_M), x2,
                             out_gain[l][None, :], w_out[l].astype(BF16), norm_ffn[l][None, :],
                             jnp.stack([w_route_hi, w_route_lo]), b_route)

        wgu = jnp.concatenate([w_gate[l], w_up[l]], axis=-1).astype(BF16)
        moe = _moe(h3, route, wgu, w_down[l].astype(BF16), seq)
    return _add(x2, moe).reshape(bsz, seq, d)
```

```python
import functools

import numpy as np
import jax
import jax.numpy as jnp
from jax import lax
from jax.experimental import pallas as pl
from jax.experimental.pallas import tpu as pltpu

F32 = jnp.float32
BF16 = jnp.bfloat16

D_MODEL = 1024
HEAD_DIM = 64
N_HEADS_A = 6
N_HEADS_B = 6
N_HEADS_M = 4
WIDTH_A = N_HEADS_A * HEAD_DIM
WIDTH_B = N_HEADS_B * HEAD_DIM
WIDTH_M = N_HEADS_M * HEAD_DIM
IN_WIDTH = 3 * WIDTH_A + 3 * WIDTH_B + WIDTH_M
DILATED_PATTERNS = ((128, 1), (512, 4), (2048, 16))
DIL_RADIUS = 64
GRID_W = 64
NA_ROWS = 8
NA_COLS = 16
N_GROUPS = 4
EXPERTS_PER_GROUP = 8
N_EXPERTS = N_GROUPS * EXPERTS_PER_GROUP
D_EXPERT = 512
EPS = 1e-6
ATTN_SCALE = HEAD_DIM ** -0.5
LOG2E = float(np.log2(np.e))
QSCALE = ATTN_SCALE * LOG2E

LANES = 128
SUBLANES = 8
ROW_TILES = D_MODEL // LANES
QBLK = 128
PROJ_ROWS = 512
MOE_TM = 128
VMEM_LIMIT = 56 * 1024 * 1024


def _cparams(sem):
    return pltpu.CompilerParams(dimension_semantics=sem, vmem_limit_bytes=VMEM_LIMIT)


def _proj_kernel(*refs, add):
    if add:
        x_ref, m_ref, g_ref, w_ref, xo_ref, o_ref = refs
        x = x_ref[...] + m_ref[...]
        xo_ref[...] = x
    else:
        x_ref, g_ref, w_ref, o_ref = refs
        x = x_ref[...]
    ms = jnp.mean(x * x, axis=-1, keepdims=True)
    h = (x * lax.rsqrt(ms + EPS) * g_ref[...]).astype(BF16)
    o_ref[...] = jnp.dot(h, w_ref[...], preferred_element_type=F32).astype(BF16)


def _proj(x2, moe2, gain, w):
    t = x2.shape[0]
    add = moe2 is not None
    row_spec = pl.BlockSpec((PROJ_ROWS, D_MODEL), lambda i: (i, 0))
    in_specs = [row_spec] + ([row_spec] if add else []) + [
        pl.BlockSpec((1, D_MODEL), lambda i: (0, 0)),
        pl.BlockSpec((D_MODEL, IN_WIDTH), lambda i: (0, 0)),
    ]
    proj_spec = pl.BlockSpec((PROJ_ROWS, IN_WIDTH), lambda i: (i, 0))
    proj_shape = jax.ShapeDtypeStruct((t, IN_WIDTH), BF16)
    if add:
        out_shape = (jax.ShapeDtypeStruct((t, D_MODEL), F32), proj_shape)
        out_specs = (row_spec, proj_spec)
        args = (x2, moe2, gain, w)
    else:
        out_shape, out_specs, args = proj_shape, proj_spec, (x2, gain, w)
    return pl.pallas_call(
        functools.partial(_proj_kernel, add=add),
        out_shape=out_shape, grid=(t // PROJ_ROWS,), in_specs=in_specs, out_specs=out_specs,
        compiler_params=_cparams(("parallel",)), name="proj_add" if add else "proj",
    )(*args)


def _data_mask(lane, par):
    return (lane < HEAD_DIM) if par == 0 else (lane >= HEAD_DIM)


def _ones_lane(par):
    return HEAD_DIM if par == 0 else 0


def _head_norm(x, dm, gain):
    ss = jnp.sum(jnp.where(dm, x * x, 0.0), axis=-1, keepdims=True)
    return x * lax.rsqrt(ss * (1.0 / HEAD_DIM) + EPS) * gain


def _normalise(tot, lane, par):
    den = jnp.sum(jnp.where(lane == _ones_lane(par), tot, 0.0), axis=-1, keepdims=True)
    return tot / den


def _memkv_kernel(m_ref, g_ref, w_ref, kg_ref, k_ref, v_ref):
    x = m_ref[...]
    ms = jnp.mean(x * x, axis=-1, keepdims=True)
    h = (x * lax.rsqrt(ms + EPS) * g_ref[...]).astype(BF16)
    kv = jnp.dot(h, w_ref[...], preferred_element_type=F32)
    lane = lax.broadcasted_iota(jnp.int32, (x.shape[0], LANES), 1)
    for slab in range(WIDTH_M // LANES):
        kf = kv[:, slab * LANES:(slab + 1) * LANES]
        vf = kv[:, WIDTH_M + slab * LANES:WIDTH_M + (slab + 1) * LANES]
        for par in (0, 1):
            dm = _data_mask(lane, par)
            kn = _head_norm(kf, dm, kg_ref[...])
            k_ref[2 * slab + par] = jnp.where(dm, kn, 0.0).astype(BF16)
            v_ref[2 * slab + par] = jnp.where(dm, vf, jnp.where(lane == _ones_lane(par), 1.0, 0.0)).astype(BF16)


def _memkv(mem, gain, w, kgain):
    b, m, _ = mem.shape
    kv_shape = jax.ShapeDtypeStruct((b, N_HEADS_M, m, LANES), BF16)
    kv_spec = pl.BlockSpec((None, N_HEADS_M, m, LANES), lambda i: (i, 0, 0, 0))
    return pl.pallas_call(
        _memkv_kernel, out_shape=(kv_shape, kv_shape), grid=(b,),
        in_specs=[pl.BlockSpec((None, m, D_MODEL), lambda i: (i, 0, 0)),
                  pl.BlockSpec((1, D_MODEL), lambda i: (0, 0)),
                  pl.BlockSpec((D_MODEL, 2 * WIDTH_M), lambda i: (0, 0)),
                  pl.BlockSpec((1, LANES), lambda i: (0, 0))],
        out_specs=(kv_spec, kv_spec), compiler_params=_cparams(("parallel",)), name="memkv",
    )(mem, gain, w, kgain)


PREP_ROWS = 256


def _attn_a_kernel(q_ref, k_ref, v_ref, g_ref, b1_ref, b4_ref, b16_ref, o_ref,
                   qs, ks, vs, qd, kd, vd, accd, tot):
    s_len = q_ref.shape[0]
    n_prep = s_len // PREP_ROWS
    lane_p = lax.broadcasted_iota(jnp.int32, (PREP_ROWS, LANES), 1)
    gq = g_ref[0:1, :]
    gk = g_ref[1:2, :]

    for par in (0, 1):
        def prep(c, carry, par=par):
            rows = pl.ds(pl.multiple_of(c * PREP_ROWS, PREP_ROWS), PREP_ROWS)
            dm = _data_mask(lane_p, par)
            qn = _head_norm(q_ref[rows, :].astype(F32), dm, gq)
            kn = _head_norm(k_ref[rows, :].astype(F32), dm, gk)
            vf = v_ref[rows, :].astype(F32)
            qs[rows, :] = jnp.where(dm, qn * QSCALE, 0.0)
            ks[rows, :] = jnp.where(dm, kn, 0.0)
            vs[rows, :] = jnp.where(dm, vf, jnp.where(lane_p == _ones_lane(par), 1.0, 0.0))
            return carry
        lax.fori_loop(0, n_prep, prep, 0)

        for (_, dil), b_ref in zip(DILATED_PATTERNS, (b1_ref, b4_ref, b16_ref)):
            length = s_len // dil
            nblk = length // QBLK
            if dil == 1:
                def cast(c, carry):
                    rows = pl.ds(pl.multiple_of(c * PREP_ROWS, PREP_ROWS), PREP_ROWS)
                    qd[rows, :] = qs[rows, :].astype(BF16)
                    kd[rows, :] = ks[rows, :].astype(BF16)
                    vd[rows, :] = vs[rows, :].astype(BF16)
                    return carry
                lax.fori_loop(0, n_prep, cast, 0)
            else:
                def deint(r, carry, dil=dil, length=length):
                    dst = pl.ds(pl.multiple_of(r * length, QBLK), length)
                    src = pl.ds(r, length, stride=dil)
                    qd[dst, :] = qs[src, :].astype(BF16)
                    kd[dst, :] = ks[src, :].astype(BF16)
                    vd[dst, :] = vs[src, :].astype(BF16)
                    return carry
                lax.fori_loop(0, dil, deint, 0)

            unroll = 4
            n_units = dil * nblk

            def blocks(o, carry, par=par, dil=dil, length=length, nblk=nblk, b_ref=b_ref):
                for i in range(unroll):
                    u = o * unroll + i
                    if nblk == 1:
                        base = pl.multiple_of(u * length, QBLK)
                        qrows = pl.ds(base, QBLK)
                        keys = pl.ds(base, QBLK)
                        bias = b_ref[par]
                    else:
                        r = u // nblk
                        blk = u % nblk
                        q0 = blk * QBLK
                        ws = jnp.clip(q0 - DIL_RADIUS, 0, length - 2 * QBLK)
                        tid = jnp.where(blk == 0, 0, jnp.where(blk == nblk - 1, 2, 1))
                        qrows = pl.ds(pl.multiple_of(r * length + q0, QBLK), QBLK)
                        keys = pl.ds(pl.multiple_of(r * length + ws, DIL_RADIUS), 2 * QBLK)
                        bias = b_ref[par, tid]
                    s = lax.dot_general(qd[qrows, :], kd[keys, :], (((1,), (1,)), ((), ())),
                                        preferred_element_type=F32)
                    p = jnp.exp2(s + bias).astype(BF16)
                    accd[qrows, :] = jnp.dot(p, vd[keys, :], preferred_element_type=F32)
                return carry
            lax.fori_loop(0, n_units // unroll, blocks, 0)

            if dil == 1:
                def first(c, carry, par=par):
                    rows = pl.ds(pl.multiple_of(c * PREP_ROWS, PREP_ROWS), PREP_ROWS)
                    tot[par, rows, :] = accd[rows, :]
                    return carry
                lax.fori_loop(0, n_prep, first, 0)
            else:
                def merge(r, carry, par=par, dil=dil, length=length):
                    src = pl.ds(pl.multiple_of(r * length, QBLK), length)
                    dst = pl.ds(r, length, stride=dil)
                    tot[par, dst, :] = tot[par, dst, :] + accd[src, :]
                    return carry
                lax.fori_loop(0, dil, merge, 0)

    def fin(c, carry):
        rows = pl.ds(pl.multiple_of(c * PREP_ROWS, PREP_ROWS), PREP_ROWS)
        o0 = _normalise(tot[0, rows, :], lane_p, 0)
        o1 = _normalise(tot[1, rows, :], lane_p, 1)
        o_ref[rows, :] = jnp.where(lane_p < HEAD_DIM, o0, o1).astype(BF16)
        return carry
    lax.fori_loop(0, n_prep, fin, 0)


def _attn_a(proj3, gains, b1, b4, b16):
    b, s, _ = proj3.shape
    n_pairs = WIDTH_A // LANES

    def slab(off):
        return pl.BlockSpec((None, s, LANES), lambda i, p: (i, 0, off + p))
    return pl.pallas_call(
        _attn_a_kernel, out_shape=jax.ShapeDtypeStruct((b, s, WIDTH_A), BF16), grid=(b, n_pairs),
        in_specs=[slab(0), slab(n_pairs), slab(2 * n_pairs),
                  pl.BlockSpec((2, LANES), lambda i, p: (0, 0)),
                  pl.BlockSpec((2, 3, QBLK, 2 * QBLK), lambda i, p: (p, 0, 0, 0)),
                  pl.BlockSpec((2, 3, QBLK, 2 * QBLK), lambda i, p: (p, 0, 0, 0)),
                  pl.BlockSpec((2, QBLK, QBLK), lambda i, p: (p, 0, 0))],
        out_specs=pl.BlockSpec((None, s, LANES), lambda i, p: (i, 0, p)),
        scratch_shapes=[pltpu.VMEM((s, LANES), F32)] * 3 + [pltpu.VMEM((s, LANES), BF16)] * 3
        + [pltpu.VMEM((s, LANES), F32), pltpu.VMEM((2, s, LANES), F32)],
        compiler_params=_cparams(("parallel", "parallel")), name="attn_a",
    )(proj3, proj3, proj3, gains, b1, b4, b16)


def _attn_b_kernel(q_ref, k_ref, v_ref, g_ref, bias_ref, o_ref, qb, kb, vb):
    s_len = q_ref.shape[0]
    n_prep = s_len // PREP_ROWS
    rows_total = s_len // GRID_W
    kh = min(NA_ROWS, rows_total)
    lane_p = lax.broadcasted_iota(jnp.int32, (PREP_ROWS, LANES), 1)
    lane_q = lax.broadcasted_iota(jnp.int32, (GRID_W, LANES), 1)
    gq = g_ref[0:1, :]
    gk = g_ref[1:2, :]

    for par in (0, 1):
        def prep(c, carry, par=par):
            rows = pl.ds(pl.multiple_of(c * PREP_ROWS, PREP_ROWS), PREP_ROWS)
            dm = _data_mask(lane_p, par)
            qn = _head_norm(q_ref[rows, :].astype(F32), dm, gq)
            kn = _head_norm(k_ref[rows, :].astype(F32), dm, gk)
            vf = v_ref[rows, :].astype(F32)
            qb[par, rows, :] = jnp.where(dm, qn * QSCALE, 0.0).astype(BF16)
            kb[par, rows, :] = jnp.where(dm, kn, 0.0).astype(BF16)
            vb[par, rows, :] = jnp.where(dm, vf, jnp.where(lane_p == _ones_lane(par), 1.0, 0.0)).astype(BF16)
            return carry
        lax.fori_loop(0, n_prep, prep, 0)

    unroll = 4

    def grid_rows(o, carry):
        for i in range(unroll):
            r = o * unroll + i
            r0 = jnp.clip(r - kh // 2, 0, rows_total - kh)
            typ = jnp.where(r < kh // 2, r, jnp.where(r > rows_total - kh // 2, r - (rows_total - kh), kh // 2))
            qrows = pl.ds(pl.multiple_of(r * GRID_W, GRID_W), GRID_W)
            keys = pl.ds(pl.multiple_of(r0 * GRID_W, GRID_W), kh * GRID_W)
            outs = []
            for par in (0, 1):
                s = lax.dot_general(qb[par, qrows, :], kb[par, keys, :], (((1,), (1,)), ((), ())),
                                    preferred_element_type=F32)
                p = jnp.exp2(s + bias_ref[par, typ]).astype(BF16)
                acc = jnp.dot(p, vb[par, keys, :], preferred_element_type=F32)
                outs.append(_normalise(acc, lane_q, par))
            o_ref[qrows, :] = jnp.where(lane_q < HEAD_DIM, outs[0], outs[1]).astype(BF16)
        return carry
    lax.fori_loop(0, rows_total // unroll, grid_rows, 0)


def _attn_b(proj3, gains, bias):
    b, s, _ = proj3.shape
    n_pairs = WIDTH_B // LANES
    first = 3 * WIDTH_A // LANES
    n_types, kw = bias.shape[1], bias.shape[3]

    def slab(off):
        return pl.BlockSpec((None, s, LANES), lambda i, p: (i, 0, first + off + p))
    return pl.pallas_call(
        _attn_b_kernel, out_shape=jax.ShapeDtypeStruct((b, s, WIDTH_B), BF16), grid=(b, n_pairs),
        in_specs=[slab(0), slab(n_pairs), slab(2 * n_pairs),
                  pl.BlockSpec((2, LANES), lambda i, p: (0, 0)),
                  pl.BlockSpec((2, n_types, GRID_W, kw), lambda i, p: (p, 0, 0, 0))],
        out_specs=pl.BlockSpec((None, s, LANES), lambda i, p: (i, 0, p)),
        scratch_shapes=[pltpu.VMEM((2, s, LANES), BF16)] * 3,
        compiler_params=_cparams(("parallel", "parallel")), name="attn_b",
    )(proj3, proj3, proj3, gains, bias)


MEM_QROWS = 256


def _attn_m_kernel(q_ref, k_ref, v_ref, g_ref, shift_ref, o_ref):
    s_len = q_ref.shape[0]
    lane = lax.broadcasted_iota(jnp.int32, (MEM_QROWS, LANES), 1)
    gq = g_ref[0:1, :]

    def chunk(c, carry):
        rows = pl.ds(pl.multiple_of(c * MEM_QROWS, MEM_QROWS), MEM_QROWS)
        qf = q_ref[rows, :].astype(F32)
        outs = []
        for par in (0, 1):
            dm = _data_mask(lane, par)
            qn = jnp.where(dm, _head_norm(qf, dm, gq) * QSCALE, 0.0).astype(BF16)
            s = lax.dot_general(qn, k_ref[par], (((1,), (1,)), ((), ())), preferred_element_type=F32)
            p = jnp.exp2(s - shift_ref[...]).astype(BF16)
            acc = jnp.dot(p, v_ref[par], preferred_element_type=F32)
            outs.append(_normalise(acc, lane, par))
        o_ref[rows, :] = jnp.where(lane < HEAD_DIM, outs[0], outs[1]).astype(BF16)
        return carry
    lax.fori_loop(0, s_len // MEM_QROWS, chunk, 0)


def _attn_m(proj3, kaug, vaug, gains, shift):
    b, s, _ = proj3.shape
    m = kaug.shape[2]
    n_pairs = WIDTH_M // LANES
    first = 3 * (WIDTH_A + WIDTH_B) // LANES
    kv_spec = pl.BlockSpec((None, 2, m, LANES), lambda i, p: (i, p, 0, 0))
    return pl.pallas_call(
        _attn_m_kernel, out_shape=jax.ShapeDtypeStruct((b, s, WIDTH_M), BF16), grid=(b, n_pairs),
        in_specs=[pl.BlockSpec((None, s, LANES), lambda i, p: (i, 0, first + p)), kv_spec, kv_spec,
                  pl.BlockSpec((2, LANES), lambda i, p: (0, 0)),
                  pl.BlockSpec((1, m), lambda i, p: (0, 0))],
        out_specs=pl.BlockSpec((None, s, LANES), lambda i, p: (i, 0, p)),
        compiler_params=_cparams(("parallel", "parallel")), name="attn_m",
    )(proj3, kaug, vaug, gains, shift)


def _mix_kernel(oa_ref, ob_ref, om_ref, x_ref, og_ref, wo_ref, gf_ref, wr_ref, br_ref,
                x1_ref, h3_ref, route_ref):
    rows = x_ref.shape[0]
    y = None
    off = 0
    for o_ref in (oa_ref, ob_ref, om_ref):
        width = o_ref.shape[1]
        o = o_ref[...].astype(F32)
        ms = jnp.mean(o * o, axis=-1, keepdims=True)
        mixed = (o * lax.rsqrt(ms + EPS) * og_ref[:, off:off + width]).astype(BF16)
        part = jnp.dot(mixed, wo_ref[off:off + width, :], preferred_element_type=F32)
        y = part if y is None else y + part
        off += width
    x1 = x_ref[...] + y
    x1_ref[...] = x1
    ms = jnp.mean(x1 * x1, axis=-1, keepdims=True)
    h = x1 * lax.rsqrt(ms + EPS) * gf_ref[...]
    for j in range(ROW_TILES):
        h3_ref[pl.ds(j, rows, stride=ROW_TILES), :] = h[:, j * LANES:(j + 1) * LANES]

    h_hi = h.astype(BF16)
    h_lo = (h - h_hi.astype(F32)).astype(BF16)
    logits = (jnp.dot(h_hi, wr_ref[0], preferred_element_type=F32)
              + jnp.dot(h_lo, wr_ref[0], preferred_element_type=F32)
              + jnp.dot(h_hi, wr_ref[1], preferred_element_type=F32)) + br_ref[...]
    lane = lax.broadcasted_iota(jnp.int32, (rows, LANES), 1)
    lane_f = lane.astype(F32)
    neg = -jnp.inf
    far = float(LANES)
    is_g = (lane >> 2) == (N_EXPERTS // N_GROUPS)
    gl = jnp.where(is_g, logits, neg)
    gmax = jnp.max(gl, axis=-1, keepdims=True)
    gidx = jnp.min(jnp.where(gl == gmax, lane_f, far), axis=-1, keepdims=True) - float(N_EXPERTS)
    gsum = jnp.sum(jnp.where(is_g, jnp.exp(gl - gmax), 0.0), axis=-1, keepdims=True)
    g_top = 1.0 / gsum
    in_grp = (lane >> 3).astype(F32) == gidx
    el = jnp.where(in_grp, logits, neg)
    v1 = jnp.max(el, axis=-1, keepdims=True)
    i1 = jnp.min(jnp.where(el == v1, lane_f, far), axis=-1, keepdims=True)
    el2 = jnp.where(lane_f == i1, neg, el)
    v2 = jnp.max(el2, axis=-1, keepdims=True)
    i2 = jnp.min(jnp.where(el2 == v2, lane_f, far), axis=-1, keepdims=True)
    ev = jnp.exp(v2 - v1)
    w1 = g_top / (1.0 + ev)
    w2 = g_top * ev / (1.0 + ev)
    route_ref[...] = jnp.where(lane == 0, i1, jnp.where(lane == 1, i2, jnp.where(lane == 2, w1,
                               jnp.where(lane == 3, w2, 0.0))))


def _mix(oa, ob, om, x2, out_gain, w_out, gain_ffn, w_route, b_route):
    t = x2.shape[0]

    def rows(width):
        return pl.BlockSpec((PROJ_ROWS, width), lambda i: (i, 0))

    def whole(*shape):
        return pl.BlockSpec(shape, lambda i: (0,) * len(shape))
    return pl.pallas_call(
        _mix_kernel,
        out_shape=(jax.ShapeDtypeStruct((t, D_MODEL), F32),
                   jax.ShapeDtypeStruct((t * ROW_TILES, LANES), F32),
                   jax.ShapeDtypeStruct((t, LANES), F32)),
        grid=(t // PROJ_ROWS,),
        in_specs=[rows(WIDTH_A), rows(WIDTH_B), rows(WIDTH_M), rows(D_MODEL),
                  whole(1, D_MODEL), whole(D_MODEL, D_MODEL), whole(1, D_MODEL),
                  whole(2, D_MODEL, LANES), whole(1, LANES)],
        out_specs=(rows(D_MODEL), pl.BlockSpec((PROJ_ROWS * ROW_TILES, LANES), lambda i: (i, 0)), rows(LANES)),
        compiler_params=_cparams(("parallel",)), name="mix",
    )(oa, ob, om, x2, out_gain, w_out, gain_ffn, w_route, b_route)


SCATTER_UNROLL = 8


def _moe_weight_copies(wgu_hbm, wd_hbm, wgu_buf, wd_buf, sems, e, slot):
    return (pltpu.make_async_copy(wgu_hbm.at[e], wgu_buf.at[slot], sems.at[0, slot]),
            pltpu.make_async_copy(wd_hbm.at[e], wd_buf.at[slot], sems.at[1, slot]))


def _moe_kernel(cstart_ref, cnt_ref, tok_ref, gate_ref, h3_ref, wgu_hbm, wd_hbm, o_ref,
                wgu_buf, wd_buf, sems, xt, yt, acc):
    b = pl.program_id(0)
    n_tok = o_ref.shape[0]
    copies = functools.partial(_moe_weight_copies, wgu_hbm, wd_hbm, wgu_buf, wd_buf, sems)

    for c in copies(0, 0):
        c.start()
    acc[...] = jnp.zeros_like(acc)

    def expert(e, slot):
        for c in copies(e, slot):
            c.wait()

        @pl.when(e + 1 < N_EXPERTS)
        def _():
            for c in copies(e + 1, 1 - slot):
                c.start()

        first = cstart_ref[b * N_EXPERTS + e]
        count = cnt_ref[b * N_EXPERTS + e]

        def tile(j, carry):
            rs = first + j * MOE_TM
            valid = count - j * MOE_TM
            toks = [tok_ref[0, rs + i] for i in range(MOE_TM)]
            for i in range(MOE_TM):
                xt[i * ROW_TILES:(i + 1) * ROW_TILES, :] = h3_ref[pl.ds(pl.multiple_of(toks[i] * ROW_TILES, ROW_TILES), ROW_TILES), :]
            lhs = jnp.concatenate([xt[pl.ds(k, MOE_TM, stride=ROW_TILES), :] for k in range(ROW_TILES)], axis=1)
            gu = jnp.dot(lhs.astype(BF16), wgu_buf[slot], preferred_element_type=F32)
            g = gu[:, :D_EXPERT]
            hid = (g * (1.0 / (1.0 + jnp.exp(-g)))) * gu[:, D_EXPERT:]
            y = jnp.dot(hid.astype(BF16), wd_buf[slot], preferred_element_type=F32)
            for k in range(ROW_TILES):
                yt[pl.ds(k, MOE_TM, stride=ROW_TILES), :] = y[:, k * LANES:(k + 1) * LANES]
            for i0 in range(0, MOE_TM, SCATTER_UNROLL):
                vals = []
                for i in range(i0, i0 + SCATTER_UNROLL):
                    live = i < valid
                    dst = jnp.where(live, toks[i], n_tok)
                    gate = jnp.where(live, gate_ref[0, rs + i], 0.0)
                    rows = pl.ds(pl.multiple_of(dst * ROW_TILES, ROW_TILES), ROW_TILES)
                    vals.append((rows, acc[rows, :] + gate * yt[i * ROW_TILES:(i + 1) * ROW_TILES, :]))
                for rows, val in vals:
                    acc[rows, :] = val
            return carry
        lax.fori_loop(0, (count + MOE_TM - 1) // MOE_TM, tile, 0)

    def expert_pair(k, carry):
        expert(2 * k, 0)
        expert(2 * k + 1, 1)
        return carry
    lax.fori_loop(0, N_EXPERTS // 2, expert_pair, 0)

    for k in range(ROW_TILES):
        o_ref[:, k * LANES:(k + 1) * LANES] = acc[pl.ds(k, n_tok, stride=ROW_TILES), :]


def _moe(h3, route, wgu, wd, block_tokens):
    t = route.shape[0]
    nblk = t // block_tokens
    n_assign = 2 * block_tokens

    e = route[:, 0:2].astype(jnp.int32).reshape(nblk, n_assign)
    w = route[:, 2:4].reshape(nblk, n_assign)
    key = e * n_assign + jnp.arange(n_assign, dtype=jnp.int32)[None, :]
    key_s, w_s = lax.sort((key, w), dimension=1, num_keys=1)
    tok_s = (key_s & (n_assign - 1)) >> 1
    pad_i = jnp.zeros((nblk, MOE_TM), jnp.int32)
    tok_s = jnp.concatenate([tok_s, pad_i], axis=1)
    w_s = jnp.concatenate([w_s, pad_i.astype(F32)], axis=1)
    cnt = jnp.sum((e[:, :, None] == jnp.arange(N_EXPERTS, dtype=jnp.int32)[None, None, :]).astype(jnp.int32), axis=1)
    cstart = jnp.cumsum(cnt, axis=1) - cnt

    def smem_list():
        return pl.BlockSpec((None, 1, n_assign + MOE_TM), lambda b, *_: (b, 0, 0), memory_space=pltpu.SMEM)
    grid_spec = pltpu.PrefetchScalarGridSpec(
        num_scalar_prefetch=2, grid=(nblk,),
        in_specs=[smem_list(), smem_list(),
                  pl.BlockSpec((block_tokens * ROW_TILES, LANES), lambda b, *_: (b, 0)),
                  pl.BlockSpec(memory_space=pl.ANY), pl.BlockSpec(memory_space=pl.ANY)],
        out_specs=pl.BlockSpec((block_tokens, D_MODEL), lambda b, *_: (b, 0)),
        scratch_shapes=[pltpu.VMEM((2, D_MODEL, 2 * D_EXPERT), BF16),
                        pltpu.VMEM((2, D_EXPERT, D_MODEL), BF16),
                        pltpu.SemaphoreType.DMA((2, 2)),
                        pltpu.VMEM((MOE_TM * ROW_TILES, LANES), F32),
                        pltpu.VMEM((MOE_TM * ROW_TILES, LANES), F32),
                        pltpu.VMEM(((block_tokens + 1) * ROW_TILES, LANES), F32)])
    return pl.pallas_call(
        _moe_kernel, out_shape=jax.ShapeDtypeStruct((t, D_MODEL), F32), grid_spec=grid_spec,
        compiler_params=_cparams(("arbitrary",)), name="moe",
    )(cstart.reshape(-1).astype(jnp.int32), cnt.reshape(-1).astype(jnp.int32),
      tok_s.reshape(nblk, 1, n_assign + MOE_TM), w_s.reshape(nblk, 1, n_assign + MOE_TM), h3, wgu, wd)


def _add_kernel(a_ref, b_ref, o_ref):
    o_ref[...] = a_ref[...] + b_ref[...]


def _add(a, b):
    t = a.shape[0]
    spec = pl.BlockSpec((PROJ_ROWS, D_MODEL), lambda i: (i, 0))
    return pl.pallas_call(_add_kernel, out_shape=jax.ShapeDtypeStruct(a.shape, a.dtype), grid=(t // PROJ_ROWS,),
                          in_specs=[spec, spec], out_specs=spec, compiler_params=_cparams(("parallel",)),
                          name="residual_add")(a, b)


def _score_bound(gq, gk):
    return (HEAD_DIM * ATTN_SCALE * LOG2E) * jnp.max(jnp.abs(gq)) * jnp.max(jnp.abs(gk))


def _alibi_tables(shift):
    slopes = (2.0 ** (-8.0 * np.arange(1, N_HEADS_A + 1) / N_HEADS_A)).astype(np.float32)
    tables = []
    for _, dil in DILATED_PATTERNS:
        if dil == DILATED_PATTERNS[-1][1]:
            offs, kw = (0,), QBLK
        else:
            offs, kw = (0, DIL_RADIUS, 2 * DIL_RADIUS), 2 * QBLK
        tiles = []
        for off in offs:
            rel = np.abs((off + np.arange(QBLK))[:, None] - np.arange(kw)[None, :])
            tiles.append(np.where(rel <= DIL_RADIUS, rel * dil, np.inf).astype(np.float32))
        dist = np.stack(tiles)
        tab = -(slopes * LOG2E)[:, None, None, None] * dist[None]
        tab = jnp.asarray(tab) - shift
        tables.append(tab[:, 0] if len(offs) == 1 else tab)
    return tables


def _rpb_table(rpb, shift, rows_total):
    kh = min(NA_ROWS, rows_total)
    c = np.arange(GRID_W)
    c0 = np.clip(c - NA_COLS // 2, 0, GRID_W - NA_COLS)
    col_ok = (c[None, :] >= c0[:, None]) & (c[None, :] < c0[:, None] + NA_COLS)
    dc = np.clip(c[None, :] - c[:, None], -(NA_COLS - 1), NA_COLS - 1) + NA_COLS - 1
    type_rows = list(range(kh // 2)) + [kh // 2] + list(range(rows_total - kh // 2 + 1, rows_total))
    dr = np.array([[int(np.clip(r - kh // 2, 0, rows_total - kh)) + j - r + NA_ROWS - 1 for j in range(kh)]
                   for r in type_rows])
    tab = rpb[:, dr[:, :, None, None], dc[None, None, :, :]]
    tab = jnp.where(col_ok[None, None, None], tab, -jnp.inf)
    tab = tab.transpose(0, 1, 3, 2, 4).reshape(rpb.shape[0], len(type_rows), GRID_W, kh * GRID_W) * LOG2E
    return tab - shift[:, None, None, None]


def kernel(x, mem, norm_mix, w_in, qk_gain, rpb, norm_mem, w_mem_kv, out_gain, w_out, norm_ffn, w_group,
           b_group, w_router, b_router, w_gate, w_up, w_down):
    bsz, seq, d = x.shape
    t = bsz * seq
    depth = w_in.shape[0]
    rows_total = seq // GRID_W

    x2 = x.reshape(t, d)
    moe = None
    for l in range(depth):
        gains = jnp.tile(qk_gain[l], (1, 2))
        w_in_l = w_in[l].astype(BF16)
        if moe is None:
            proj = _proj(x2, None, norm_mix[l][None, :], w_in_l)
        else:
            x2, proj = _proj(x2, moe, norm_mix[l][None, :], w_in_l)
        proj3 = proj.reshape(bsz, seq, IN_WIDTH)

        shift_a = _score_bound(qk_gain[l, 0], qk_gain[l, 1])
        b1, b4, b16 = _alibi_tables(shift_a)
        oa = _attn_a(proj3, gains[0:2], b1, b4, b16)

        shift_b = _score_bound(qk_gain[l, 2], qk_gain[l, 3]) + LOG2E * jnp.max(rpb[l], axis=(1, 2))
        ob = _attn_b(proj3, gains[2:4], _rpb_table(rpb[l], shift_b, rows_total))

        kaug, vaug = _memkv(mem, norm_mem[l][None, :], w_mem_kv[l].astype(BF16), gains[5:6])
        shift_m = jnp.full((1, mem.shape[1]), _score_bound(qk_gain[l, 4], qk_gain[l, 5]), F32)
        om = _attn_m(proj3, kaug, vaug, gains[4:6], shift_m)

        w_route = jnp.zeros((d, LANES), F32).at[:, :N_EXPERTS].set(w_router[l])
        w_route = w_route.at[:, N_EXPERTS:N_EXPERTS + N_GROUPS].set(w_group[l])
        w_route_hi = w_route.astype(BF16)
        w_route_lo = (w_route - w_route_hi.astype(F32)).astype(BF16)
        b_route = jnp.zeros((1, LANES), F32).at[0, :N_EXPERTS].set(b_router[l])
        b_route = b_route.at[0, N_EXPERTS:N_EXPERTS + N_GROUPS].set(b_group[l])
        x2, h3, route = _mix(oa.reshape(t, WIDTH_A), ob.reshape(t, WIDTH_B), om.reshape(t, WIDTH_M), x2,
                             out_gain[l][None, :], w_out[l].astype(BF16), norm_ffn[l][None, :],
                             jnp.stack([w_route_hi, w_route_lo]), b_route)

        wgu = jnp.concatenate([w_gate[l], w_up[l]], axis=-1).astype(BF16)
        moe = _moe(h3, route, wgu, w_down[l].astype(BF16), seq)
    return _add(x2, moe).reshape(bsz, seq, d)
```

```python
import functools

import numpy as np
import jax
import jax.numpy as jnp
from jax import lax
from jax.experimental import pallas as pl
from jax.experimental.pallas import tpu as pltpu

F32 = jnp.float32
BF16 = jnp.bfloat16

D_MODEL = 1024
HEAD_DIM = 64
N_HEADS_A = 6
N_HEADS_B = 6
N_HEADS_M = 4
WIDTH_A = N_HEADS_A * HEAD_DIM
WIDTH_B = N_HEADS_B * HEAD_DIM
WIDTH_M = N_HEADS_M * HEAD_DIM
IN_WIDTH = 3 * WIDTH_A + 3 * WIDTH_B + WIDTH_M
DILATED_PATTERNS = ((128, 1), (512, 4), (2048, 16))
DIL_RADIUS = 64
GRID_W = 64
NA_ROWS = 8
NA_COLS = 16
N_GROUPS = 4
EXPERTS_PER_GROUP = 8
N_EXPERTS = N_GROUPS * EXPERTS_PER_GROUP
D_EXPERT = 512
EPS = 1e-6
ATTN_SCALE = HEAD_DIM ** -0.5
LOG2E = float(np.log2(np.e))
QSCALE = ATTN_SCALE * LOG2E

LANES = 128
SUBLANES = 8
ROW_TILES = D_MODEL // LANES
QBLK = 128
PROJ_ROWS = 512
MOE_TM = 128
VMEM_LIMIT = 56 * 1024 * 1024


def _cparams(sem):
    return pltpu.CompilerParams(dimension_semantics=sem, vmem_limit_bytes=VMEM_LIMIT)


def _proj_kernel(*refs, add):
    if add:
        x_ref, m_ref, g_ref, w_ref, xo_ref, o_ref = refs
        x = x_ref[...] + m_ref[...]
        xo_ref[...] = x
    else:
        x_ref, g_ref, w_ref, o_ref = refs
        x = x_ref[...]
    ms = jnp.mean(x * x, axis=-1, keepdims=True)
    h = (x * lax.rsqrt(ms + EPS) * g_ref[...]).astype(BF16)
    o_ref[...] = jnp.dot(h, w_ref[...], preferred_element_type=F32).astype(BF16)


def _proj(x2, moe2, gain, w):
    t = x2.shape[0]
    add = moe2 is not None
    row_spec = pl.BlockSpec((PROJ_ROWS, D_MODEL), lambda i: (i, 0))
    in_specs = [row_spec] + ([row_spec] if add else []) + [
        pl.BlockSpec((1, D_MODEL), lambda i: (0, 0)),
        pl.BlockSpec((D_MODEL, IN_WIDTH), lambda i: (0, 0)),
    ]
    proj_spec = pl.BlockSpec((PROJ_ROWS, IN_WIDTH), lambda i: (i, 0))
    proj_shape = jax.ShapeDtypeStruct((t, IN_WIDTH), BF16)
    if add:
        out_shape = (jax.ShapeDtypeStruct((t, D_MODEL), F32), proj_shape)
        out_specs = (row_spec, proj_spec)
        args = (x2, moe2, gain, w)
    else:
        out_shape, out_specs, args = proj_shape, proj_spec, (x2, gain, w)
    return pl.pallas_call(
        functools.partial(_proj_kernel, add=add),
        out_shape=out_shape, grid=(t // PROJ_ROWS,), in_specs=in_specs, out_specs=out_specs,
        compiler_params=_cparams(("parallel",)), name="proj_add" if add else "proj",
    )(*args)


def _data_mask(lane, par):
    return (lane < HEAD_DIM) if par == 0 else (lane >= HEAD_DIM)


def _ones_lane(par):
    return HEAD_DIM if par == 0 else 0


def _head_norm(x, dm, gain):
    ss = jnp.sum(jnp.where(dm, x * x, 0.0), axis=-1, keepdims=True)
    return x * lax.rsqrt(ss * (1.0 / HEAD_DIM) + EPS) * gain


def _normalise(tot, lane, par):
    den = jnp.sum(jnp.where(lane == _ones_lane(par), tot, 0.0), axis=-1, keepdims=True)
    return tot / den


def _memkv_kernel(m_ref, g_ref, w_ref, kg_ref, k_ref, v_ref):
    x = m_ref[...]
    ms = jnp.mean(x * x, axis=-1, keepdims=True)
    h = (x * lax.rsqrt(ms + EPS) * g_ref[...]).astype(BF16)
    kv = jnp.dot(h, w_ref[...], preferred_element_type=F32)
    lane = lax.broadcasted_iota(jnp.int32, (x.shape[0], LANES), 1)
    for slab in range(WIDTH_M // LANES):
        kf = kv[:, slab * LANES:(slab + 1) * LANES]
        vf = kv[:, WIDTH_M + slab * LANES:WIDTH_M + (slab + 1) * LANES]
        for par in (0, 1):
            dm = _data_mask(lane, par)
            kn = _head_norm(kf, dm, kg_ref[...])
            k_ref[2 * slab + par] = jnp.where(dm, kn, 0.0).astype(BF16)
            v_ref[2 * slab + par] = jnp.where(dm, vf, jnp.where(lane == _ones_lane(par), 1.0, 0.0)).astype(BF16)


def _memkv(mem, gain, w, kgain):
    b, m, _ = mem.shape
    kv_shape = jax.ShapeDtypeStruct((b, N_HEADS_M, m, LANES), BF16)
    kv_spec = pl.BlockSpec((None, N_HEADS_M, m, LANES), lambda i: (i, 0, 0, 0))
    return pl.pallas_call(
        _memkv_kernel, out_shape=(kv_shape, kv_shape), grid=(b,),
        in_specs=[pl.BlockSpec((None, m, D_MODEL), lambda i: (i, 0, 0)),
                  pl.BlockSpec((1, D_MODEL), lambda i: (0, 0)),
                  pl.BlockSpec((D_MODEL, 2 * WIDTH_M), lambda i: (0, 0)),
                  pl.BlockSpec((1, LANES), lambda i: (0, 0))],
        out_specs=(kv_spec, kv_spec), compiler_params=_cparams(("parallel",)), name="memkv",
    )(mem, gain, w, kgain)


PREP_ROWS = 256
A_UNROLL = 8


def _attn_a_kernel(q_ref, k_ref, v_ref, g_ref, b1_ref, b4_ref, b16_ref, o_ref,
                   qs, ks, vs, qm, km, vm, qd, kd, vd, accd, accm, tot, s_scr):
    s_len = q_ref.shape[0]
    n_prep = s_len // PREP_ROWS
    lane_p = lax.broadcasted_iota(jnp.int32, (PREP_ROWS, LANES), 1)
    gq = g_ref[0:1, :]
    gk = g_ref[1:2, :]

    for par in (0, 1):
        def prep(c, carry, par=par):
            rows = pl.ds(pl.multiple_of(c * PREP_ROWS, PREP_ROWS), PREP_ROWS)
            dm = _data_mask(lane_p, par)
            qn = _head_norm(q_ref[rows, :].astype(F32), dm, gq)
            kn = _head_norm(k_ref[rows, :].astype(F32), dm, gk)
            vf = v_ref[rows, :].astype(F32)
            qs[rows, :] = jnp.where(dm, qn * QSCALE, 0.0)
            ks[rows, :] = jnp.where(dm, kn, 0.0)
            vs[rows, :] = jnp.where(dm, vf, jnp.where(lane_p == _ones_lane(par), 1.0, 0.0))
            return carry
        lax.fori_loop(0, n_prep, prep, 0)

        def run_blocks(length, b_ref, out, par=par):
            nblk = length // QBLK

            def blocks(o, carry):
                units = []
                for i in range(A_UNROLL):
                    u = o * A_UNROLL + i
                    if nblk == 1:
                        base = pl.multiple_of(u * length, QBLK)
                        qrows = pl.ds(base, QBLK)
                        keys = pl.ds(base, QBLK)
                        bias = b_ref[par]
                        kw = QBLK
                    else:
                        r = u // nblk
                        blk = u % nblk
                        q0 = blk * QBLK
                        ws = jnp.clip(q0 - DIL_RADIUS, 0, length - 2 * QBLK)
                        tid = jnp.where(blk == 0, 0, jnp.where(blk == nblk - 1, 2, 1))
                        qrows = pl.ds(pl.multiple_of(r * length + q0, QBLK), QBLK)
                        keys = pl.ds(pl.multiple_of(r * length + ws, DIL_RADIUS), 2 * QBLK)
                        bias = b_ref[par, tid]
                        kw = 2 * QBLK
                    s_scr[i, :, 0:kw] = lax.dot_general(qd[qrows, :], kd[keys, :], (((1,), (1,)), ((), ())),
                                                        preferred_element_type=F32)
                    units.append((qrows, keys, bias, kw))
                for i, (qrows, keys, bias, kw) in enumerate(units):
                    p = jnp.exp2(s_scr[i, :, 0:kw] + bias).astype(BF16)
                    out[qrows, :] = jnp.dot(p, vd[keys, :], preferred_element_type=F32)
                return carry
            lax.fori_loop(0, s_len // QBLK // A_UNROLL, blocks, 0)

        d_mid, d_far = DILATED_PATTERNS[1][1], DILATED_PATTERNS[2][1]
        len_mid, len_far = s_len // d_mid, s_len // d_far
        sub = d_far // d_mid
        arrays = ((qs, qm, qd), (ks, km, kd), (vs, vm, vd))

        def cast(c, carry):
            rows = pl.ds(pl.multiple_of(c * PREP_ROWS, PREP_ROWS), PREP_ROWS)
            for x_tok, _, x_bf in arrays:
                x_bf[rows, :] = x_tok[rows, :].astype(BF16)
            return carry
        lax.fori_loop(0, n_prep, cast, 0)
        run_blocks(s_len, b1_ref, accd)

        def first(c, carry, par=par):
            rows = pl.ds(pl.multiple_of(c * PREP_ROWS, PREP_ROWS), PREP_ROWS)
            tot[par, rows, :] = accd[rows, :]
            return carry
        lax.fori_loop(0, n_prep, first, 0)

        def split_mid(r, carry):
            dst = pl.ds(pl.multiple_of(r * len_mid, QBLK), len_mid)
            src = pl.ds(r, len_mid, stride=d_mid)
            for x_tok, x_mid, x_bf in arrays:
                x = x_tok[src, :]
                x_mid[dst, :] = x
                x_bf[dst, :] = x.astype(BF16)
            return carry
        lax.fori_loop(0, d_mid, split_mid, 0)
        run_blocks(len_mid, b4_ref, accm)

        def far_rows(c):
            mid = pl.ds((c // sub) * len_mid + c % sub, len_far, stride=sub)
            far = pl.ds(pl.multiple_of(c * len_far, QBLK), len_far)
            return mid, far

        def split_far(c, carry):
            mid, far = far_rows(c)
            for _, x_mid, x_bf in arrays:
                x_bf[far, :] = x_mid[mid, :].astype(BF16)
            return carry
        lax.fori_loop(0, d_far, split_far, 0)
        run_blocks(len_far, b16_ref, accd)

        def merge_far(c, carry):
            mid, far = far_rows(c)
            accm[mid, :] = accm[mid, :] + accd[far, :]
            return carry
        lax.fori_loop(0, d_far, merge_far, 0)

        def merge_mid(r, carry, par=par):
            src = pl.ds(pl.multiple_of(r * len_mid, QBLK), len_mid)
            dst = pl.ds(r, len_mid, stride=d_mid)
            tot[par, dst, :] = tot[par, dst, :] + accm[src, :]
            return carry
        lax.fori_loop(0, d_mid, merge_mid, 0)

    def fin(c, carry):
        rows = pl.ds(pl.multiple_of(c * PREP_ROWS, PREP_ROWS), PREP_ROWS)
        o0 = _normalise(tot[0, rows, :], lane_p, 0)
        o1 = _normalise(tot[1, rows, :], lane_p, 1)
        o_ref[rows, :] = jnp.where(lane_p < HEAD_DIM, o0, o1).astype(BF16)
        return carry
    lax.fori_loop(0, n_prep, fin, 0)


def _attn_a(proj3, gains, b1, b4, b16):
    b, s, _ = proj3.shape
    n_pairs = WIDTH_A // LANES

    def slab(off):
        return pl.BlockSpec((None, s, LANES), lambda i, p: (i, 0, off + p))
    return pl.pallas_call(
        _attn_a_kernel, out_shape=jax.ShapeDtypeStruct((b, s, WIDTH_A), BF16), grid=(b, n_pairs),
        in_specs=[slab(0), slab(n_pairs), slab(2 * n_pairs),
                  pl.BlockSpec((2, LANES), lambda i, p: (0, 0)),
                  pl.BlockSpec((2, 3, QBLK, 2 * QBLK), lambda i, p: (p, 0, 0, 0)),
                  pl.BlockSpec((2, 3, QBLK, 2 * QBLK), lambda i, p: (p, 0, 0, 0)),
                  pl.BlockSpec((2, QBLK, QBLK), lambda i, p: (p, 0, 0))],
        out_specs=pl.BlockSpec((None, s, LANES), lambda i, p: (i, 0, p)),
        scratch_shapes=[pltpu.VMEM((s, LANES), F32)] * 6 + [pltpu.VMEM((s, LANES), BF16)] * 3
        + [pltpu.VMEM((s, LANES), F32)] * 2
        + [pltpu.VMEM((2, s, LANES), F32), pltpu.VMEM((A_UNROLL, QBLK, 2 * QBLK), F32)],
        compiler_params=_cparams(("parallel", "parallel")), name="attn_a",
    )(proj3, proj3, proj3, gains, b1, b4, b16)


def _attn_b_kernel(q_ref, k_ref, v_ref, g_ref, bias_ref, o_ref, qb, kb, vb, s_scr):
    s_len = q_ref.shape[0]
    n_prep = s_len // PREP_ROWS
    rows_total = s_len // GRID_W
    kh = min(NA_ROWS, rows_total)
    lane_p = lax.broadcasted_iota(jnp.int32, (PREP_ROWS, LANES), 1)
    lane_q = lax.broadcasted_iota(jnp.int32, (GRID_W, LANES), 1)
    gq = g_ref[0:1, :]
    gk = g_ref[1:2, :]

    for par in (0, 1):
        def prep(c, carry, par=par):
            rows = pl.ds(pl.multiple_of(c * PREP_ROWS, PREP_ROWS), PREP_ROWS)
            dm = _data_mask(lane_p, par)
            qn = _head_norm(q_ref[rows, :].astype(F32), dm, gq)
            kn = _head_norm(k_ref[rows, :].astype(F32), dm, gk)
            vf = v_ref[rows, :].astype(F32)
            qb[par, rows, :] = jnp.where(dm, qn * QSCALE, 0.0).astype(BF16)
            kb[par, rows, :] = jnp.where(dm, kn, 0.0).astype(BF16)
            vb[par, rows, :] = jnp.where(dm, vf, jnp.where(lane_p == _ones_lane(par), 1.0, 0.0)).astype(BF16)
            return carry
        lax.fori_loop(0, n_prep, prep, 0)

    unroll = 4

    def grid_rows(o, carry):
        units = []
        for i in range(unroll):
            r = o * unroll + i
            r0 = jnp.clip(r - kh // 2, 0, rows_total - kh)
            typ = jnp.where(r < kh // 2, r, jnp.where(r > rows_total - kh // 2, r - (rows_total - kh), kh // 2))
            qrows = pl.ds(pl.multiple_of(r * GRID_W, GRID_W), GRID_W)
            keys = pl.ds(pl.multiple_of(r0 * GRID_W, GRID_W), kh * GRID_W)
            for par in (0, 1):
                s_scr[2 * i + par] = lax.dot_general(qb[par, qrows, :], kb[par, keys, :], (((1,), (1,)), ((), ())),
                                                     preferred_element_type=F32)
            units.append((qrows, keys, typ))
        for i, (qrows, keys, typ) in enumerate(units):
            outs = []
            for par in (0, 1):
                p = jnp.exp2(s_scr[2 * i + par] + bias_ref[par, typ]).astype(BF16)
                acc = jnp.dot(p, vb[par, keys, :], preferred_element_type=F32)
                outs.append(_normalise(acc, lane_q, par))
            o_ref[qrows, :] = jnp.where(lane_q < HEAD_DIM, outs[0], outs[1]).astype(BF16)
        return carry
    lax.fori_loop(0, rows_total // unroll, grid_rows, 0)


def _attn_b(proj3, gains, bias):
    b, s, _ = proj3.shape
    n_pairs = WIDTH_B // LANES
    first = 3 * WIDTH_A // LANES
    n_types, kw = bias.shape[1], bias.shape[3]

    def slab(off):
        return pl.BlockSpec((None, s, LANES), lambda i, p: (i, 0, first + off + p))
    return pl.pallas_call(
        _attn_b_kernel, out_shape=jax.ShapeDtypeStruct((b, s, WIDTH_B), BF16), grid=(b, n_pairs),
        in_specs=[slab(0), slab(n_pairs), slab(2 * n_pairs),
                  pl.BlockSpec((2, LANES), lambda i, p: (0, 0)),
                  pl.BlockSpec((2, n_types, GRID_W, kw), lambda i, p: (p, 0, 0, 0))],
        out_specs=pl.BlockSpec((None, s, LANES), lambda i, p: (i, 0, p)),
        scratch_shapes=[pltpu.VMEM((2, s, LANES), BF16)] * 3 + [pltpu.VMEM((8, GRID_W, kw), F32)],
        compiler_params=_cparams(("parallel", "parallel")), name="attn_b",
    )(proj3, proj3, proj3, gains, bias)


MEM_QROWS = 256


def _attn_m_kernel(q_ref, k_ref, v_ref, g_ref, shift_ref, o_ref):
    s_len = q_ref.shape[0]
    lane = lax.broadcasted_iota(jnp.int32, (MEM_QROWS, LANES), 1)
    gq = g_ref[0:1, :]

    def chunk(c, carry):
        rows = pl.ds(pl.multiple_of(c * MEM_QROWS, MEM_QROWS), MEM_QROWS)
        qf = q_ref[rows, :].astype(F32)
        outs = []
        for par in (0, 1):
            dm = _data_mask(lane, par)
            qn = jnp.where(dm, _head_norm(qf, dm, gq) * QSCALE, 0.0).astype(BF16)
            s = lax.dot_general(qn, k_ref[par], (((1,), (1,)), ((), ())), preferred_element_type=F32)
            p = jnp.exp2(s - shift_ref[...]).astype(BF16)
            acc = jnp.dot(p, v_ref[par], preferred_element_type=F32)
            outs.append(_normalise(acc, lane, par))
        o_ref[rows, :] = jnp.where(lane < HEAD_DIM, outs[0], outs[1]).astype(BF16)
        return carry
    lax.fori_loop(0, s_len // MEM_QROWS, chunk, 0)


def _attn_m(proj3, kaug, vaug, gains, shift):
    b, s, _ = proj3.shape
    m = kaug.shape[2]
    n_pairs = WIDTH_M // LANES
    first = 3 * (WIDTH_A + WIDTH_B) // LANES
    kv_spec = pl.BlockSpec((None, 2, m, LANES), lambda i, p: (i, p, 0, 0))
    return pl.pallas_call(
        _attn_m_kernel, out_shape=jax.ShapeDtypeStruct((b, s, WIDTH_M), BF16), grid=(b, n_pairs),
        in_specs=[pl.BlockSpec((None, s, LANES), lambda i, p: (i, 0, first + p)), kv_spec, kv_spec,
                  pl.BlockSpec((2, LANES), lambda i, p: (0, 0)),
                  pl.BlockSpec((1, m), lambda i, p: (0, 0))],
        out_specs=pl.BlockSpec((None, s, LANES), lambda i, p: (i, 0, p)),
        compiler_params=_cparams(("parallel", "parallel")), name="attn_m",
    )(proj3, kaug, vaug, gains, shift)


def _mix_kernel(oa_ref, ob_ref, om_ref, x_ref, og_ref, wo_ref, gf_ref, wr_ref, br_ref,
                x1_ref, h3_ref, route_ref):
    rows = x_ref.shape[0]
    y = None
    off = 0
    for o_ref in (oa_ref, ob_ref, om_ref):
        width = o_ref.shape[1]
        o = o_ref[...].astype(F32)
        ms = jnp.mean(o * o, axis=-1, keepdims=True)
        mixed = (o * lax.rsqrt(ms + EPS) * og_ref[:, off:off + width]).astype(BF16)
        part = jnp.dot(mixed, wo_ref[off:off + width, :], preferred_element_type=F32)
        y = part if y is None else y + part
        off += width
    x1 = x_ref[...] + y
    x1_ref[...] = x1
    ms = jnp.mean(x1 * x1, axis=-1, keepdims=True)
    h = x1 * lax.rsqrt(ms + EPS) * gf_ref[...]
    for j in range(ROW_TILES):
        h3_ref[pl.ds(j, rows, stride=ROW_TILES), :] = h[:, j * LANES:(j + 1) * LANES]

    h_hi = h.astype(BF16)
    h_lo = (h - h_hi.astype(F32)).astype(BF16)
    logits = (jnp.dot(h_hi, wr_ref[0], preferred_element_type=F32)
              + jnp.dot(h_lo, wr_ref[0], preferred_element_type=F32)
              + jnp.dot(h_hi, wr_ref[1], preferred_element_type=F32)) + br_ref[...]
    lane = lax.broadcasted_iota(jnp.int32, (rows, LANES), 1)
    lane_f = lane.astype(F32)
    neg = -jnp.inf
    far = float(LANES)
    is_g = (lane >> 2) == (N_EXPERTS // N_GROUPS)
    gl = jnp.where(is_g, logits, neg)
    gmax = jnp.max(gl, axis=-1, keepdims=True)
    gidx = jnp.min(jnp.where(gl == gmax, lane_f, far), axis=-1, keepdims=True) - float(N_EXPERTS)
    gsum = jnp.sum(jnp.where(is_g, jnp.exp(gl - gmax), 0.0), axis=-1, keepdims=True)
    g_top = 1.0 / gsum
    in_grp = (lane >> 3).astype(F32) == gidx
    el = jnp.where(in_grp, logits, neg)
    v1 = jnp.max(el, axis=-1, keepdims=True)
    i1 = jnp.min(jnp.where(el == v1, lane_f, far), axis=-1, keepdims=True)
    el2 = jnp.where(lane_f == i1, neg, el)
    v2 = jnp.max(el2, axis=-1, keepdims=True)
    i2 = jnp.min(jnp.where(el2 == v2, lane_f, far), axis=-1, keepdims=True)
    ev = jnp.exp(v2 - v1)
    w1 = g_top / (1.0 + ev)
    w2 = g_top * ev / (1.0 + ev)
    route_ref[...] = jnp.where(lane == 0, i1, jnp.where(lane == 1, i2, jnp.where(lane == 2, w1,
                               jnp.where(lane == 3, w2, 0.0))))


def _mix(oa, ob, om, x2, out_gain, w_out, gain_ffn, w_route, b_route):
    t = x2.shape[0]

    def rows(width):
        return pl.BlockSpec((PROJ_ROWS, width), lambda i: (i, 0))

    def whole(*shape):
        return pl.BlockSpec(shape, lambda i: (0,) * len(shape))
    return pl.pallas_call(
        _mix_kernel,
        out_shape=(jax.ShapeDtypeStruct((t, D_MODEL), F32),
                   jax.ShapeDtypeStruct((t * ROW_TILES, LANES), F32),
                   jax.ShapeDtypeStruct((t, LANES), F32)),
        grid=(t // PROJ_ROWS,),
        in_specs=[rows(WIDTH_A), rows(WIDTH_B), rows(WIDTH_M), rows(D_MODEL),
                  whole(1, D_MODEL), whole(D_MODEL, D_MODEL), whole(1, D_MODEL),
                  whole(2, D_MODEL, LANES), whole(1, LANES)],
        out_specs=(rows(D_MODEL), pl.BlockSpec((PROJ_ROWS * ROW_TILES, LANES), lambda i: (i, 0)), rows(LANES)),
        compiler_params=_cparams(("parallel",)), name="mix",
    )(oa, ob, om, x2, out_gain, w_out, gain_ffn, w_route, b_route)


SCATTER_UNROLL = 8
MOE_PART = 32


W_SLOTS = 3
W_CHUNK_ROWS = 256


def _moe_weight_copies(wgu_hbm, wd_hbm, wgu_buf, wd_buf, sems, e, slot):
    copies = []
    for hbm, buf in ((wgu_hbm, wgu_buf), (wd_hbm, wd_buf)):
        for c in range(buf.shape[1] // W_CHUNK_ROWS):
            rows = pl.ds(c * W_CHUNK_ROWS, W_CHUNK_ROWS)
            copies.append(pltpu.make_async_copy(hbm.at[e, rows], buf.at[slot, rows], sems.at[slot]))
    return copies


def _moe_kernel(cstart_ref, cnt_ref, tok_ref, gate_ref, h3_ref, wgu_hbm, wd_hbm, o_ref,
                wgu_buf, wd_buf, sems, xt, yt, acc):
    b = pl.program_id(0)
    n_tok = o_ref.shape[0]
    copies = functools.partial(_moe_weight_copies, wgu_hbm, wd_hbm, wgu_buf, wd_buf, sems)
    ahead = W_SLOTS - 1

    for e0 in range(ahead):
        for c in copies(e0, e0):
            c.start()
    acc[...] = jnp.zeros_like(acc)

    def expert(e, carry):
        slot = e % W_SLOTS
        for c in copies(e, slot):
            c.wait()

        @pl.when(e + ahead < N_EXPERTS)
        def _():
            for c in copies(e + ahead, (e + ahead) % W_SLOTS):
                c.start()

        first = cstart_ref[b * N_EXPERTS + e]
        count = cnt_ref[b * N_EXPERTS + e]

        def tile(j, carry):
            rs = first + j * MOE_TM
            valid = count - j * MOE_TM

            def gather(part, carry):
                for i in range(MOE_PART):
                    row = part * MOE_PART + i
                    tok = tok_ref[0, rs + row]
                    xt[pl.ds(pl.multiple_of(row * ROW_TILES, ROW_TILES), ROW_TILES), :] = (
                        h3_ref[pl.ds(pl.multiple_of(tok * ROW_TILES, ROW_TILES), ROW_TILES), :])
                return carry
            lax.fori_loop(0, MOE_TM // MOE_PART, gather, 0)

            lhs = jnp.concatenate([xt[pl.ds(k, MOE_TM, stride=ROW_TILES), :] for k in range(ROW_TILES)], axis=1)
            gu = jnp.dot(lhs.astype(BF16), wgu_buf[slot], preferred_element_type=F32)
            g = gu[:, :D_EXPERT]
            hid = (g * (1.0 / (1.0 + jnp.exp(-g)))) * gu[:, D_EXPERT:]
            y = jnp.dot(hid.astype(BF16), wd_buf[slot], preferred_element_type=F32)
            for k in range(ROW_TILES):
                yt[pl.ds(k, MOE_TM, stride=ROW_TILES), :] = y[:, k * LANES:(k + 1) * LANES]

            def scatter(part, carry):
                for i0 in range(0, MOE_PART, SCATTER_UNROLL):
                    vals = []
                    for i in range(i0, i0 + SCATTER_UNROLL):
                        row = part * MOE_PART + i
                        live = row < valid
                        dst = jnp.where(live, tok_ref[0, rs + row], n_tok)
                        gate = jnp.where(live, gate_ref[0, rs + row], 0.0)
                        rows = pl.ds(pl.multiple_of(dst * ROW_TILES, ROW_TILES), ROW_TILES)
                        src = pl.ds(pl.multiple_of(row * ROW_TILES, ROW_TILES), ROW_TILES)
                        vals.append((rows, acc[rows, :] + gate * yt[src, :]))
                    for rows, val in vals:
                        acc[rows, :] = val
                return carry
            lax.fori_loop(0, MOE_TM // MOE_PART, scatter, 0)
            return carry
        lax.fori_loop(0, (count + MOE_TM - 1) // MOE_TM, tile, 0)
        return carry
    lax.fori_loop(0, N_EXPERTS, expert, 0)

    for k in range(ROW_TILES):
        o_ref[:, k * LANES:(k + 1) * LANES] = acc[pl.ds(k, n_tok, stride=ROW_TILES), :]


def _moe(h3, route, wgu, wd, block_tokens):
    t = route.shape[0]
    nblk = t // block_tokens
    n_assign = 2 * block_tokens

    e = route[:, 0:2].astype(jnp.int32).reshape(nblk, n_assign)
    w = route[:, 2:4].reshape(nblk, n_assign)
    key = e * n_assign + jnp.arange(n_assign, dtype=jnp.int32)[None, :]
    key_s, w_s = lax.sort((key, w), dimension=1, num_keys=1)
    tok_s = (key_s & (n_assign - 1)) >> 1
    pad_i = jnp.zeros((nblk, MOE_TM), jnp.int32)
    tok_s = jnp.concatenate([tok_s, pad_i], axis=1)
    w_s = jnp.concatenate([w_s, pad_i.astype(F32)], axis=1)
    cnt = jnp.sum((e[:, :, None] == jnp.arange(N_EXPERTS, dtype=jnp.int32)[None, None, :]).astype(jnp.int32), axis=1)
    cstart = jnp.cumsum(cnt, axis=1) - cnt

    def smem_list():
        return pl.BlockSpec((None, 1, n_assign + MOE_TM), lambda b, *_: (b, 0, 0), memory_space=pltpu.SMEM)
    grid_spec = pltpu.PrefetchScalarGridSpec(
        num_scalar_prefetch=2, grid=(nblk,),
        in_specs=[smem_list(), smem_list(),
                  pl.BlockSpec((block_tokens * ROW_TILES, LANES), lambda b, *_: (b, 0)),
                  pl.BlockSpec(memory_space=pl.ANY), pl.BlockSpec(memory_space=pl.ANY)],
        out_specs=pl.BlockSpec((block_tokens, D_MODEL), lambda b, *_: (b, 0)),
        scratch_shapes=[pltpu.VMEM((W_SLOTS, D_MODEL, 2 * D_EXPERT), BF16),
                        pltpu.VMEM((W_SLOTS, D_EXPERT, D_MODEL), BF16),
                        pltpu.SemaphoreType.DMA((W_SLOTS,)),
                        pltpu.VMEM((MOE_TM * ROW_TILES, LANES), F32),
                        pltpu.VMEM((MOE_TM * ROW_TILES, LANES), F32),
                        pltpu.VMEM(((block_tokens + 1) * ROW_TILES, LANES), F32)])
    return pl.pallas_call(
        _moe_kernel, out_shape=jax.ShapeDtypeStruct((t, D_MODEL), F32), grid_spec=grid_spec,
        compiler_params=_cparams(("arbitrary",)), name="moe",
    )(cstart.reshape(-1).astype(jnp.int32), cnt.reshape(-1).astype(jnp.int32),
      tok_s.reshape(nblk, 1, n_assign + MOE_TM), w_s.reshape(nblk, 1, n_assign + MOE_TM), h3, wgu, wd)


def _add_kernel(a_ref, b_ref, o_ref):
    o_ref[...] = a_ref[...] + b_ref[...]


def _add(a, b):
    t = a.shape[0]
    spec = pl.BlockSpec((PROJ_ROWS, D_MODEL), lambda i: (i, 0))
    return pl.pallas_call(_add_kernel, out_shape=jax.ShapeDtypeStruct(a.shape, a.dtype), grid=(t // PROJ_ROWS,),
                          in_specs=[spec, spec], out_specs=spec, compiler_params=_cparams(("parallel",)),
                          name="residual_add")(a, b)


def _score_bound(gq, gk):
    return (HEAD_DIM * ATTN_SCALE * LOG2E) * jnp.max(jnp.abs(gq)) * jnp.max(jnp.abs(gk))


def _alibi_tables(shift):
    slopes = (2.0 ** (-8.0 * np.arange(1, N_HEADS_A + 1) / N_HEADS_A)).astype(np.float32)
    tables = []
    for _, dil in DILATED_PATTERNS:
        if dil == DILATED_PATTERNS[-1][1]:
            offs, kw = (0,), QBLK
        else:
            offs, kw = (0, DIL_RADIUS, 2 * DIL_RADIUS), 2 * QBLK
        tiles = []
        for off in offs:
            rel = np.abs((off + np.arange(QBLK))[:, None] - np.arange(kw)[None, :])
            tiles.append(np.where(rel <= DIL_RADIUS, rel * dil, np.inf).astype(np.float32))
        dist = np.stack(tiles)
        tab = -(slopes * LOG2E)[:, None, None, None] * dist[None]
        tab = jnp.asarray(tab) - shift
        tables.append(tab[:, 0] if len(offs) == 1 else tab)
    return tables


def _rpb_table(rpb, shift, rows_total):
    kh = min(NA_ROWS, rows_total)
    c = np.arange(GRID_W)
    c0 = np.clip(c - NA_COLS // 2, 0, GRID_W - NA_COLS)
    col_ok = (c[None, :] >= c0[:, None]) & (c[None, :] < c0[:, None] + NA_COLS)
    dc = np.clip(c[None, :] - c[:, None], -(NA_COLS - 1), NA_COLS - 1) + NA_COLS - 1
    type_rows = list(range(kh // 2)) + [kh // 2] + list(range(rows_total - kh // 2 + 1, rows_total))
    dr = np.array([[int(np.clip(r - kh // 2, 0, rows_total - kh)) + j - r + NA_ROWS - 1 for j in range(kh)]
                   for r in type_rows])
    n_dc = 2 * NA_COLS - 1
    pick = dc[None, None, :, :, None] == np.arange(n_dc)[None, None, None, None, :]
    cols = jnp.sum(jnp.where(pick, rpb[:, :, None, None, :], 0.0), axis=-1)
    tab = jnp.stack([cols[:, int(d)] for d in dr.reshape(-1)], axis=1).reshape(
        rpb.shape[0], len(type_rows), kh, GRID_W, GRID_W)
    tab = jnp.where(col_ok[None, None, None], tab, -jnp.inf)
    tab = tab.transpose(0, 1, 3, 2, 4).reshape(rpb.shape[0], len(type_rows), GRID_W, kh * GRID_W) * LOG2E
    return tab - shift[:, None, None, None]


def kernel(x, mem, norm_mix, w_in, qk_gain, rpb, norm_mem, w_mem_kv, out_gain, w_out, norm_ffn, w_group,
           b_group, w_router, b_router, w_gate, w_up, w_down):
    bsz, seq, d = x.shape
    t = bsz * seq
    depth = w_in.shape[0]
    rows_total = seq // GRID_W

    x2 = x.reshape(t, d)
    moe = None
    for l in range(depth):
        gains = jnp.tile(qk_gain[l], (1, 2))
        w_in_l = w_in[l].astype(BF16)
        if moe is None:
            proj = _proj(x2, None, norm_mix[l][None, :], w_in_l)
        else:
            x2, proj = _proj(x2, moe, norm_mix[l][None, :], w_in_l)
        proj3 = proj.reshape(bsz, seq, IN_WIDTH)

        shift_a = _score_bound(qk_gain[l, 0], qk_gain[l, 1])
        b1, b4, b16 = _alibi_tables(shift_a)
        oa = _attn_a(proj3, gains[0:2], b1, b4, b16)

        shift_b = _score_bound(qk_gain[l, 2], qk_gain[l, 3]) + LOG2E * jnp.max(rpb[l], axis=(1, 2))
        ob = _attn_b(proj3, gains[2:4], _rpb_table(rpb[l], shift_b, rows_total))

        kaug, vaug = _memkv(mem, norm_mem[l][None, :], w_mem_kv[l].astype(BF16), gains[5:6])
        shift_m = jnp.full((1, mem.shape[1]), _score_bound(qk_gain[l, 4], qk_gain[l, 5]), F32)
        om = _attn_m(proj3, kaug, vaug, gains[4:6], shift_m)

        w_route = jnp.zeros((d, LANES), F32).at[:, :N_EXPERTS].set(w_router[l])
        w_route = w_route.at[:, N_EXPERTS:N_EXPERTS + N_GROUPS].set(w_group[l])
        w_route_hi = w_route.astype(BF16)
        w_route_lo = (w_route - w_route_hi.astype(F32)).astype(BF16)
        b_route = jnp.zeros((1, LANES), F32).at[0, :N_EXPERTS].set(b_router[l])
        b_route = b_route.at[0, N_EXPERTS:N_EXPERTS + N_GROUPS].set(b_group[l])
        x2, h3, route = _mix(oa.reshape(t, WIDTH_A), ob.reshape(t, WIDTH_B), om.reshape(t, WIDTH_M), x2,
                             out_gain[l][None, :], w_out[l].astype(BF16), norm_ffn[l][None, :],
                             jnp.stack([w_route_hi, w_route_lo]), b_route)

        wgu = jnp.concatenate([w_gate[l], w_up[l]], axis=-1).astype(BF16)
        moe = _moe(h3, route, wgu, w_down[l].astype(BF16), seq)
    return _add(x2, moe).reshape(bsz, seq, d)
```

```python
import functools

import numpy as np
import jax
import jax.numpy as jnp
from jax import lax
from jax.experimental import pallas as pl
from jax.experimental.pallas import tpu as pltpu

F32 = jnp.float32
BF16 = jnp.bfloat16

D_MODEL = 1024
HEAD_DIM = 64
N_HEADS_A = 6
N_HEADS_B = 6
N_HEADS_M = 4
WIDTH_A = N_HEADS_A * HEAD_DIM
WIDTH_B = N_HEADS_B * HEAD_DIM
WIDTH_M = N_HEADS_M * HEAD_DIM
IN_WIDTH = 3 * WIDTH_A + 3 * WIDTH_B + WIDTH_M
DILATED_PATTERNS = ((128, 1), (512, 4), (2048, 16))
DIL_RADIUS = 64
GRID_W = 64
NA_ROWS = 8
NA_COLS = 16
N_GROUPS = 4
EXPERTS_PER_GROUP = 8
N_EXPERTS = N_GROUPS * EXPERTS_PER_GROUP
D_EXPERT = 512
EPS = 1e-6
ATTN_SCALE = HEAD_DIM ** -0.5
LOG2E = float(np.log2(np.e))
QSCALE = ATTN_SCALE * LOG2E

LANES = 128
SUBLANES = 8
ROW_TILES = D_MODEL // LANES
QBLK = 128
PROJ_ROWS = 512
MOE_TM = 128
VMEM_LIMIT = 56 * 1024 * 1024


def _cparams(sem):
    return pltpu.CompilerParams(dimension_semantics=sem, vmem_limit_bytes=VMEM_LIMIT)


def _proj_kernel(*refs, add):
    if add:
        x_ref, m_ref, g_ref, w_ref, xo_ref, o_ref = refs
        x = x_ref[...] + m_ref[...]
        xo_ref[...] = x
    else:
        x_ref, g_ref, w_ref, o_ref = refs
        x = x_ref[...]
    ms = jnp.mean(x * x, axis=-1, keepdims=True)
    h = (x * lax.rsqrt(ms + EPS) * g_ref[...]).astype(BF16)
    o_ref[...] = jnp.dot(h, w_ref[...], preferred_element_type=F32).astype(BF16)


def _proj(x2, moe2, gain, w):
    t = x2.shape[0]
    add = moe2 is not None
    row_spec = pl.BlockSpec((PROJ_ROWS, D_MODEL), lambda i: (i, 0))
    in_specs = [row_spec] + ([row_spec] if add else []) + [
        pl.BlockSpec((1, D_MODEL), lambda i: (0, 0)),
        pl.BlockSpec((D_MODEL, IN_WIDTH), lambda i: (0, 0)),
    ]
    proj_spec = pl.BlockSpec((PROJ_ROWS, IN_WIDTH), lambda i: (i, 0))
    proj_shape = jax.ShapeDtypeStruct((t, IN_WIDTH), BF16)
    if add:
        out_shape = (jax.ShapeDtypeStruct((t, D_MODEL), F32), proj_shape)
        out_specs = (row_spec, proj_spec)
        args = (x2, moe2, gain, w)
    else:
        out_shape, out_specs, args = proj_shape, proj_spec, (x2, gain, w)
    return pl.pallas_call(
        functools.partial(_proj_kernel, add=add),
        out_shape=out_shape, grid=(t // PROJ_ROWS,), in_specs=in_specs, out_specs=out_specs,
        compiler_params=_cparams(("parallel",)), name="proj_add" if add else "proj",
    )(*args)


def _data_mask(lane, par):
    return (lane < HEAD_DIM) if par == 0 else (lane >= HEAD_DIM)


def _ones_lane(par):
    return HEAD_DIM if par == 0 else 0


def _head_norm(x, dm, gain):
    ss = jnp.sum(jnp.where(dm, x * x, 0.0), axis=-1, keepdims=True)
    return x * lax.rsqrt(ss * (1.0 / HEAD_DIM) + EPS) * gain


def _normalise(tot, lane, par):
    den = jnp.sum(jnp.where(lane == _ones_lane(par), tot, 0.0), axis=-1, keepdims=True)
    return tot / den


def _memkv_kernel(m_ref, g_ref, w_ref, kg_ref, k_ref, v_ref):
    x = m_ref[...]
    ms = jnp.mean(x * x, axis=-1, keepdims=True)
    h = (x * lax.rsqrt(ms + EPS) * g_ref[...]).astype(BF16)
    kv = jnp.dot(h, w_ref[...], preferred_element_type=F32)
    lane = lax.broadcasted_iota(jnp.int32, (x.shape[0], LANES), 1)
    for slab in range(WIDTH_M // LANES):
        kf = kv[:, slab * LANES:(slab + 1) * LANES]
        vf = kv[:, WIDTH_M + slab * LANES:WIDTH_M + (slab + 1) * LANES]
        for par in (0, 1):
            dm = _data_mask(lane, par)
            kn = _head_norm(kf, dm, kg_ref[...])
            k_ref[2 * slab + par] = jnp.where(dm, kn, 0.0).astype(BF16)
            v_ref[2 * slab + par] = jnp.where(dm, vf, jnp.where(lane == _ones_lane(par), 1.0, 0.0)).astype(BF16)


def _memkv(mem, gain, w, kgain):
    b, m, _ = mem.shape
    kv_shape = jax.ShapeDtypeStruct((b, N_HEADS_M, m, LANES), BF16)
    kv_spec = pl.BlockSpec((None, N_HEADS_M, m, LANES), lambda i: (i, 0, 0, 0))
    return pl.pallas_call(
        _memkv_kernel, out_shape=(kv_shape, kv_shape), grid=(b,),
        in_specs=[pl.BlockSpec((None, m, D_MODEL), lambda i: (i, 0, 0)),
                  pl.BlockSpec((1, D_MODEL), lambda i: (0, 0)),
                  pl.BlockSpec((D_MODEL, 2 * WIDTH_M), lambda i: (0, 0)),
                  pl.BlockSpec((1, LANES), lambda i: (0, 0))],
        out_specs=(kv_spec, kv_spec), compiler_params=_cparams(("parallel",)), name="memkv",
    )(mem, gain, w, kgain)


PREP_ROWS = 256
A_UNROLL = 8


def _attn_a_kernel(q_ref, k_ref, v_ref, g_ref, b1_ref, b4_ref, b16_ref, o_ref,
                   qs, ks, vs, qm, km, vm, qd, kd, vd, accd, accm, tot, s_scr):
    s_len = q_ref.shape[0]
    n_prep = s_len // PREP_ROWS
    lane_p = lax.broadcasted_iota(jnp.int32, (PREP_ROWS, LANES), 1)
    gq = g_ref[0:1, :]
    gk = g_ref[1:2, :]

    for par in (0, 1):
        def prep(c, carry, par=par):
            rows = pl.ds(pl.multiple_of(c * PREP_ROWS, PREP_ROWS), PREP_ROWS)
            dm = _data_mask(lane_p, par)
            qn = _head_norm(q_ref[rows, :].astype(F32), dm, gq)
            kn = _head_norm(k_ref[rows, :].astype(F32), dm, gk)
            vf = v_ref[rows, :].astype(F32)
            qs[rows, :] = jnp.where(dm, qn * QSCALE, 0.0)
            ks[rows, :] = jnp.where(dm, kn, 0.0)
            vs[rows, :] = jnp.where(dm, vf, jnp.where(lane_p == _ones_lane(par), 1.0, 0.0))
            return carry
        lax.fori_loop(0, n_prep, prep, 0)

        def run_blocks(length, b_ref, out, par=par):
            nblk = length // QBLK

            def blocks(o, carry):
                units = []
                for i in range(A_UNROLL):
                    u = o * A_UNROLL + i
                    if nblk == 1:
                        base = pl.multiple_of(u * length, QBLK)
                        qrows = pl.ds(base, QBLK)
                        keys = pl.ds(base, QBLK)
                        bias = b_ref[par]
                        kw = QBLK
                    else:
                        r = u // nblk
                        blk = u % nblk
                        q0 = blk * QBLK
                        ws = jnp.clip(q0 - DIL_RADIUS, 0, length - 2 * QBLK)
                        tid = jnp.where(blk == 0, 0, jnp.where(blk == nblk - 1, 2, 1))
                        qrows = pl.ds(pl.multiple_of(r * length + q0, QBLK), QBLK)
                        keys = pl.ds(pl.multiple_of(r * length + ws, DIL_RADIUS), 2 * QBLK)
                        bias = b_ref[par, tid]
                        kw = 2 * QBLK
                    s_scr[i, :, 0:kw] = lax.dot_general(qd[qrows, :], kd[keys, :], (((1,), (1,)), ((), ())),
                                                        preferred_element_type=F32)
                    units.append((qrows, keys, bias, kw))
                for i, (qrows, keys, bias, kw) in enumerate(units):
                    p = jnp.exp2(s_scr[i, :, 0:kw] + bias).astype(BF16)
                    out[qrows, :] = jnp.dot(p, vd[keys, :], preferred_element_type=F32)
                return carry
            lax.fori_loop(0, s_len // QBLK // A_UNROLL, blocks, 0)

        d_mid, d_far = DILATED_PATTERNS[1][1], DILATED_PATTERNS[2][1]
        len_mid, len_far = s_len // d_mid, s_len // d_far
        sub = d_far // d_mid
        arrays = ((qs, qm, qd), (ks, km, kd), (vs, vm, vd))

        def cast(c, carry):
            rows = pl.ds(pl.multiple_of(c * PREP_ROWS, PREP_ROWS), PREP_ROWS)
            for x_tok, _, x_bf in arrays:
                x_bf[rows, :] = x_tok[rows, :].astype(BF16)
            return carry
        lax.fori_loop(0, n_prep, cast, 0)
        run_blocks(s_len, b1_ref, accd)

        def first(c, carry, par=par):
            rows = pl.ds(pl.multiple_of(c * PREP_ROWS, PREP_ROWS), PREP_ROWS)
            tot[par, rows, :] = accd[rows, :]
            return carry
        lax.fori_loop(0, n_prep, first, 0)

        def split_mid(r, carry):
            dst = pl.ds(pl.multiple_of(r * len_mid, QBLK), len_mid)
            src = pl.ds(r, len_mid, stride=d_mid)
            for x_tok, x_mid, x_bf in arrays:
                x = x_tok[src, :]
                x_mid[dst, :] = x
                x_bf[dst, :] = x.astype(BF16)
            return carry
        lax.fori_loop(0, d_mid, split_mid, 0)
        run_blocks(len_mid, b4_ref, accm)

        def far_rows(c):
            mid = pl.ds((c // sub) * len_mid + c % sub, len_far, stride=sub)
            far = pl.ds(pl.multiple_of(c * len_far, QBLK), len_far)
            return mid, far

        def split_far(c, carry):
            mid, far = far_rows(c)
            for _, x_mid, x_bf in arrays:
                x_bf[far, :] = x_mid[mid, :].astype(BF16)
            return carry
        lax.fori_loop(0, d_far, split_far, 0)
        run_blocks(len_far, b16_ref, accd)

        def merge_far(c, carry):
            mid, far = far_rows(c)
            accm[mid, :] = accm[mid, :] + accd[far, :]
            return carry
        lax.fori_loop(0, d_far, merge_far, 0)

        def merge_mid(r, carry, par=par):
            src = pl.ds(pl.multiple_of(r * len_mid, QBLK), len_mid)
            dst = pl.ds(r, len_mid, stride=d_mid)
            tot[par, dst, :] = tot[par, dst, :] + accm[src, :]
            return carry
        lax.fori_loop(0, d_mid, merge_mid, 0)

    def fin(c, carry):
        rows = pl.ds(pl.multiple_of(c * PREP_ROWS, PREP_ROWS), PREP_ROWS)
        o0 = _normalise(tot[0, rows, :], lane_p, 0)
        o1 = _normalise(tot[1, rows, :], lane_p, 1)
        o_ref[rows, :] = jnp.where(lane_p < HEAD_DIM, o0, o1).astype(BF16)
        return carry
    lax.fori_loop(0, n_prep, fin, 0)


def _attn_a(proj3, gains, b1, b4, b16):
    b, s, _ = proj3.shape
    n_pairs = WIDTH_A // LANES

    def slab(off):
        return pl.BlockSpec((None, s, LANES), lambda i, p: (i, 0, off + p))
    return pl.pallas_call(
        _attn_a_kernel, out_shape=jax.ShapeDtypeStruct((b, s, WIDTH_A), BF16), grid=(b, n_pairs),
        in_specs=[slab(0), slab(n_pairs), slab(2 * n_pairs),
                  pl.BlockSpec((2, LANES), lambda i, p: (0, 0)),
                  pl.BlockSpec((2, 3, QBLK, 2 * QBLK), lambda i, p: (p, 0, 0, 0)),
                  pl.BlockSpec((2, 3, QBLK, 2 * QBLK), lambda i, p: (p, 0, 0, 0)),
                  pl.BlockSpec((2, QBLK, QBLK), lambda i, p: (p, 0, 0))],
        out_specs=pl.BlockSpec((None, s, LANES), lambda i, p: (i, 0, p)),
        scratch_shapes=[pltpu.VMEM((s, LANES), F32)] * 6 + [pltpu.VMEM((s, LANES), BF16)] * 3
        + [pltpu.VMEM((s, LANES), F32)] * 2
        + [pltpu.VMEM((2, s, LANES), F32), pltpu.VMEM((A_UNROLL, QBLK, 2 * QBLK), F32)],
        compiler_params=_cparams(("parallel", "parallel")), name="attn_a",
    )(proj3, proj3, proj3, gains, b1, b4, b16)


def _attn_b_kernel(q_ref, k_ref, v_ref, g_ref, bias_ref, o_ref, qb, kb, vb, s_scr):
    s_len = q_ref.shape[0]
    n_prep = s_len // PREP_ROWS
    rows_total = s_len // GRID_W
    kh = min(NA_ROWS, rows_total)
    lane_p = lax.broadcasted_iota(jnp.int32, (PREP_ROWS, LANES), 1)
    lane_q = lax.broadcasted_iota(jnp.int32, (GRID_W, LANES), 1)
    gq = g_ref[0:1, :]
    gk = g_ref[1:2, :]

    for par in (0, 1):
        def prep(c, carry, par=par):
            rows = pl.ds(pl.multiple_of(c * PREP_ROWS, PREP_ROWS), PREP_ROWS)
            dm = _data_mask(lane_p, par)
            qn = _head_norm(q_ref[rows, :].astype(F32), dm, gq)
            kn = _head_norm(k_ref[rows, :].astype(F32), dm, gk)
            vf = v_ref[rows, :].astype(F32)
            qb[par, rows, :] = jnp.where(dm, qn * QSCALE, 0.0).astype(BF16)
            kb[par, rows, :] = jnp.where(dm, kn, 0.0).astype(BF16)
            vb[par, rows, :] = jnp.where(dm, vf, jnp.where(lane_p == _ones_lane(par), 1.0, 0.0)).astype(BF16)
            return carry
        lax.fori_loop(0, n_prep, prep, 0)

    unroll = 4

    def grid_rows(o, carry):
        units = []
        for i in range(unroll):
            r = o * unroll + i
            r0 = jnp.clip(r - kh // 2, 0, rows_total - kh)
            typ = jnp.where(r < kh // 2, r, jnp.where(r > rows_total - kh // 2, r - (rows_total - kh), kh // 2))
            qrows = pl.ds(pl.multiple_of(r * GRID_W, GRID_W), GRID_W)
            keys = pl.ds(pl.multiple_of(r0 * GRID_W, GRID_W), kh * GRID_W)
            for par in (0, 1):
                s_scr[2 * i + par] = lax.dot_general(qb[par, qrows, :], kb[par, keys, :], (((1,), (1,)), ((), ())),
                                                     preferred_element_type=F32)
            units.append((qrows, keys, typ))
        for i, (qrows, keys, typ) in enumerate(units):
            outs = []
            for par in (0, 1):
                p = jnp.exp2(s_scr[2 * i + par] + bias_ref[par, typ]).astype(BF16)
                acc = jnp.dot(p, vb[par, keys, :], preferred_element_type=F32)
                outs.append(_normalise(acc, lane_q, par))
            o_ref[qrows, :] = jnp.where(lane_q < HEAD_DIM, outs[0], outs[1]).astype(BF16)
        return carry
    lax.fori_loop(0, rows_total // unroll, grid_rows, 0)


def _attn_b(proj3, gains, bias):
    b, s, _ = proj3.shape
    n_pairs = WIDTH_B // LANES
    first = 3 * WIDTH_A // LANES
    n_types, kw = bias.shape[1], bias.shape[3]

    def slab(off):
        return pl.BlockSpec((None, s, LANES), lambda i, p: (i, 0, first + off + p))
    return pl.pallas_call(
        _attn_b_kernel, out_shape=jax.ShapeDtypeStruct((b, s, WIDTH_B), BF16), grid=(b, n_pairs),
        in_specs=[slab(0), slab(n_pairs), slab(2 * n_pairs),
                  pl.BlockSpec((2, LANES), lambda i, p: (0, 0)),
                  pl.BlockSpec((2, n_types, GRID_W, kw), lambda i, p: (p, 0, 0, 0))],
        out_specs=pl.BlockSpec((None, s, LANES), lambda i, p: (i, 0, p)),
        scratch_shapes=[pltpu.VMEM((2, s, LANES), BF16)] * 3 + [pltpu.VMEM((8, GRID_W, kw), F32)],
        compiler_params=_cparams(("parallel", "parallel")), name="attn_b",
    )(proj3, proj3, proj3, gains, bias)


MEM_QROWS = 256


def _attn_m_kernel(q_ref, k_ref, v_ref, g_ref, shift_ref, o_ref):
    s_len = q_ref.shape[0]
    lane = lax.broadcasted_iota(jnp.int32, (MEM_QROWS, LANES), 1)
    gq = g_ref[0:1, :]

    def chunk(c, carry):
        rows = pl.ds(pl.multiple_of(c * MEM_QROWS, MEM_QROWS), MEM_QROWS)
        qf = q_ref[rows, :].astype(F32)
        outs = []
        for par in (0, 1):
            dm = _data_mask(lane, par)
            qn = jnp.where(dm, _head_norm(qf, dm, gq) * QSCALE, 0.0).astype(BF16)
            s = lax.dot_general(qn, k_ref[par], (((1,), (1,)), ((), ())), preferred_element_type=F32)
            p = jnp.exp2(s - shift_ref[...]).astype(BF16)
            acc = jnp.dot(p, v_ref[par], preferred_element_type=F32)
            outs.append(_normalise(acc, lane, par))
        o_ref[rows, :] = jnp.where(lane < HEAD_DIM, outs[0], outs[1]).astype(BF16)
        return carry
    lax.fori_loop(0, s_len // MEM_QROWS, chunk, 0)


def _attn_m(proj3, kaug, vaug, gains, shift):
    b, s, _ = proj3.shape
    m = kaug.shape[2]
    n_pairs = WIDTH_M // LANES
    first = 3 * (WIDTH_A + WIDTH_B) // LANES
    kv_spec = pl.BlockSpec((None, 2, m, LANES), lambda i, p: (i, p, 0, 0))
    return pl.pallas_call(
        _attn_m_kernel, out_shape=jax.ShapeDtypeStruct((b, s, WIDTH_M), BF16), grid=(b, n_pairs),
        in_specs=[pl.BlockSpec((None, s, LANES), lambda i, p: (i, 0, first + p)), kv_spec, kv_spec,
                  pl.BlockSpec((2, LANES), lambda i, p: (0, 0)),
                  pl.BlockSpec((1, m), lambda i, p: (0, 0))],
        out_specs=pl.BlockSpec((None, s, LANES), lambda i, p: (i, 0, p)),
        compiler_params=_cparams(("parallel", "parallel")), name="attn_m",
    )(proj3, kaug, vaug, gains, shift)


def _mix_kernel(oa_ref, ob_ref, om_ref, x_ref, og_ref, wo_ref, gf_ref, wr_ref, br_ref,
                x1_ref, h3_ref, route_ref):
    rows = x_ref.shape[0]
    y = None
    off = 0
    for o_ref in (oa_ref, ob_ref, om_ref):
        width = o_ref.shape[1]
        o = o_ref[...].astype(F32)
        ms = jnp.mean(o * o, axis=-1, keepdims=True)
        mixed = (o * lax.rsqrt(ms + EPS) * og_ref[:, off:off + width]).astype(BF16)
        part = jnp.dot(mixed, wo_ref[off:off + width, :], preferred_element_type=F32)
        y = part if y is None else y + part
        off += width
    x1 = x_ref[...] + y
    x1_ref[...] = x1
    ms = jnp.mean(x1 * x1, axis=-1, keepdims=True)
    h = x1 * lax.rsqrt(ms + EPS) * gf_ref[...]
    for j in range(ROW_TILES):
        h3_ref[pl.ds(j, rows, stride=ROW_TILES), :] = h[:, j * LANES:(j + 1) * LANES]

    h_hi = h.astype(BF16)
    h_lo = (h - h_hi.astype(F32)).astype(BF16)
    logits = (jnp.dot(h_hi, wr_ref[0], preferred_element_type=F32)
              + jnp.dot(h_lo, wr_ref[0], preferred_element_type=F32)
              + jnp.dot(h_hi, wr_ref[1], preferred_element_type=F32)) + br_ref[...]
    lane = lax.broadcasted_iota(jnp.int32, (rows, LANES), 1)
    lane_f = lane.astype(F32)
    neg = -jnp.inf
    far = float(LANES)
    is_g = (lane >> 2) == (N_EXPERTS // N_GROUPS)
    gl = jnp.where(is_g, logits, neg)
    gmax = jnp.max(gl, axis=-1, keepdims=True)
    gidx = jnp.min(jnp.where(gl == gmax, lane_f, far), axis=-1, keepdims=True) - float(N_EXPERTS)
    gsum = jnp.sum(jnp.where(is_g, jnp.exp(gl - gmax), 0.0), axis=-1, keepdims=True)
    g_top = 1.0 / gsum
    in_grp = (lane >> 3).astype(F32) == gidx
    el = jnp.where(in_grp, logits, neg)
    v1 = jnp.max(el, axis=-1, keepdims=True)
    i1 = jnp.min(jnp.where(el == v1, lane_f, far), axis=-1, keepdims=True)
    el2 = jnp.where(lane_f == i1, neg, el)
    v2 = jnp.max(el2, axis=-1, keepdims=True)
    i2 = jnp.min(jnp.where(el2 == v2, lane_f, far), axis=-1, keepdims=True)
    ev = jnp.exp(v2 - v1)
    w1 = g_top / (1.0 + ev)
    w2 = g_top * ev / (1.0 + ev)
    route_ref[...] = jnp.where(lane == 0, i1, jnp.where(lane == 1, i2, jnp.where(lane == 2, w1,
                               jnp.where(lane == 3, w2, 0.0))))


def _mix(oa, ob, om, x2, out_gain, w_out, gain_ffn, w_route, b_route):
    t = x2.shape[0]

    def rows(width):
        return pl.BlockSpec((PROJ_ROWS, width), lambda i: (i, 0))

    def whole(*shape):
        return pl.BlockSpec(shape, lambda i: (0,) * len(shape))
    return pl.pallas_call(
        _mix_kernel,
        out_shape=(jax.ShapeDtypeStruct((t, D_MODEL), F32),
                   jax.ShapeDtypeStruct((t * ROW_TILES, LANES), F32),
                   jax.ShapeDtypeStruct((t, LANES), F32)),
        grid=(t // PROJ_ROWS,),
        in_specs=[rows(WIDTH_A), rows(WIDTH_B), rows(WIDTH_M), rows(D_MODEL),
                  whole(1, D_MODEL), whole(D_MODEL, D_MODEL), whole(1, D_MODEL),
                  whole(2, D_MODEL, LANES), whole(1, LANES)],
        out_specs=(rows(D_MODEL), pl.BlockSpec((PROJ_ROWS * ROW_TILES, LANES), lambda i: (i, 0)), rows(LANES)),
        compiler_params=_cparams(("parallel",)), name="mix",
    )(oa, ob, om, x2, out_gain, w_out, gain_ffn, w_route, b_route)


SCATTER_UNROLL = 8
W_SLOTS = 4
W_CHUNK_ROWS = 256


def _moe_weight_copies(wgu_hbm, wd_hbm, wgu_buf, wd_buf, sems, e, slot):
    copies = []
    for hbm, buf in ((wgu_hbm, wgu_buf), (wd_hbm, wd_buf)):
        for c in range(buf.shape[1] // W_CHUNK_ROWS):
            rows = pl.ds(c * W_CHUNK_ROWS, W_CHUNK_ROWS)
            copies.append(pltpu.make_async_copy(hbm.at[e, rows], buf.at[slot, rows], sems.at[slot]))
    return copies


def _moe_kernel(rstart_ref, tslot_ref, tfirst_ref, tnext_ref, npair_ref, act_ref,
                src_ref, dst_ref, gate_ref, h3_ref, wgu_hbm, wd_hbm, o_ref,
                wgu_buf, wd_buf, sems, xt0, xt1, yt0, yt1, acc):
    b = pl.program_id(0)
    n_tok = o_ref.shape[0]
    tiles_per_block = 2 * n_tok // MOE_TM + N_EXPERTS
    copies = functools.partial(_moe_weight_copies, wgu_hbm, wd_hbm, wgu_buf, wd_buf, sems)

    for k in range(W_SLOTS - 1):
        e0 = act_ref[b * (W_SLOTS - 1) + k]

        @pl.when(e0 >= 0)
        def _(e0=e0, k=k):
            for c in copies(e0, k):
                c.start()
    acc[...] = jnp.zeros_like(acc)

    def pair(p, carry):
        tiles = []
        for half, (xt, yt) in enumerate(((xt0, yt0), (xt1, yt1))):
            idx = b * tiles_per_block + 2 * p + half
            slot = tslot_ref[idx]

            @pl.when(tfirst_ref[idx] == 1)
            def _(slot=slot):
                for c in copies(0, slot):
                    c.wait()
            tiles.append((rstart_ref[idx], slot, tnext_ref[idx], xt, yt))

        for rs, _, _, xt, _ in tiles:
            for i in range(MOE_TM):
                off = pl.multiple_of(src_ref[0, rs + i], ROW_TILES)
                xt[i * ROW_TILES:(i + 1) * ROW_TILES, :] = h3_ref[pl.ds(off, ROW_TILES), :]

        for _, slot, _, xt, yt in tiles:
            lhs = jnp.concatenate([xt[pl.ds(k, MOE_TM, stride=ROW_TILES), :] for k in range(ROW_TILES)], axis=1)
            gu = jnp.dot(lhs.astype(BF16), wgu_buf[slot], preferred_element_type=F32)
            g = gu[:, :D_EXPERT]
            hid = (g * (1.0 / (1.0 + jnp.exp(-g)))) * gu[:, D_EXPERT:]
            y = jnp.dot(hid.astype(BF16), wd_buf[slot], preferred_element_type=F32)
            for k in range(ROW_TILES):
                yt[pl.ds(k, MOE_TM, stride=ROW_TILES), :] = y[:, k * LANES:(k + 1) * LANES]

        for rs, _, _, _, yt in tiles:
            for i0 in range(0, MOE_TM, SCATTER_UNROLL):
                vals = []
                for i in range(i0, i0 + SCATTER_UNROLL):
                    rows = pl.ds(pl.multiple_of(dst_ref[0, rs + i], ROW_TILES), ROW_TILES)
                    vals.append((rows, acc[rows, :] + gate_ref[0, rs + i] * yt[i * ROW_TILES:(i + 1) * ROW_TILES, :]))
                for rows, val in vals:
                    acc[rows, :] = val

        for _, slot, nxt, _, _ in tiles:
            @pl.when(nxt >= 0)
            def _(slot=slot, nxt=nxt):
                for c in copies(nxt, (slot + W_SLOTS - 1) % W_SLOTS):
                    c.start()
        return carry
    lax.fori_loop(0, npair_ref[b], pair, 0)

    for k in range(ROW_TILES):
        o_ref[:, k * LANES:(k + 1) * LANES] = acc[pl.ds(k, n_tok, stride=ROW_TILES), :]


def _moe(h3, route, wgu, wd, block_tokens):
    t = route.shape[0]
    nblk = t // block_tokens
    n_assign = 2 * block_tokens
    nt_max = n_assign // MOE_TM + N_EXPERTS

    n_pad = MOE_TM - 1
    stride = 2 * n_assign
    n_list = n_assign + N_EXPERTS * MOE_TM
    e = route[:, 0:2].astype(jnp.int32).reshape(nblk, n_assign)
    w = route[:, 2:4].reshape(nblk, n_assign)
    experts = jnp.arange(N_EXPERTS, dtype=jnp.int32)
    key = e * stride + jnp.arange(n_assign, dtype=jnp.int32)[None, :]
    pad_key = (experts[:, None] * stride + n_assign + jnp.arange(n_pad, dtype=jnp.int32)[None, :]).reshape(1, -1)
    tail_key = (N_EXPERTS * stride + n_assign + experts)[None, :]
    extra_key = jnp.broadcast_to(jnp.concatenate([pad_key, tail_key], axis=1), (nblk, n_list - n_assign))
    key_s, w_s = lax.sort((jnp.concatenate([key, extra_key], axis=1),
                           jnp.concatenate([w, jnp.zeros((nblk, n_list - n_assign), F32)], axis=1)),
                          dimension=1, num_keys=1)
    idx_s = key_s & (stride - 1)
    real = idx_s < n_assign
    row_off = (idx_s >> 1) * ROW_TILES
    src_s = jnp.where(real, row_off, 0)
    dst_s = jnp.where(real, row_off, block_tokens * ROW_TILES)
    cnt = jnp.sum((e[:, :, None] == experts[None, None, :]).astype(jnp.int32), axis=1)
    cstart = jnp.cumsum(cnt, axis=1) - cnt + experts[None, :] * n_pad
    ntile_e = (cnt + MOE_TM - 1) // MOE_TM
    tend = jnp.cumsum(ntile_e, axis=1)
    ntiles = tend[:, -1:]
    active = (ntile_e > 0).astype(jnp.int32)
    order_e = jnp.cumsum(active, axis=1) - active
    ranks = jnp.arange(N_EXPERTS + W_SLOTS, dtype=jnp.int32)
    hit = (order_e[:, None, :] == ranks[None, :, None]) & (active[:, None, :] == 1)
    act = jnp.sum(jnp.where(hit, experts[None, None, :] + 1, 0), axis=2) - 1
    tile_ids = jnp.arange(nt_max, dtype=jnp.int32)[None, :]
    live = tile_ids < ntiles
    texp = jnp.minimum(jnp.sum((tile_ids[:, :, None] >= tend[:, None, :]).astype(jnp.int32), axis=2), N_EXPERTS - 1)
    texp = jnp.where(live, texp, jnp.take_along_axis(texp, jnp.maximum(ntiles - 1, 0), axis=1))
    within = tile_ids - jnp.take_along_axis(tend - ntile_e, texp, axis=1)
    rstart = jnp.where(live, jnp.take_along_axis(cstart, texp, axis=1) + within * MOE_TM, n_list - MOE_TM)
    torder = jnp.take_along_axis(order_e, texp, axis=1)
    tfirst = (live & (within == 0)).astype(jnp.int32)
    tnext = jnp.where(tfirst == 1, jnp.take_along_axis(act, torder + (W_SLOTS - 1), axis=1), -1)
    npairs = (ntiles[:, 0] + 1) // 2

    def flat(x):
        return x.reshape(-1).astype(jnp.int32)

    def smem_list():
        return pl.BlockSpec((None, 1, n_list), lambda b, *_: (b, 0, 0), memory_space=pltpu.SMEM)
    tile_buf = pltpu.VMEM((MOE_TM * ROW_TILES, LANES), F32)
    grid_spec = pltpu.PrefetchScalarGridSpec(
        num_scalar_prefetch=6, grid=(nblk,),
        in_specs=[smem_list(), smem_list(), smem_list(),
                  pl.BlockSpec((block_tokens * ROW_TILES, LANES), lambda b, *_: (b, 0)),
                  pl.BlockSpec(memory_space=pl.ANY), pl.BlockSpec(memory_space=pl.ANY)],
        out_specs=pl.BlockSpec((block_tokens, D_MODEL), lambda b, *_: (b, 0)),
        scratch_shapes=[pltpu.VMEM((W_SLOTS, D_MODEL, 2 * D_EXPERT), BF16),
                        pltpu.VMEM((W_SLOTS, D_EXPERT, D_MODEL), BF16),
                        pltpu.SemaphoreType.DMA((W_SLOTS,)),
                        tile_buf, tile_buf, tile_buf, tile_buf,
                        pltpu.VMEM(((block_tokens + 1) * ROW_TILES, LANES), F32)])
    return pl.pallas_call(
        _moe_kernel, out_shape=jax.ShapeDtypeStruct((t, D_MODEL), F32), grid_spec=grid_spec,
        compiler_params=_cparams(("arbitrary",)), name="moe",
    )(flat(rstart), flat(torder % W_SLOTS), flat(tfirst), flat(tnext), flat(npairs), flat(act[:, :W_SLOTS - 1]),
      src_s.reshape(nblk, 1, n_list), dst_s.reshape(nblk, 1, n_list), w_s.reshape(nblk, 1, n_list), h3, wgu, wd)


def _add_kernel(a_ref, b_ref, o_ref):
    o_ref[...] = a_ref[...] + b_ref[...]


def _add(a, b):
    t = a.shape[0]
    spec = pl.BlockSpec((PROJ_ROWS, D_MODEL), lambda i: (i, 0))
    return pl.pallas_call(_add_kernel, out_shape=jax.ShapeDtypeStruct(a.shape, a.dtype), grid=(t // PROJ_ROWS,),
                          in_specs=[spec, spec], out_specs=spec, compiler_params=_cparams(("parallel",)),
                          name="residual_add")(a, b)


def _score_bound(gq, gk):
    return (HEAD_DIM * ATTN_SCALE * LOG2E) * jnp.max(jnp.abs(gq)) * jnp.max(jnp.abs(gk))


def _alibi_tables(shift):
    slopes = (2.0 ** (-8.0 * np.arange(1, N_HEADS_A + 1) / N_HEADS_A)).astype(np.float32)
    tables = []
    for _, dil in DILATED_PATTERNS:
        if dil == DILATED_PATTERNS[-1][1]:
            offs, kw = (0,), QBLK
        else:
            offs, kw = (0, DIL_RADIUS, 2 * DIL_RADIUS), 2 * QBLK
        tiles = []
        for off in offs:
            rel = np.abs((off + np.arange(QBLK))[:, None] - np.arange(kw)[None, :])
            tiles.append(np.where(rel <= DIL_RADIUS, rel * dil, np.inf).astype(np.float32))
        dist = np.stack(tiles)
        tab = -(slopes * LOG2E)[:, None, None, None] * dist[None]
        tab = jnp.asarray(tab) - shift
        tables.append(tab[:, 0] if len(offs) == 1 else tab)
    return tables


def _rpb_table(rpb, shift, rows_total):
    kh = min(NA_ROWS, rows_total)
    c = np.arange(GRID_W)
    c0 = np.clip(c - NA_COLS // 2, 0, GRID_W - NA_COLS)
    col_ok = (c[None, :] >= c0[:, None]) & (c[None, :] < c0[:, None] + NA_COLS)
    dc = np.clip(c[None, :] - c[:, None], -(NA_COLS - 1), NA_COLS - 1) + NA_COLS - 1
    type_rows = list(range(kh // 2)) + [kh // 2] + list(range(rows_total - kh // 2 + 1, rows_total))
    dr = np.array([[int(np.clip(r - kh // 2, 0, rows_total - kh)) + j - r + NA_ROWS - 1 for j in range(kh)]
                   for r in type_rows])
    n_dc = 2 * NA_COLS - 1
    pick = dc[None, None, :, :, None] == np.arange(n_dc)[None, None, None, None, :]
    cols = jnp.sum(jnp.where(pick, rpb[:, :, None, None, :], 0.0), axis=-1)
    tab = jnp.stack([cols[:, int(d)] for d in dr.reshape(-1)], axis=1).reshape(
        rpb.shape[0], len(type_rows), kh, GRID_W, GRID_W)
    tab = jnp.where(col_ok[None, None, None], tab, -jnp.inf)
    tab = tab.transpose(0, 1, 3, 2, 4).reshape(rpb.shape[0], len(type_rows), GRID_W, kh * GRID_W) * LOG2E
    return tab - shift[:, None, None, None]


def kernel(x, mem, norm_mix, w_in, qk_gain, rpb, norm_mem, w_mem_kv, out_gain, w_out, norm_ffn, w_group,
           b_group, w_router, b_router, w_gate, w_up, w_down):
    bsz, seq, d = x.shape
    t = bsz * seq
    depth = w_in.shape[0]
    rows_total = seq // GRID_W

    x2 = x.reshape(t, d)
    moe = None
    for l in range(depth):
        gains = jnp.tile(qk_gain[l], (1, 2))
        w_in_l = w_in[l].astype(BF16)
        if moe is None:
            proj = _proj(x2, None, norm_mix[l][None, :], w_in_l)
        else:
            x2, proj = _proj(x2, moe, norm_mix[l][None, :], w_in_l)
        proj3 = proj.reshape(bsz, seq, IN_WIDTH)

        shift_a = _score_bound(qk_gain[l, 0], qk_gain[l, 1])
        b1, b4, b16 = _alibi_tables(shift_a)
        oa = _attn_a(proj3, gains[0:2], b1, b4, b16)

        shift_b = _score_bound(qk_gain[l, 2], qk_gain[l, 3]) + LOG2E * jnp.max(rpb[l], axis=(1, 2))
        ob = _attn_b(proj3, gains[2:4], _rpb_table(rpb[l], shift_b, rows_total))

        kaug, vaug = _memkv(mem, norm_mem[l][None, :], w_mem_kv[l].astype(BF16), gains[5:6])
        shift_m = jnp.full((1, mem.shape[1]), _score_bound(qk_gain[l, 4], qk_gain[l, 5]), F32)
        om = _attn_m(proj3, kaug, vaug, gains[4:6], shift_m)

        w_route = jnp.zeros((d, LANES), F32).at[:, :N_EXPERTS].set(w_router[l])
        w_route = w_route.at[:, N_EXPERTS:N_EXPERTS + N_GROUPS].set(w_group[l])
        w_route_hi = w_route.astype(BF16)
        w_route_lo = (w_route - w_route_hi.astype(F32)).astype(BF16)
        b_route = jnp.zeros((1, LANES), F32).at[0, :N_EXPERTS].set(b_router[l])
        b_route = b_route.at[0, N_EXPERTS:N_EXPERTS + N_GROUPS].set(b_group[l])
        x2, h3, route = _mix(oa.reshape(t, WIDTH_A), ob.reshape(t, WIDTH_B), om.reshape(t, WIDTH_M), x2,
                             out_gain[l][None, :], w_out[l].astype(BF16), norm_ffn[l][None, :],
                             jnp.stack([w_route_hi, w_route_lo]), b_route)

        wgu = jnp.concatenate([w_gate[l], w_up[l]], axis=-1).astype(BF16)
        moe = _moe(h3, route, wgu, w_down[l].astype(BF16), seq)
    return _add(x2, moe).reshape(bsz, seq, d)
```

```python
import functools

import numpy as np
import jax
import jax.numpy as jnp
from jax import lax
from jax.experimental import pallas as pl
from jax.experimental.pallas import tpu as pltpu

F32 = jnp.float32
BF16 = jnp.bfloat16

D_MODEL = 1024
HEAD_DIM = 64
N_HEADS_A = 6
N_HEADS_B = 6
N_HEADS_M = 4
WIDTH_A = N_HEADS_A * HEAD_DIM
WIDTH_B = N_HEADS_B * HEAD_DIM
WIDTH_M = N_HEADS_M * HEAD_DIM
IN_WIDTH = 3 * WIDTH_A + 3 * WIDTH_B + WIDTH_M
DILATED_PATTERNS = ((128, 1), (512, 4), (2048, 16))
DIL_RADIUS = 64
GRID_W = 64
NA_ROWS = 8
NA_COLS = 16
N_GROUPS = 4
EXPERTS_PER_GROUP = 8
N_EXPERTS = N_GROUPS * EXPERTS_PER_GROUP
D_EXPERT = 512
EPS = 1e-6
ATTN_SCALE = HEAD_DIM ** -0.5
LOG2E = float(np.log2(np.e))
QSCALE = ATTN_SCALE * LOG2E

LANES = 128
SUBLANES = 8
ROW_TILES = D_MODEL // LANES
QBLK = 128
PROJ_ROWS = 512
MOE_TM = 128
MOE_BLOCK_TOKENS = 4096
VMEM_LIMIT = 56 * 1024 * 1024


def _cparams(sem):
    return pltpu.CompilerParams(dimension_semantics=sem, vmem_limit_bytes=VMEM_LIMIT)


def _from_token_tiles(t_ref):
    rows = t_ref.shape[0] // ROW_TILES
    return jnp.concatenate([t_ref[pl.ds(j, rows, stride=ROW_TILES), :] for j in range(ROW_TILES)], axis=1)


def _proj_kernel(*refs, add):
    if add:
        x_ref, m_ref, g_ref, w_ref, xo_ref, o_ref = refs
        x = x_ref[...] + _from_token_tiles(m_ref)
        xo_ref[...] = x
    else:
        x_ref, g_ref, w_ref, o_ref = refs
        x = x_ref[...]
    ms = jnp.mean(x * x, axis=-1, keepdims=True)
    h = (x * lax.rsqrt(ms + EPS) * g_ref[...]).astype(BF16)
    o_ref[...] = jnp.dot(h, w_ref[...], preferred_element_type=F32).astype(BF16)


def _proj(x2, moe2, gain, w):
    t = x2.shape[0]
    add = moe2 is not None
    row_spec = pl.BlockSpec((PROJ_ROWS, D_MODEL), lambda i: (i, 0))
    tile_spec = pl.BlockSpec((PROJ_ROWS * ROW_TILES, LANES), lambda i: (i, 0))
    in_specs = [row_spec] + ([tile_spec] if add else []) + [
        pl.BlockSpec((1, D_MODEL), lambda i: (0, 0)),
        pl.BlockSpec((D_MODEL, IN_WIDTH), lambda i: (0, 0)),
    ]
    proj_spec = pl.BlockSpec((PROJ_ROWS, IN_WIDTH), lambda i: (i, 0))
    proj_shape = jax.ShapeDtypeStruct((t, IN_WIDTH), BF16)
    if add:
        out_shape = (jax.ShapeDtypeStruct((t, D_MODEL), F32), proj_shape)
        out_specs = (row_spec, proj_spec)
        args = (x2, moe2, gain, w)
    else:
        out_shape, out_specs, args = proj_shape, proj_spec, (x2, gain, w)
    return pl.pallas_call(
        functools.partial(_proj_kernel, add=add),
        out_shape=out_shape, grid=(t // PROJ_ROWS,), in_specs=in_specs, out_specs=out_specs,
        compiler_params=_cparams(("parallel",)), name="proj_add" if add else "proj",
    )(*args)


def _data_mask(lane, par):
    return (lane < HEAD_DIM) if par == 0 else (lane >= HEAD_DIM)


def _ones_lane(par):
    return HEAD_DIM if par == 0 else 0


def _head_norm(x, dm, gain):
    ss = jnp.sum(jnp.where(dm, x * x, 0.0), axis=-1, keepdims=True)
    return x * lax.rsqrt(ss * (1.0 / HEAD_DIM) + EPS) * gain


def _normalise(tot, lane, par):
    den = jnp.sum(jnp.where(lane == _ones_lane(par), tot, 0.0), axis=-1, keepdims=True)
    return tot / den


def _memkv_kernel(m_ref, g_ref, w_ref, kg_ref, k_ref, v_ref):
    x = m_ref[...]
    ms = jnp.mean(x * x, axis=-1, keepdims=True)
    h = (x * lax.rsqrt(ms + EPS) * g_ref[...]).astype(BF16)
    kv = jnp.dot(h, w_ref[...], preferred_element_type=F32)
    lane = lax.broadcasted_iota(jnp.int32, (x.shape[0], LANES), 1)
    for slab in range(WIDTH_M // LANES):
        kf = kv[:, slab * LANES:(slab + 1) * LANES]
        vf = kv[:, WIDTH_M + slab * LANES:WIDTH_M + (slab + 1) * LANES]
        for par in (0, 1):
            dm = _data_mask(lane, par)
            kn = _head_norm(kf, dm, kg_ref[...])
            k_ref[2 * slab + par] = jnp.where(dm, kn, 0.0).astype(BF16)
            v_ref[2 * slab + par] = jnp.where(dm, vf, jnp.where(lane == _ones_lane(par), 1.0, 0.0)).astype(BF16)


def _memkv(mem, gain, w, kgain):
    b, m, _ = mem.shape
    kv_shape = jax.ShapeDtypeStruct((b, N_HEADS_M, m, LANES), BF16)
    kv_spec = pl.BlockSpec((None, N_HEADS_M, m, LANES), lambda i: (i, 0, 0, 0))
    return pl.pallas_call(
        _memkv_kernel, out_shape=(kv_shape, kv_shape), grid=(b,),
        in_specs=[pl.BlockSpec((None, m, D_MODEL), lambda i: (i, 0, 0)),
                  pl.BlockSpec((1, D_MODEL), lambda i: (0, 0)),
                  pl.BlockSpec((D_MODEL, 2 * WIDTH_M), lambda i: (0, 0)),
                  pl.BlockSpec((1, LANES), lambda i: (0, 0))],
        out_specs=(kv_spec, kv_spec), compiler_params=_cparams(("parallel",)), name="memkv",
    )(mem, gain, w, kgain)


PREP_ROWS = 256
A_UNROLL = 8


def _attn_a_kernel(q_ref, k_ref, v_ref, g_ref, b1_ref, b4_ref, b16_ref, o_ref,
                   qs, ks, vs, qm, km, vm, qd, kd, vd, accd, accm, tot, s_scr):
    s_len = q_ref.shape[0]
    n_prep = s_len // PREP_ROWS
    lane_p = lax.broadcasted_iota(jnp.int32, (PREP_ROWS, LANES), 1)
    gq = g_ref[0:1, :]
    gk = g_ref[1:2, :]

    for par in (0, 1):
        def prep(c, carry, par=par):
            rows = pl.ds(pl.multiple_of(c * PREP_ROWS, PREP_ROWS), PREP_ROWS)
            dm = _data_mask(lane_p, par)
            qn = _head_norm(q_ref[rows, :].astype(F32), dm, gq)
            kn = _head_norm(k_ref[rows, :].astype(F32), dm, gk)
            vf = v_ref[rows, :].astype(F32)
            qs[rows, :] = jnp.where(dm, qn * QSCALE, 0.0)
            ks[rows, :] = jnp.where(dm, kn, 0.0)
            vs[rows, :] = jnp.where(dm, vf, jnp.where(lane_p == _ones_lane(par), 1.0, 0.0))
            return carry
        lax.fori_loop(0, n_prep, prep, 0)

        def run_blocks(length, b_ref, out, par=par):
            nblk = length // QBLK

            def blocks(o, carry):
                units = []
                for i in range(A_UNROLL):
                    u = o * A_UNROLL + i
                    if nblk == 1:
                        base = pl.multiple_of(u * length, QBLK)
                        qrows = pl.ds(base, QBLK)
                        keys = pl.ds(base, QBLK)
                        bias = b_ref[par]
                        kw = QBLK
                    else:
                        r = u // nblk
                        blk = u % nblk
                        q0 = blk * QBLK
                        ws = jnp.clip(q0 - DIL_RADIUS, 0, length - 2 * QBLK)
                        tid = jnp.where(blk == 0, 0, jnp.where(blk == nblk - 1, 2, 1))
                        qrows = pl.ds(pl.multiple_of(r * length + q0, QBLK), QBLK)
                        keys = pl.ds(pl.multiple_of(r * length + ws, DIL_RADIUS), 2 * QBLK)
                        bias = b_ref[par, tid]
                        kw = 2 * QBLK
                    s_scr[i, :, 0:kw] = lax.dot_general(qd[qrows, :], kd[keys, :], (((1,), (1,)), ((), ())),
                                                        preferred_element_type=F32)
                    units.append((qrows, keys, bias, kw))
                for i, (qrows, keys, bias, kw) in enumerate(units):
                    p = jnp.exp2(s_scr[i, :, 0:kw] + bias).astype(BF16)
                    out[qrows, :] = jnp.dot(p, vd[keys, :], preferred_element_type=F32)
                return carry
            lax.fori_loop(0, s_len // QBLK // A_UNROLL, blocks, 0)

        d_mid, d_far = DILATED_PATTERNS[1][1], DILATED_PATTERNS[2][1]
        len_mid, len_far = s_len // d_mid, s_len // d_far
        sub = d_far // d_mid
        arrays = ((qs, qm, qd), (ks, km, kd), (vs, vm, vd))

        def cast(c, carry):
            rows = pl.ds(pl.multiple_of(c * PREP_ROWS, PREP_ROWS), PREP_ROWS)
            for x_tok, _, x_bf in arrays:
                x_bf[rows, :] = x_tok[rows, :].astype(BF16)
            return carry
        lax.fori_loop(0, n_prep, cast, 0)
        run_blocks(s_len, b1_ref, accd)

        def first(c, carry, par=par):
            rows = pl.ds(pl.multiple_of(c * PREP_ROWS, PREP_ROWS), PREP_ROWS)
            tot[par, rows, :] = accd[rows, :]
            return carry
        lax.fori_loop(0, n_prep, first, 0)

        def split_mid(r, carry):
            dst = pl.ds(pl.multiple_of(r * len_mid, QBLK), len_mid)
            src = pl.ds(r, len_mid, stride=d_mid)
            for x_tok, x_mid, x_bf in arrays:
                x = x_tok[src, :]
                x_mid[dst, :] = x
                x_bf[dst, :] = x.astype(BF16)
            return carry
        lax.fori_loop(0, d_mid, split_mid, 0)
        run_blocks(len_mid, b4_ref, accm)

        def far_rows(c):
            mid = pl.ds((c // sub) * len_mid + c % sub, len_far, stride=sub)
            far = pl.ds(pl.multiple_of(c * len_far, QBLK), len_far)
            return mid, far

        def split_far(c, carry):
            mid, far = far_rows(c)
            for _, x_mid, x_bf in arrays:
                x_bf[far, :] = x_mid[mid, :].astype(BF16)
            return carry
        lax.fori_loop(0, d_far, split_far, 0)
        run_blocks(len_far, b16_ref, accd)

        def merge_far(c, carry):
            mid, far = far_rows(c)
            accm[mid, :] = accm[mid, :] + accd[far, :]
            return carry
        lax.fori_loop(0, d_far, merge_far, 0)

        def merge_mid(r, carry, par=par):
            src = pl.ds(pl.multiple_of(r * len_mid, QBLK), len_mid)
            dst = pl.ds(r, len_mid, stride=d_mid)
            tot[par, dst, :] = tot[par, dst, :] + accm[src, :]
            return carry
        lax.fori_loop(0, d_mid, merge_mid, 0)

    def fin(c, carry):
        rows = pl.ds(pl.multiple_of(c * PREP_ROWS, PREP_ROWS), PREP_ROWS)
        o0 = _normalise(tot[0, rows, :], lane_p, 0)
        o1 = _normalise(tot[1, rows, :], lane_p, 1)
        o_ref[rows, :] = jnp.where(lane_p < HEAD_DIM, o0, o1).astype(BF16)
        return carry
    lax.fori_loop(0, n_prep, fin, 0)


def _attn_a(proj3, gains, b1, b4, b16):
    b, s, _ = proj3.shape
    n_pairs = WIDTH_A // LANES

    def slab(off):
        return pl.BlockSpec((None, s, LANES), lambda i, p: (i, 0, off + p))
    return pl.pallas_call(
        _attn_a_kernel, out_shape=jax.ShapeDtypeStruct((b, s, WIDTH_A), BF16), grid=(b, n_pairs),
        in_specs=[slab(0), slab(n_pairs), slab(2 * n_pairs),
                  pl.BlockSpec((2, LANES), lambda i, p: (0, 0)),
                  pl.BlockSpec((2, 3, QBLK, 2 * QBLK), lambda i, p: (p, 0, 0, 0)),
                  pl.BlockSpec((2, 3, QBLK, 2 * QBLK), lambda i, p: (p, 0, 0, 0)),
                  pl.BlockSpec((2, QBLK, QBLK), lambda i, p: (p, 0, 0))],
        out_specs=pl.BlockSpec((None, s, LANES), lambda i, p: (i, 0, p)),
        scratch_shapes=[pltpu.VMEM((s, LANES), F32)] * 6 + [pltpu.VMEM((s, LANES), BF16)] * 3
        + [pltpu.VMEM((s, LANES), F32)] * 2
        + [pltpu.VMEM((2, s, LANES), F32), pltpu.VMEM((A_UNROLL, QBLK, 2 * QBLK), F32)],
        compiler_params=_cparams(("parallel", "parallel")), name="attn_a",
    )(proj3, proj3, proj3, gains, b1, b4, b16)


def _attn_b_kernel(q_ref, k_ref, v_ref, g_ref, bias_ref, o_ref, qb, kb, vb, s_scr):
    s_len = q_ref.shape[0]
    n_prep = s_len // PREP_ROWS
    rows_total = s_len // GRID_W
    kh = min(NA_ROWS, rows_total)
    lane_p = lax.broadcasted_iota(jnp.int32, (PREP_ROWS, LANES), 1)
    lane_q = lax.broadcasted_iota(jnp.int32, (GRID_W, LANES), 1)
    gq = g_ref[0:1, :]
    gk = g_ref[1:2, :]

    for par in (0, 1):
        def prep(c, carry, par=par):
            rows = pl.ds(pl.multiple_of(c * PREP_ROWS, PREP_ROWS), PREP_ROWS)
            dm = _data_mask(lane_p, par)
            qn = _head_norm(q_ref[rows, :].astype(F32), dm, gq)
            kn = _head_norm(k_ref[rows, :].astype(F32), dm, gk)
            vf = v_ref[rows, :].astype(F32)
            qb[par, rows, :] = jnp.where(dm, qn * QSCALE, 0.0).astype(BF16)
            kb[par, rows, :] = jnp.where(dm, kn, 0.0).astype(BF16)
            vb[par, rows, :] = jnp.where(dm, vf, jnp.where(lane_p == _ones_lane(par), 1.0, 0.0)).astype(BF16)
            return carry
        lax.fori_loop(0, n_prep, prep, 0)

    unroll = 4

    def grid_rows(o, carry):
        units = []
        for i in range(unroll):
            r = o * unroll + i
            r0 = jnp.clip(r - kh // 2, 0, rows_total - kh)
            typ = jnp.where(r < kh // 2, r, jnp.where(r > rows_total - kh // 2, r - (rows_total - kh), kh // 2))
            qrows = pl.ds(pl.multiple_of(r * GRID_W, GRID_W), GRID_W)
            keys = pl.ds(pl.multiple_of(r0 * GRID_W, GRID_W), kh * GRID_W)
            for par in (0, 1):
                s_scr[2 * i + par] = lax.dot_general(qb[par, qrows, :], kb[par, keys, :], (((1,), (1,)), ((), ())),
                                                     preferred_element_type=F32)
            units.append((qrows, keys, typ))
        for i, (qrows, keys, typ) in enumerate(units):
            outs = []
            for par in (0, 1):
                p = jnp.exp2(s_scr[2 * i + par] + bias_ref[par, typ]).astype(BF16)
                acc = jnp.dot(p, vb[par, keys, :], preferred_element_type=F32)
                outs.append(_normalise(acc, lane_q, par))
            o_ref[qrows, :] = jnp.where(lane_q < HEAD_DIM, outs[0], outs[1]).astype(BF16)
        return carry
    lax.fori_loop(0, rows_total // unroll, grid_rows, 0)


def _attn_b(proj3, gains, bias):
    b, s, _ = proj3.shape
    n_pairs = WIDTH_B // LANES
    first = 3 * WIDTH_A // LANES
    n_types, kw = bias.shape[1], bias.shape[3]

    def slab(off):
        return pl.BlockSpec((None, s, LANES), lambda i, p: (i, 0, first + off + p))
    return pl.pallas_call(
        _attn_b_kernel, out_shape=jax.ShapeDtypeStruct((b, s, WIDTH_B), BF16), grid=(b, n_pairs),
        in_specs=[slab(0), slab(n_pairs), slab(2 * n_pairs),
                  pl.BlockSpec((2, LANES), lambda i, p: (0, 0)),
                  pl.BlockSpec((2, n_types, GRID_W, kw), lambda i, p: (p, 0, 0, 0))],
        out_specs=pl.BlockSpec((None, s, LANES), lambda i, p: (i, 0, p)),
        scratch_shapes=[pltpu.VMEM((2, s, LANES), BF16)] * 3 + [pltpu.VMEM((8, GRID_W, kw), F32)],
        compiler_params=_cparams(("parallel", "parallel")), name="attn_b",
    )(proj3, proj3, proj3, gains, bias)


MEM_QROWS = 256


def _attn_m_kernel(q_ref, k_ref, v_ref, g_ref, shift_ref, o_ref):
    s_len = q_ref.shape[0]
    lane = lax.broadcasted_iota(jnp.int32, (MEM_QROWS, LANES), 1)
    gq = g_ref[0:1, :]

    def chunk(c, carry):
        rows = pl.ds(pl.multiple_of(c * MEM_QROWS, MEM_QROWS), MEM_QROWS)
        qf = q_ref[rows, :].astype(F32)
        outs = []
        for par in (0, 1):
            dm = _data_mask(lane, par)
            qn = jnp.where(dm, _head_norm(qf, dm, gq) * QSCALE, 0.0).astype(BF16)
            s = lax.dot_general(qn, k_ref[par], (((1,), (1,)), ((), ())), preferred_element_type=F32)
            p = jnp.exp2(s - shift_ref[...]).astype(BF16)
            acc = jnp.dot(p, v_ref[par], preferred_element_type=F32)
            outs.append(_normalise(acc, lane, par))
        o_ref[rows, :] = jnp.where(lane < HEAD_DIM, outs[0], outs[1]).astype(BF16)
        return carry
    lax.fori_loop(0, s_len // MEM_QROWS, chunk, 0)


def _attn_m(proj3, kaug, vaug, gains, shift):
    b, s, _ = proj3.shape
    m = kaug.shape[2]
    n_pairs = WIDTH_M // LANES
    first = 3 * (WIDTH_A + WIDTH_B) // LANES
    kv_spec = pl.BlockSpec((None, 2, m, LANES), lambda i, p: (i, p, 0, 0))
    return pl.pallas_call(
        _attn_m_kernel, out_shape=jax.ShapeDtypeStruct((b, s, WIDTH_M), BF16), grid=(b, n_pairs),
        in_specs=[pl.BlockSpec((None, s, LANES), lambda i, p: (i, 0, first + p)), kv_spec, kv_spec,
                  pl.BlockSpec((2, LANES), lambda i, p: (0, 0)),
                  pl.BlockSpec((1, m), lambda i, p: (0, 0))],
        out_specs=pl.BlockSpec((None, s, LANES), lambda i, p: (i, 0, p)),
        compiler_params=_cparams(("parallel", "parallel")), name="attn_m",
    )(proj3, kaug, vaug, gains, shift)


def _mix_kernel(oa_ref, ob_ref, om_ref, x_ref, og_ref, wo_ref, gf_ref, wr_ref, br_ref,
                x1_ref, h3_ref, route_ref):
    rows = x_ref.shape[0]
    y = None
    off = 0
    for o_ref in (oa_ref, ob_ref, om_ref):
        width = o_ref.shape[1]
        o = o_ref[...].astype(F32)
        ms = jnp.mean(o * o, axis=-1, keepdims=True)
        mixed = (o * lax.rsqrt(ms + EPS) * og_ref[:, off:off + width]).astype(BF16)
        part = jnp.dot(mixed, wo_ref[off:off + width, :], preferred_element_type=F32)
        y = part if y is None else y + part
        off += width
    x1 = x_ref[...] + y
    x1_ref[...] = x1
    ms = jnp.mean(x1 * x1, axis=-1, keepdims=True)
    h = x1 * lax.rsqrt(ms + EPS) * gf_ref[...]
    for j in range(ROW_TILES):
        h3_ref[pl.ds(j, rows, stride=ROW_TILES), :] = h[:, j * LANES:(j + 1) * LANES]

    h_hi = h.astype(BF16)
    h_lo = (h - h_hi.astype(F32)).astype(BF16)
    logits = (jnp.dot(h_hi, wr_ref[0], preferred_element_type=F32)
              + jnp.dot(h_lo, wr_ref[0], preferred_element_type=F32)
              + jnp.dot(h_hi, wr_ref[1], preferred_element_type=F32)) + br_ref[...]
    lane = lax.broadcasted_iota(jnp.int32, (rows, LANES), 1)
    lane_f = lane.astype(F32)
    neg = -jnp.inf
    far = float(LANES)
    is_g = (lane >> 2) == (N_EXPERTS // N_GROUPS)
    gl = jnp.where(is_g, logits, neg)
    gmax = jnp.max(gl, axis=-1, keepdims=True)
    gidx = jnp.min(jnp.where(gl == gmax, lane_f, far), axis=-1, keepdims=True) - float(N_EXPERTS)
    gsum = jnp.sum(jnp.where(is_g, jnp.exp(gl - gmax), 0.0), axis=-1, keepdims=True)
    g_top = 1.0 / gsum
    in_grp = (lane >> 3).astype(F32) == gidx
    el = jnp.where(in_grp, logits, neg)
    v1 = jnp.max(el, axis=-1, keepdims=True)
    i1 = jnp.min(jnp.where(el == v1, lane_f, far), axis=-1, keepdims=True)
    el2 = jnp.where(lane_f == i1, neg, el)
    v2 = jnp.max(el2, axis=-1, keepdims=True)
    i2 = jnp.min(jnp.where(el2 == v2, lane_f, far), axis=-1, keepdims=True)
    ev = jnp.exp(v2 - v1)
    w1 = g_top / (1.0 + ev)
    w2 = g_top * ev / (1.0 + ev)
    route_ref[...] = jnp.where(lane == 0, i1, jnp.where(lane == 1, i2, jnp.where(lane == 2, w1,
                               jnp.where(lane == 3, w2, 0.0))))


def _mix(oa, ob, om, x2, out_gain, w_out, gain_ffn, w_route, b_route):
    t = x2.shape[0]

    def rows(width):
        return pl.BlockSpec((PROJ_ROWS, width), lambda i: (i, 0))

    def whole(*shape):
        return pl.BlockSpec(shape, lambda i: (0,) * len(shape))
    return pl.pallas_call(
        _mix_kernel,
        out_shape=(jax.ShapeDtypeStruct((t, D_MODEL), F32),
                   jax.ShapeDtypeStruct((t * ROW_TILES, LANES), F32),
                   jax.ShapeDtypeStruct((t, LANES), F32)),
        grid=(t // PROJ_ROWS,),
        in_specs=[rows(WIDTH_A), rows(WIDTH_B), rows(WIDTH_M), rows(D_MODEL),
                  whole(1, D_MODEL), whole(D_MODEL, D_MODEL), whole(1, D_MODEL),
                  whole(2, D_MODEL, LANES), whole(1, LANES)],
        out_specs=(rows(D_MODEL), pl.BlockSpec((PROJ_ROWS * ROW_TILES, LANES), lambda i: (i, 0)), rows(LANES)),
        compiler_params=_cparams(("parallel",)), name="mix",
    )(oa, ob, om, x2, out_gain, w_out, gain_ffn, w_route, b_route)


SCATTER_UNROLL = 8
W_SLOTS = 4
W_CHUNK_ROWS = 256


def _moe_weight_copies(wgu_hbm, wd_hbm, wgu_buf, wd_buf, sems, e, slot):
    copies = []
    for hbm, buf in ((wgu_hbm, wgu_buf), (wd_hbm, wd_buf)):
        for c in range(buf.shape[1] // W_CHUNK_ROWS):
            rows = pl.ds(c * W_CHUNK_ROWS, W_CHUNK_ROWS)
            copies.append(pltpu.make_async_copy(hbm.at[e, rows], buf.at[slot, rows], sems.at[slot]))
    return copies


def _moe_kernel(rstart_ref, tslot_ref, tfirst_ref, tnext_ref, npair_ref, act_ref,
                src_ref, dst_ref, gate_ref, h3_ref, wgu_hbm, wd_hbm, o_hbm,
                wgu_buf, wd_buf, sems, out_sem, xt0, xt1, yt0, yt1, acc):
    b = pl.program_id(0)
    n_rows = h3_ref.shape[0]
    tiles_per_block = 2 * (n_rows // ROW_TILES) // MOE_TM + N_EXPERTS
    copies = functools.partial(_moe_weight_copies, wgu_hbm, wd_hbm, wgu_buf, wd_buf, sems)

    for k in range(W_SLOTS - 1):
        e0 = act_ref[b * (W_SLOTS - 1) + k]

        @pl.when(e0 >= 0)
        def _(e0=e0, k=k):
            for c in copies(e0, k):
                c.start()
    acc[...] = jnp.zeros_like(acc)

    def pair(p, carry):
        tiles = []
        for half, (xt, yt) in enumerate(((xt0, yt0), (xt1, yt1))):
            idx = b * tiles_per_block + 2 * p + half
            slot = tslot_ref[idx]

            @pl.when(tfirst_ref[idx] == 1)
            def _(slot=slot):
                for c in copies(0, slot):
                    c.wait()
            tiles.append((rstart_ref[idx], slot, tnext_ref[idx], xt, yt))

        for rs, _, _, xt, _ in tiles:
            for i in range(MOE_TM):
                off = pl.multiple_of(src_ref[0, rs + i], ROW_TILES)
                xt[i * ROW_TILES:(i + 1) * ROW_TILES, :] = h3_ref[pl.ds(off, ROW_TILES), :]

        for _, slot, _, xt, yt in tiles:
            lhs = jnp.concatenate([xt[pl.ds(k, MOE_TM, stride=ROW_TILES), :] for k in range(ROW_TILES)], axis=1)
            gu = jnp.dot(lhs.astype(BF16), wgu_buf[slot], preferred_element_type=F32)
            g = gu[:, :D_EXPERT]
            hid = (g * (1.0 / (1.0 + jnp.exp(-g)))) * gu[:, D_EXPERT:]
            y = jnp.dot(hid.astype(BF16), wd_buf[slot], preferred_element_type=F32)
            for k in range(ROW_TILES):
                yt[pl.ds(k, MOE_TM, stride=ROW_TILES), :] = y[:, k * LANES:(k + 1) * LANES]

        for rs, _, _, _, yt in tiles:
            for i0 in range(0, MOE_TM, SCATTER_UNROLL):
                vals = []
                for i in range(i0, i0 + SCATTER_UNROLL):
                    rows = pl.ds(pl.multiple_of(dst_ref[0, rs + i], ROW_TILES), ROW_TILES)
                    vals.append((rows, acc[rows, :] + gate_ref[0, rs + i] * yt[i * ROW_TILES:(i + 1) * ROW_TILES, :]))
                for rows, val in vals:
                    acc[rows, :] = val

        for _, slot, nxt, _, _ in tiles:
            @pl.when(nxt >= 0)
            def _(slot=slot, nxt=nxt):
                for c in copies(nxt, (slot + W_SLOTS - 1) % W_SLOTS):
                    c.start()
        return carry
    lax.fori_loop(0, npair_ref[b], pair, 0)

    out_copy = pltpu.make_async_copy(acc.at[pl.ds(0, n_rows)],
                                     o_hbm.at[pl.ds(pl.multiple_of(b * n_rows, ROW_TILES), n_rows)], out_sem)
    out_copy.start()
    out_copy.wait()


def _moe(h3, route, wgu, wd, block_tokens):
    t = route.shape[0]
    nblk = t // block_tokens
    n_assign = 2 * block_tokens
    nt_max = n_assign // MOE_TM + N_EXPERTS

    n_pad = MOE_TM - 1
    stride = 2 * n_assign
    n_list = n_assign + N_EXPERTS * MOE_TM
    e = route[:, 0:2].astype(jnp.int32).reshape(nblk, n_assign)
    w = route[:, 2:4].reshape(nblk, n_assign)
    experts = jnp.arange(N_EXPERTS, dtype=jnp.int32)
    key = e * stride + jnp.arange(n_assign, dtype=jnp.int32)[None, :]
    pad_key = (experts[:, None] * stride + n_assign + jnp.arange(n_pad, dtype=jnp.int32)[None, :]).reshape(1, -1)
    tail_key = (N_EXPERTS * stride + n_assign + experts)[None, :]
    extra_key = jnp.broadcast_to(jnp.concatenate([pad_key, tail_key], axis=1), (nblk, n_list - n_assign))
    key_s, w_s = lax.sort((jnp.concatenate([key, extra_key], axis=1),
                           jnp.concatenate([w, jnp.zeros((nblk, n_list - n_assign), F32)], axis=1)),
                          dimension=1, num_keys=1)
    idx_s = key_s & (stride - 1)
    real = idx_s < n_assign
    row_off = (idx_s >> 1) * ROW_TILES
    src_s = jnp.where(real, row_off, 0)
    dst_s = jnp.where(real, row_off, block_tokens * ROW_TILES)
    cnt = jnp.sum((e[:, :, None] == experts[None, None, :]).astype(jnp.int32), axis=1)
    cstart = jnp.cumsum(cnt, axis=1) - cnt + experts[None, :] * n_pad
    ntile_e = (cnt + MOE_TM - 1) // MOE_TM
    tend = jnp.cumsum(ntile_e, axis=1)
    ntiles = tend[:, -1:]
    active = (ntile_e > 0).astype(jnp.int32)
    order_e = jnp.cumsum(active, axis=1) - active
    ranks = jnp.arange(N_EXPERTS + W_SLOTS, dtype=jnp.int32)
    hit = (order_e[:, None, :] == ranks[None, :, None]) & (active[:, None, :] == 1)
    act = jnp.sum(jnp.where(hit, experts[None, None, :] + 1, 0), axis=2) - 1
    tile_ids = jnp.arange(nt_max, dtype=jnp.int32)[None, :]
    live = tile_ids < ntiles
    texp = jnp.minimum(jnp.sum((tile_ids[:, :, None] >= tend[:, None, :]).astype(jnp.int32), axis=2), N_EXPERTS - 1)
    texp = jnp.where(live, texp, jnp.take_along_axis(texp, jnp.maximum(ntiles - 1, 0), axis=1))
    within = tile_ids - jnp.take_along_axis(tend - ntile_e, texp, axis=1)
    rstart = jnp.where(live, jnp.take_along_axis(cstart, texp, axis=1) + within * MOE_TM, n_list - MOE_TM)
    torder = jnp.take_along_axis(order_e, texp, axis=1)
    tfirst = (live & (within == 0)).astype(jnp.int32)
    tnext = jnp.where(tfirst == 1, jnp.take_along_axis(act, torder + (W_SLOTS - 1), axis=1), -1)
    npairs = (ntiles[:, 0] + 1) // 2

    def flat(x):
        return x.reshape(-1).astype(jnp.int32)

    def smem_list():
        return pl.BlockSpec((None, 1, n_list), lambda b, *_: (b, 0, 0), memory_space=pltpu.SMEM)
    tile_buf = pltpu.VMEM((MOE_TM * ROW_TILES, LANES), F32)
    grid_spec = pltpu.PrefetchScalarGridSpec(
        num_scalar_prefetch=6, grid=(nblk,),
        in_specs=[smem_list(), smem_list(), smem_list(),
                  pl.BlockSpec((block_tokens * ROW_TILES, LANES), lambda b, *_: (b, 0),
                               pipeline_mode=pl.Buffered(1)),
                  pl.BlockSpec(memory_space=pl.ANY), pl.BlockSpec(memory_space=pl.ANY)],
        out_specs=pl.BlockSpec(memory_space=pl.ANY),
        scratch_shapes=[pltpu.VMEM((W_SLOTS, D_MODEL, 2 * D_EXPERT), BF16),
                        pltpu.VMEM((W_SLOTS, D_EXPERT, D_MODEL), BF16),
                        pltpu.SemaphoreType.DMA((W_SLOTS,)), pltpu.SemaphoreType.DMA(()),
                        tile_buf, tile_buf, tile_buf, tile_buf,
                        pltpu.VMEM(((block_tokens + 1) * ROW_TILES, LANES), F32)])
    return pl.pallas_call(
        _moe_kernel, out_shape=jax.ShapeDtypeStruct((t * ROW_TILES, LANES), F32), grid_spec=grid_spec,
        compiler_params=_cparams(("arbitrary",)), name="moe",
    )(flat(rstart), flat(torder % W_SLOTS), flat(tfirst), flat(tnext), flat(npairs), flat(act[:, :W_SLOTS - 1]),
      src_s.reshape(nblk, 1, n_list), dst_s.reshape(nblk, 1, n_list), w_s.reshape(nblk, 1, n_list), h3, wgu, wd)


def _add_kernel(a_ref, b_ref, o_ref):
    o_ref[...] = a_ref[...] + _from_token_tiles(b_ref)


def _add(a, b_tiles):
    t = a.shape[0]
    spec = pl.BlockSpec((PROJ_ROWS, D_MODEL), lambda i: (i, 0))
    tile_spec = pl.BlockSpec((PROJ_ROWS * ROW_TILES, LANES), lambda i: (i, 0))
    return pl.pallas_call(_add_kernel, out_shape=jax.ShapeDtypeStruct(a.shape, a.dtype), grid=(t // PROJ_ROWS,),
                          in_specs=[spec, tile_spec], out_specs=spec, compiler_params=_cparams(("parallel",)),
                          name="residual_add")(a, b_tiles)


def _score_bound(gq, gk):
    return (HEAD_DIM * ATTN_SCALE * LOG2E) * jnp.max(jnp.abs(gq)) * jnp.max(jnp.abs(gk))


def _alibi_tables(shift):
    slopes = (2.0 ** (-8.0 * np.arange(1, N_HEADS_A + 1) / N_HEADS_A)).astype(np.float32)
    tables = []
    for _, dil in DILATED_PATTERNS:
        if dil == DILATED_PATTERNS[-1][1]:
            offs, kw = (0,), QBLK
        else:
            offs, kw = (0, DIL_RADIUS, 2 * DIL_RADIUS), 2 * QBLK
        tiles = []
        for off in offs:
            rel = np.abs((off + np.arange(QBLK))[:, None] - np.arange(kw)[None, :])
            tiles.append(np.where(rel <= DIL_RADIUS, rel * dil, np.inf).astype(np.float32))
        dist = np.stack(tiles)
        tab = -(slopes * LOG2E)[:, None, None, None] * dist[None]
        tab = jnp.asarray(tab) - shift
        tables.append(tab[:, 0] if len(offs) == 1 else tab)
    return tables


def _rpb_table(rpb, shift, rows_total):
    kh = min(NA_ROWS, rows_total)
    c = np.arange(GRID_W)
    c0 = np.clip(c - NA_COLS // 2, 0, GRID_W - NA_COLS)
    col_ok = (c[None, :] >= c0[:, None]) & (c[None, :] < c0[:, None] + NA_COLS)
    dc = np.clip(c[None, :] - c[:, None], -(NA_COLS - 1), NA_COLS - 1) + NA_COLS - 1
    type_rows = list(range(kh // 2)) + [kh // 2] + list(range(rows_total - kh // 2 + 1, rows_total))
    dr = np.array([[int(np.clip(r - kh // 2, 0, rows_total - kh)) + j - r + NA_ROWS - 1 for j in range(kh)]
                   for r in type_rows])
    n_dc = 2 * NA_COLS - 1
    pick = dc[None, None, :, :, None] == np.arange(n_dc)[None, None, None, None, :]
    cols = jnp.sum(jnp.where(pick, rpb[:, :, None, None, :], 0.0), axis=-1)
    tab = jnp.stack([cols[:, int(d)] for d in dr.reshape(-1)], axis=1).reshape(
        rpb.shape[0], len(type_rows), kh, GRID_W, GRID_W)
    tab = jnp.where(col_ok[None, None, None], tab, -jnp.inf)
    tab = tab.transpose(0, 1, 3, 2, 4).reshape(rpb.shape[0], len(type_rows), GRID_W, kh * GRID_W) * LOG2E
    return tab - shift[:, None, None, None]


def kernel(x, mem, norm_mix, w_in, qk_gain, rpb, norm_mem, w_mem_kv, out_gain, w_out, norm_ffn, w_group,
           b_group, w_router, b_router, w_gate, w_up, w_down):
    bsz, seq, d = x.shape
    t = bsz * seq
    depth = w_in.shape[0]
    rows_total = seq // GRID_W

    x2 = x.reshape(t, d)
    moe = None
    for l in range(depth):
        gains = jnp.tile(qk_gain[l], (1, 2))
        w_in_l = w_in[l].astype(BF16)
        if moe is None:
            proj = _proj(x2, None, norm_mix[l][None, :], w_in_l)
        else:
            x2, proj = _proj(x2, moe, norm_mix[l][None, :], w_in_l)
        proj3 = proj.reshape(bsz, seq, IN_WIDTH)

        shift_a = _score_bound(qk_gain[l, 0], qk_gain[l, 1])
        b1, b4, b16 = _alibi_tables(shift_a)
        oa = _attn_a(proj3, gains[0:2], b1, b4, b16)

        shift_b = _score_bound(qk_gain[l, 2], qk_gain[l, 3]) + LOG2E * jnp.max(rpb[l], axis=(1, 2))
        ob = _attn_b(proj3, gains[2:4], _rpb_table(rpb[l], shift_b, rows_total))

        kaug, vaug = _memkv(mem, norm_mem[l][None, :], w_mem_kv[l].astype(BF16), gains[5:6])
        shift_m = jnp.full((1, mem.shape[1]), _score_bound(qk_gain[l, 4], qk_gain[l, 5]), F32)
        om = _attn_m(proj3, kaug, vaug, gains[4:6], shift_m)

        w_route = jnp.zeros((d, LANES), F32).at[:, :N_EXPERTS].set(w_router[l])
        w_route = w_route.at[:, N_EXPERTS:N_EXPERTS + N_GROUPS].set(w_group[l])
        w_route_hi = w_route.astype(BF16)
        w_route_lo = (w_route - w_route_hi.astype(F32)).astype(BF16)
        b_route = jnp.zeros((1, LANES), F32).at[0, :N_EXPERTS].set(b_router[l])
        b_route = b_route.at[0, N_EXPERTS:N_EXPERTS + N_GROUPS].set(b_group[l])
        x2, h3, route = _mix(oa.reshape(t, WIDTH_A), ob.reshape(t, WIDTH_B), om.reshape(t, WIDTH_M), x2,
                             out_gain[l][None, :], w_out[l].astype(BF16), norm_ffn[l][None, :],
                             jnp.stack([w_route_hi, w_route_lo]), b_route)

        wgu = jnp.concatenate([w_gate[l], w_up[l]], axis=-1).astype(BF16)
        moe = _moe(h3, route, wgu, w_down[l].astype(BF16), MOE_BLOCK_TOKENS)
    return _add(x2, moe).reshape(bsz, seq, d)
```

```python
import functools

import numpy as np
import jax
import jax.numpy as jnp
from jax import lax
from jax.experimental import pallas as pl
from jax.experimental.pallas import tpu as pltpu

F32 = jnp.float32
BF16 = jnp.bfloat16

D_MODEL = 1024
HEAD_DIM = 64
N_HEADS_A = 6
N_HEADS_B = 6
N_HEADS_M = 4
WIDTH_A = N_HEADS_A * HEAD_DIM
WIDTH_B = N_HEADS_B * HEAD_DIM
WIDTH_M = N_HEADS_M * HEAD_DIM
IN_WIDTH = 3 * WIDTH_A + 3 * WIDTH_B + WIDTH_M
DILATED_PATTERNS = ((128, 1), (512, 4), (2048, 16))
DIL_RADIUS = 64
GRID_W = 64
NA_ROWS = 8
NA_COLS = 16
N_GROUPS = 4
EXPERTS_PER_GROUP = 8
N_EXPERTS = N_GROUPS * EXPERTS_PER_GROUP
D_EXPERT = 512
EPS = 1e-6
ATTN_SCALE = HEAD_DIM ** -0.5
LOG2E = float(np.log2(np.e))
QSCALE = ATTN_SCALE * LOG2E

LANES = 128
SUBLANES = 8
ROW_TILES = D_MODEL // LANES
QBLK = 128
PROJ_ROWS = 512
MOE_TM = 128
MOE_BLOCK_TOKENS = 4096
VMEM_LIMIT = 56 * 1024 * 1024


def _cparams(sem):
    return pltpu.CompilerParams(dimension_semantics=sem, vmem_limit_bytes=VMEM_LIMIT)


def _from_token_tiles(t_ref):
    rows = t_ref.shape[0] // ROW_TILES
    return jnp.concatenate([t_ref[pl.ds(j, rows, stride=ROW_TILES), :] for j in range(ROW_TILES)], axis=1)


def _qk_slabs():
    a, b, m = WIDTH_A // LANES, WIDTH_B // LANES, WIDTH_M // LANES
    return tuple(range(0, 2 * a)) + tuple(range(3 * a, 3 * a + 2 * b)) + tuple(range(3 * (a + b), 3 * (a + b) + m))


def _proj_kernel(*refs, add):
    if add:
        x_ref, m_ref, g_ref, w_ref, cs_ref, xo_ref, o_ref = refs
        x = x_ref[...] + _from_token_tiles(m_ref)
        xo_ref[...] = x
    else:
        x_ref, g_ref, w_ref, cs_ref, o_ref = refs
        x = x_ref[...]
    ms = jnp.mean(x * x, axis=-1, keepdims=True)
    h = (x * lax.rsqrt(ms + EPS) * g_ref[...]).astype(BF16)
    acc = jnp.dot(h, w_ref[...], preferred_element_type=F32)
    lo = lax.broadcasted_iota(jnp.int32, (x.shape[0], LANES), 1) < HEAD_DIM
    qk = _qk_slabs()
    for s in range(IN_WIDTH // LANES):
        cols = slice(s * LANES, (s + 1) * LANES)
        y = acc[:, cols]
        if s in qk:
            sq = y * y
            ss_lo = jnp.sum(jnp.where(lo, sq, 0.0), axis=-1, keepdims=True)
            ss_hi = jnp.sum(jnp.where(lo, 0.0, sq), axis=-1, keepdims=True)
            inv = jnp.where(lo, lax.rsqrt(ss_lo * (1.0 / HEAD_DIM) + EPS), lax.rsqrt(ss_hi * (1.0 / HEAD_DIM) + EPS))
            y = y * inv * cs_ref[:, cols]
        o_ref[:, cols] = y.astype(BF16)


def _proj(x2, moe2, gain, w, col_scale):
    t = x2.shape[0]
    add = moe2 is not None
    row_spec = pl.BlockSpec((PROJ_ROWS, D_MODEL), lambda i: (i, 0))
    tile_spec = pl.BlockSpec((PROJ_ROWS * ROW_TILES, LANES), lambda i: (i, 0))
    in_specs = [row_spec] + ([tile_spec] if add else []) + [
        pl.BlockSpec((1, D_MODEL), lambda i: (0, 0)),
        pl.BlockSpec((D_MODEL, IN_WIDTH), lambda i: (0, 0)),
        pl.BlockSpec((1, IN_WIDTH), lambda i: (0, 0)),
    ]
    proj_spec = pl.BlockSpec((PROJ_ROWS, IN_WIDTH), lambda i: (i, 0))
    proj_shape = jax.ShapeDtypeStruct((t, IN_WIDTH), BF16)
    if add:
        out_shape = (jax.ShapeDtypeStruct((t, D_MODEL), F32), proj_shape)
        out_specs = (row_spec, proj_spec)
        args = (x2, moe2, gain, w, col_scale)
    else:
        out_shape, out_specs, args = proj_shape, proj_spec, (x2, gain, w, col_scale)
    return pl.pallas_call(
        functools.partial(_proj_kernel, add=add),
        out_shape=out_shape, grid=(t // PROJ_ROWS,), in_specs=in_specs, out_specs=out_specs,
        compiler_params=_cparams(("parallel",)), name="proj_add" if add else "proj",
    )(*args)


def _data_mask(lane, par):
    return (lane < HEAD_DIM) if par == 0 else (lane >= HEAD_DIM)


def _ones_lane(par):
    return HEAD_DIM if par == 0 else 0


def _head_norm(x, dm, gain):
    ss = jnp.sum(jnp.where(dm, x * x, 0.0), axis=-1, keepdims=True)
    return x * lax.rsqrt(ss * (1.0 / HEAD_DIM) + EPS) * gain


def _normalise(tot, lane, par):
    den = jnp.sum(jnp.where(lane == _ones_lane(par), tot, 0.0), axis=-1, keepdims=True)
    return tot / den


def _memkv_kernel(m_ref, g_ref, w_ref, kg_ref, k_ref, v_ref):
    x = m_ref[...]
    ms = jnp.mean(x * x, axis=-1, keepdims=True)
    h = (x * lax.rsqrt(ms + EPS) * g_ref[...]).astype(BF16)
    kv = jnp.dot(h, w_ref[...], preferred_element_type=F32)
    lane = lax.broadcasted_iota(jnp.int32, (x.shape[0], LANES), 1)
    for slab in range(WIDTH_M // LANES):
        kf = kv[:, slab * LANES:(slab + 1) * LANES]
        vf = kv[:, WIDTH_M + slab * LANES:WIDTH_M + (slab + 1) * LANES]
        for par in (0, 1):
            dm = _data_mask(lane, par)
            kn = _head_norm(kf, dm, kg_ref[...])
            k_ref[2 * slab + par] = jnp.where(dm, kn, 0.0).astype(BF16)
            v_ref[2 * slab + par] = jnp.where(dm, vf, jnp.where(lane == _ones_lane(par), 1.0, 0.0)).astype(BF16)


def _memkv(mem, gain, w, kgain):
    b, m, _ = mem.shape
    kv_shape = jax.ShapeDtypeStruct((b, N_HEADS_M, m, LANES), BF16)
    kv_spec = pl.BlockSpec((None, N_HEADS_M, m, LANES), lambda i: (i, 0, 0, 0))
    return pl.pallas_call(
        _memkv_kernel, out_shape=(kv_shape, kv_shape), grid=(b,),
        in_specs=[pl.BlockSpec((None, m, D_MODEL), lambda i: (i, 0, 0)),
                  pl.BlockSpec((1, D_MODEL), lambda i: (0, 0)),
                  pl.BlockSpec((D_MODEL, 2 * WIDTH_M), lambda i: (0, 0)),
                  pl.BlockSpec((1, LANES), lambda i: (0, 0))],
        out_specs=(kv_spec, kv_spec), compiler_params=_cparams(("parallel",)), name="memkv",
    )(mem, gain, w, kgain)


PREP_ROWS = 256
A_UNROLL = 8


def _attn_a_kernel(q_ref, k_ref, v_ref, b1_ref, b4_ref, b16_ref, o_ref,
                   qs, ks, vs, qm, km, vm, q4, q16, kd, vd, accd, accm, tot, s_scr):
    s_len = q_ref.shape[0]
    n_prep = s_len // PREP_ROWS
    lane_p = lax.broadcasted_iota(jnp.int32, (PREP_ROWS, LANES), 1)
    d_mid, d_far = DILATED_PATTERNS[1][1], DILATED_PATTERNS[2][1]
    len_mid, len_far = s_len // d_mid, s_len // d_far
    sub = d_far // d_mid

    def chunk_rows(c):
        return pl.ds(pl.multiple_of(c * PREP_ROWS, PREP_ROWS), PREP_ROWS)

    def mid_rows(r):
        return pl.ds(r, len_mid, stride=d_mid), pl.ds(pl.multiple_of(r * len_mid, QBLK), len_mid)

    def far_rows(c):
        return (pl.ds((c // sub) * len_mid + c % sub, len_far, stride=sub),
                pl.ds(pl.multiple_of(c * len_far, QBLK), len_far))

    def split_mid(pairs):
        def body(r, carry):
            tok, mid = mid_rows(r)
            for x_tok, x_mid, x_bf in pairs:
                x = x_tok[tok, :]
                x_mid[mid, :] = x
                x_bf[mid, :] = x.astype(BF16)
            return carry
        lax.fori_loop(0, d_mid, body, 0)

    def split_far(pairs):
        def body(c, carry):
            mid, far = far_rows(c)
            for x_mid, x_bf in pairs:
                x_bf[far, :] = x_mid[mid, :].astype(BF16)
            return carry
        lax.fori_loop(0, d_far, body, 0)

    def q_cast(c, carry):
        qs[chunk_rows(c), :] = q_ref[chunk_rows(c), :].astype(F32)
        return carry
    lax.fori_loop(0, n_prep, q_cast, 0)
    split_mid(((qs, qm, q4),))
    split_far(((qm, q16),))

    for par in (0, 1):
        def prep(c, carry, par=par):
            rows = chunk_rows(c)
            dm = _data_mask(lane_p, par)
            kf = jnp.where(dm, k_ref[rows, :].astype(F32), 0.0)
            vf = jnp.where(dm, v_ref[rows, :].astype(F32), jnp.where(lane_p == _ones_lane(par), 1.0, 0.0))
            ks[rows, :] = kf
            vs[rows, :] = vf
            kd[rows, :] = kf.astype(BF16)
            vd[rows, :] = vf.astype(BF16)
            return carry
        lax.fori_loop(0, n_prep, prep, 0)

        def run_blocks(q_src, length, b_ref, out, par=par):
            nblk = length // QBLK

            def blocks(o, carry):
                units = []
                for i in range(A_UNROLL):
                    u = o * A_UNROLL + i
                    if nblk == 1:
                        base = pl.multiple_of(u * length, QBLK)
                        qrows = pl.ds(base, QBLK)
                        keys = pl.ds(base, QBLK)
                        bias = b_ref[par]
                        kw = QBLK
                    else:
                        r = u // nblk
                        blk = u % nblk
                        q0 = blk * QBLK
                        ws = jnp.clip(q0 - DIL_RADIUS, 0, length - 2 * QBLK)
                        tid = jnp.where(blk == 0, 0, jnp.where(blk == nblk - 1, 2, 1))
                        qrows = pl.ds(pl.multiple_of(r * length + q0, QBLK), QBLK)
                        keys = pl.ds(pl.multiple_of(r * length + ws, DIL_RADIUS), 2 * QBLK)
                        bias = b_ref[par, tid]
                        kw = 2 * QBLK
                    s_scr[i, :, 0:kw] = lax.dot_general(q_src[qrows, :], kd[keys, :], (((1,), (1,)), ((), ())),
                                                        preferred_element_type=F32)
                    units.append((qrows, keys, bias, kw))
                for i, (qrows, keys, bias, kw) in enumerate(units):
                    p = jnp.exp2(s_scr[i, :, 0:kw] + bias).astype(BF16)
                    out[qrows, :] = jnp.dot(p, vd[keys, :], preferred_element_type=F32)
                return carry
            lax.fori_loop(0, s_len // QBLK // A_UNROLL, blocks, 0)

        run_blocks(q_ref, s_len, b1_ref, accd)

        def first(c, carry, par=par):
            tot[par, chunk_rows(c), :] = accd[chunk_rows(c), :]
            return carry
        lax.fori_loop(0, n_prep, first, 0)

        split_mid(((ks, km, kd), (vs, vm, vd)))
        run_blocks(q4, len_mid, b4_ref, accm)
        split_far(((km, kd), (vm, vd)))
        run_blocks(q16, len_far, b16_ref, accd)

        def merge_far(c, carry):
            mid, far = far_rows(c)
            accm[mid, :] = accm[mid, :] + accd[far, :]
            return carry
        lax.fori_loop(0, d_far, merge_far, 0)

        def merge_mid(r, carry, par=par):
            tok, mid = mid_rows(r)
            tot[par, tok, :] = tot[par, tok, :] + accm[mid, :]
            return carry
        lax.fori_loop(0, d_mid, merge_mid, 0)

    def fin(c, carry):
        rows = chunk_rows(c)
        o0 = _normalise(tot[0, rows, :], lane_p, 0)
        o1 = _normalise(tot[1, rows, :], lane_p, 1)
        o_ref[rows, :] = jnp.where(lane_p < HEAD_DIM, o0, o1).astype(BF16)
        return carry
    lax.fori_loop(0, n_prep, fin, 0)


def _attn_a(proj3, b1, b4, b16):
    b, s, _ = proj3.shape
    n_pairs = WIDTH_A // LANES

    def slab(off):
        return pl.BlockSpec((None, s, LANES), lambda i, p: (i, 0, off + p))
    return pl.pallas_call(
        _attn_a_kernel, out_shape=jax.ShapeDtypeStruct((b, s, WIDTH_A), BF16), grid=(b, n_pairs),
        in_specs=[slab(0), slab(n_pairs), slab(2 * n_pairs),
                  pl.BlockSpec((2, 3, QBLK, 2 * QBLK), lambda i, p: (p, 0, 0, 0)),
                  pl.BlockSpec((2, 3, QBLK, 2 * QBLK), lambda i, p: (p, 0, 0, 0)),
                  pl.BlockSpec((2, QBLK, QBLK), lambda i, p: (p, 0, 0))],
        out_specs=pl.BlockSpec((None, s, LANES), lambda i, p: (i, 0, p)),
        scratch_shapes=[pltpu.VMEM((s, LANES), F32)] * 6 + [pltpu.VMEM((s, LANES), BF16)] * 4
        + [pltpu.VMEM((s, LANES), F32)] * 2
        + [pltpu.VMEM((2, s, LANES), F32), pltpu.VMEM((A_UNROLL, QBLK, 2 * QBLK), F32)],
        compiler_params=_cparams(("parallel", "parallel")), name="attn_a",
    )(proj3, proj3, proj3, b1, b4, b16)


def _attn_b_kernel(q_ref, k_ref, v_ref, bias_ref, o_ref, kb, vb, s_scr):
    s_len = q_ref.shape[0]
    n_prep = s_len // PREP_ROWS
    rows_total = s_len // GRID_W
    kh = min(NA_ROWS, rows_total)
    lane_p = lax.broadcasted_iota(jnp.int32, (PREP_ROWS, LANES), 1)
    lane_q = lax.broadcasted_iota(jnp.int32, (GRID_W, LANES), 1)

    def prep(c, carry):
        rows = pl.ds(pl.multiple_of(c * PREP_ROWS, PREP_ROWS), PREP_ROWS)
        kf = k_ref[rows, :].astype(F32)
        vf = v_ref[rows, :].astype(F32)
        for par in (0, 1):
            dm = _data_mask(lane_p, par)
            kb[par, rows, :] = jnp.where(dm, kf, 0.0).astype(BF16)
            vb[par, rows, :] = jnp.where(dm, vf, jnp.where(lane_p == _ones_lane(par), 1.0, 0.0)).astype(BF16)
        return carry
    lax.fori_loop(0, n_prep, prep, 0)

    unroll = 4

    def grid_rows(o, carry):
        units = []
        for i in range(unroll):
            r = o * unroll + i
            r0 = jnp.clip(r - kh // 2, 0, rows_total - kh)
            typ = jnp.where(r < kh // 2, r, jnp.where(r > rows_total - kh // 2, r - (rows_total - kh), kh // 2))
            qrows = pl.ds(pl.multiple_of(r * GRID_W, GRID_W), GRID_W)
            keys = pl.ds(pl.multiple_of(r0 * GRID_W, GRID_W), kh * GRID_W)
            for par in (0, 1):
                s_scr[2 * i + par] = lax.dot_general(q_ref[qrows, :], kb[par, keys, :], (((1,), (1,)), ((), ())),
                                                     preferred_element_type=F32)
            units.append((qrows, keys, typ))
        for i, (qrows, keys, typ) in enumerate(units):
            outs = []
            for par in (0, 1):
                p = jnp.exp2(s_scr[2 * i + par] + bias_ref[par, typ]).astype(BF16)
                acc = jnp.dot(p, vb[par, keys, :], preferred_element_type=F32)
                outs.append(_normalise(acc, lane_q, par))
            o_ref[qrows, :] = jnp.where(lane_q < HEAD_DIM, outs[0], outs[1]).astype(BF16)
        return carry
    lax.fori_loop(0, rows_total // unroll, grid_rows, 0)


def _attn_b(proj3, bias):
    b, s, _ = proj3.shape
    n_pairs = WIDTH_B // LANES
    first = 3 * WIDTH_A // LANES
    n_types, kw = bias.shape[1], bias.shape[3]

    def slab(off):
        return pl.BlockSpec((None, s, LANES), lambda i, p: (i, 0, first + off + p))
    return pl.pallas_call(
        _attn_b_kernel, out_shape=jax.ShapeDtypeStruct((b, s, WIDTH_B), BF16), grid=(b, n_pairs),
        in_specs=[slab(0), slab(n_pairs), slab(2 * n_pairs),
                  pl.BlockSpec((2, n_types, GRID_W, kw), lambda i, p: (p, 0, 0, 0))],
        out_specs=pl.BlockSpec((None, s, LANES), lambda i, p: (i, 0, p)),
        scratch_shapes=[pltpu.VMEM((2, s, LANES), BF16)] * 2 + [pltpu.VMEM((8, GRID_W, kw), F32)],
        compiler_params=_cparams(("parallel", "parallel")), name="attn_b",
    )(proj3, proj3, proj3, bias)


MEM_QROWS = 256
MEM_UNROLL = 2


def _attn_m_kernel(q_ref, k_ref, v_ref, shift_ref, o_ref, s_scr):
    s_len = q_ref.shape[0]
    lane = lax.broadcasted_iota(jnp.int32, (MEM_QROWS, LANES), 1)

    def chunks(o, carry):
        all_rows = []
        for i in range(MEM_UNROLL):
            rows = pl.ds(pl.multiple_of((o * MEM_UNROLL + i) * MEM_QROWS, MEM_QROWS), MEM_QROWS)
            for par in (0, 1):
                s_scr[2 * i + par] = lax.dot_general(q_ref[rows, :], k_ref[par], (((1,), (1,)), ((), ())),
                                                     preferred_element_type=F32)
            all_rows.append(rows)
        for i, rows in enumerate(all_rows):
            outs = []
            for par in (0, 1):
                p = jnp.exp2(s_scr[2 * i + par] - shift_ref[...]).astype(BF16)
                acc = jnp.dot(p, v_ref[par], preferred_element_type=F32)
                outs.append(_normalise(acc, lane, par))
            o_ref[rows, :] = jnp.where(lane < HEAD_DIM, outs[0], outs[1]).astype(BF16)
        return carry
    lax.fori_loop(0, s_len // MEM_QROWS // MEM_UNROLL, chunks, 0)


def _attn_m(proj3, kaug, vaug, shift):
    b, s, _ = proj3.shape
    m = kaug.shape[2]
    n_pairs = WIDTH_M // LANES
    first = 3 * (WIDTH_A + WIDTH_B) // LANES
    kv_spec = pl.BlockSpec((None, 2, m, LANES), lambda i, p: (i, p, 0, 0))
    return pl.pallas_call(
        _attn_m_kernel, out_shape=jax.ShapeDtypeStruct((b, s, WIDTH_M), BF16), grid=(b, n_pairs),
        in_specs=[pl.BlockSpec((None, s, LANES), lambda i, p: (i, 0, first + p)), kv_spec, kv_spec,
                  pl.BlockSpec((1, m), lambda i, p: (0, 0))],
        out_specs=pl.BlockSpec((None, s, LANES), lambda i, p: (i, 0, p)),
        scratch_shapes=[pltpu.VMEM((2 * MEM_UNROLL, MEM_QROWS, m), F32)],
        compiler_params=_cparams(("parallel", "parallel")), name="attn_m",
    )(proj3, kaug, vaug, shift)


def _mix_kernel(oa_ref, ob_ref, om_ref, x_ref, og_ref, wo_ref, gf_ref, wr_ref, br_ref,
                x1_ref, h3_ref, route_ref):
    rows = x_ref.shape[0]
    y = None
    off = 0
    for o_ref in (oa_ref, ob_ref, om_ref):
        width = o_ref.shape[1]
        o = o_ref[...].astype(F32)
        ms = jnp.mean(o * o, axis=-1, keepdims=True)
        mixed = (o * lax.rsqrt(ms + EPS) * og_ref[:, off:off + width]).astype(BF16)
        part = jnp.dot(mixed, wo_ref[off:off + width, :], preferred_element_type=F32)
        y = part if y is None else y + part
        off += width
    x1 = x_ref[...] + y
    x1_ref[...] = x1
    ms = jnp.mean(x1 * x1, axis=-1, keepdims=True)
    h = x1 * lax.rsqrt(ms + EPS) * gf_ref[...]
    for j in range(ROW_TILES):
        h3_ref[pl.ds(j, rows, stride=ROW_TILES), :] = h[:, j * LANES:(j + 1) * LANES]

    h_hi = h.astype(BF16)
    h_lo = (h - h_hi.astype(F32)).astype(BF16)
    logits = (jnp.dot(h_hi, wr_ref[0], preferred_element_type=F32)
              + jnp.dot(h_lo, wr_ref[0], preferred_element_type=F32)
              + jnp.dot(h_hi, wr_ref[1], preferred_element_type=F32)) + br_ref[...]
    lane = lax.broadcasted_iota(jnp.int32, (rows, LANES), 1)
    lane_f = lane.astype(F32)
    neg = -jnp.inf
    far = float(LANES)
    is_g = (lane >> 2) == (N_EXPERTS // N_GROUPS)
    gl = jnp.where(is_g, logits, neg)
    gmax = jnp.max(gl, axis=-1, keepdims=True)
    gidx = jnp.min(jnp.where(gl == gmax, lane_f, far), axis=-1, keepdims=True) - float(N_EXPERTS)
    gsum = jnp.sum(jnp.where(is_g, jnp.exp(gl - gmax), 0.0), axis=-1, keepdims=True)
    g_top = 1.0 / gsum
    in_grp = (lane >> 3).astype(F32) == gidx
    el = jnp.where(in_grp, logits, neg)
    v1 = jnp.max(el, axis=-1, keepdims=True)
    i1 = jnp.min(jnp.where(el == v1, lane_f, far), axis=-1, keepdims=True)
    el2 = jnp.where(lane_f == i1, neg, el)
    v2 = jnp.max(el2, axis=-1, keepdims=True)
    i2 = jnp.min(jnp.where(el2 == v2, lane_f, far), axis=-1, keepdims=True)
    ev = jnp.exp(v2 - v1)
    w1 = g_top / (1.0 + ev)
    w2 = g_top * ev / (1.0 + ev)
    route_ref[...] = jnp.where(lane == 0, i1, jnp.where(lane == 1, i2, jnp.where(lane == 2, w1,
                               jnp.where(lane == 3, w2, 0.0))))


def _mix(oa, ob, om, x2, out_gain, w_out, gain_ffn, w_route, b_route):
    t = x2.shape[0]

    def rows(width):
        return pl.BlockSpec((PROJ_ROWS, width), lambda i: (i, 0))

    def whole(*shape):
        return pl.BlockSpec(shape, lambda i: (0,) * len(shape))
    return pl.pallas_call(
        _mix_kernel,
        out_shape=(jax.ShapeDtypeStruct((t, D_MODEL), F32),
                   jax.ShapeDtypeStruct((t * ROW_TILES, LANES), F32),
                   jax.ShapeDtypeStruct((t, LANES), F32)),
        grid=(t // PROJ_ROWS,),
        in_specs=[rows(WIDTH_A), rows(WIDTH_B), rows(WIDTH_M), rows(D_MODEL),
                  whole(1, D_MODEL), whole(D_MODEL, D_MODEL), whole(1, D_MODEL),
                  whole(2, D_MODEL, LANES), whole(1, LANES)],
        out_specs=(rows(D_MODEL), pl.BlockSpec((PROJ_ROWS * ROW_TILES, LANES), lambda i: (i, 0)), rows(LANES)),
        compiler_params=_cparams(("parallel",)), name="mix",
    )(oa, ob, om, x2, out_gain, w_out, gain_ffn, w_route, b_route)


SCATTER_UNROLL = 8
W_SLOTS = 4
W_CHUNK_ROWS = 256


def _moe_weight_copies(wgu_hbm, wd_hbm, wgu_buf, wd_buf, sems, e, slot):
    copies = []
    for hbm, buf in ((wgu_hbm, wgu_buf), (wd_hbm, wd_buf)):
        for c in range(buf.shape[1] // W_CHUNK_ROWS):
            rows = pl.ds(c * W_CHUNK_ROWS, W_CHUNK_ROWS)
            copies.append(pltpu.make_async_copy(hbm.at[e, rows], buf.at[slot, rows], sems.at[slot]))
    return copies


def _moe_kernel(rstart_ref, tslot_ref, tfirst_ref, tnext_ref, npair_ref, act_ref,
                src_ref, dst_ref, gate_ref, h3_ref, wgu_hbm, wd_hbm, o_hbm,
                wgu_buf, wd_buf, sems, out_sem, xt0, xt1, yt0, yt1, acc):
    b = pl.program_id(0)
    n_rows = h3_ref.shape[0]
    tiles_per_block = 2 * (n_rows // ROW_TILES) // MOE_TM + N_EXPERTS
    copies = functools.partial(_moe_weight_copies, wgu_hbm, wd_hbm, wgu_buf, wd_buf, sems)

    for k in range(W_SLOTS - 1):
        e0 = act_ref[b * (W_SLOTS - 1) + k]

        @pl.when(e0 >= 0)
        def _(e0=e0, k=k):
            for c in copies(e0, k):
                c.start()
    acc[...] = jnp.zeros_like(acc)

    def pair(p, carry):
        tiles = []
        for half, (xt, yt) in enumerate(((xt0, yt0), (xt1, yt1))):
            idx = b * tiles_per_block + 2 * p + half
            slot = tslot_ref[idx]

            @pl.when(tfirst_ref[idx] == 1)
            def _(slot=slot):
                for c in copies(0, slot):
                    c.wait()
            tiles.append((rstart_ref[idx], slot, tnext_ref[idx], xt, yt))

        for rs, _, _, xt, _ in tiles:
            for i in range(MOE_TM):
                off = pl.multiple_of(src_ref[0, rs + i], ROW_TILES)
                xt[i * ROW_TILES:(i + 1) * ROW_TILES, :] = h3_ref[pl.ds(off, ROW_TILES), :]

        for _, slot, _, xt, yt in tiles:
            lhs = jnp.concatenate([xt[pl.ds(k, MOE_TM, stride=ROW_TILES), :] for k in range(ROW_TILES)], axis=1)
            gu = jnp.dot(lhs.astype(BF16), wgu_buf[slot], preferred_element_type=F32)
            g = gu[:, :D_EXPERT]
            hid = (g * (1.0 / (1.0 + jnp.exp(-g)))) * gu[:, D_EXPERT:]
            y = jnp.dot(hid.astype(BF16), wd_buf[slot], preferred_element_type=F32)
            for k in range(ROW_TILES):
                yt[pl.ds(k, MOE_TM, stride=ROW_TILES), :] = y[:, k * LANES:(k + 1) * LANES]

        for rs, _, _, _, yt in tiles:
            for i0 in range(0, MOE_TM, SCATTER_UNROLL):
                vals = []
                for i in range(i0, i0 + SCATTER_UNROLL):
                    rows = pl.ds(pl.multiple_of(dst_ref[0, rs + i], ROW_TILES), ROW_TILES)
                    vals.append((rows, acc[rows, :] + gate_ref[0, rs + i] * yt[i * ROW_TILES:(i + 1) * ROW_TILES, :]))
                for rows, val in vals:
                    acc[rows, :] = val

        for _, slot, nxt, _, _ in tiles:
            @pl.when(nxt >= 0)
            def _(slot=slot, nxt=nxt):
                for c in copies(nxt, (slot + W_SLOTS - 1) % W_SLOTS):
                    c.start()
        return carry
    lax.fori_loop(0, npair_ref[b], pair, 0)

    out_copy = pltpu.make_async_copy(acc.at[pl.ds(0, n_rows)],
                                     o_hbm.at[pl.ds(pl.multiple_of(b * n_rows, ROW_TILES), n_rows)], out_sem)
    out_copy.start()
    out_copy.wait()


def _moe(h3, route, wgu, wd, block_tokens):
    t = route.shape[0]
    nblk = t // block_tokens
    n_assign = 2 * block_tokens
    nt_max = n_assign // MOE_TM + N_EXPERTS

    n_pad = MOE_TM - 1
    stride = 2 * n_assign
    n_list = n_assign + N_EXPERTS * MOE_TM
    e = route[:, 0:2].astype(jnp.int32).reshape(nblk, n_assign)
    w = route[:, 2:4].reshape(nblk, n_assign)
    experts = jnp.arange(N_EXPERTS, dtype=jnp.int32)
    key = e * stride + jnp.arange(n_assign, dtype=jnp.int32)[None, :]
    pad_key = (experts[:, None] * stride + n_assign + jnp.arange(n_pad, dtype=jnp.int32)[None, :]).reshape(1, -1)
    tail_key = (N_EXPERTS * stride + n_assign + experts)[None, :]
    extra_key = jnp.broadcast_to(jnp.concatenate([pad_key, tail_key], axis=1), (nblk, n_list - n_assign))
    key_s, w_s = lax.sort((jnp.concatenate([key, extra_key], axis=1),
                           jnp.concatenate([w, jnp.zeros((nblk, n_list - n_assign), F32)], axis=1)),
                          dimension=1, num_keys=1)
    idx_s = key_s & (stride - 1)
    real = idx_s < n_assign
    row_off = (idx_s >> 1) * ROW_TILES
    src_s = jnp.where(real, row_off, 0)
    dst_s = jnp.where(real, row_off, block_tokens * ROW_TILES)
    cnt = jnp.sum((e[:, :, None] == experts[None, None, :]).astype(jnp.int32), axis=1)
    cstart = jnp.cumsum(cnt, axis=1) - cnt + experts[None, :] * n_pad
    ntile_e = (cnt + MOE_TM - 1) // MOE_TM
    tend = jnp.cumsum(ntile_e, axis=1)
    ntiles = tend[:, -1:]
    active = (ntile_e > 0).astype(jnp.int32)
    order_e = jnp.cumsum(active, axis=1) - active
    ranks = jnp.arange(N_EXPERTS + W_SLOTS, dtype=jnp.int32)
    hit = (order_e[:, None, :] == ranks[None, :, None]) & (active[:, None, :] == 1)
    act = jnp.sum(jnp.where(hit, experts[None, None, :] + 1, 0), axis=2) - 1
    tile_ids = jnp.arange(nt_max, dtype=jnp.int32)[None, :]
    live = tile_ids < ntiles
    texp = jnp.minimum(jnp.sum((tile_ids[:, :, None] >= tend[:, None, :]).astype(jnp.int32), axis=2), N_EXPERTS - 1)
    texp = jnp.where(live, texp, jnp.take_along_axis(texp, jnp.maximum(ntiles - 1, 0), axis=1))
    within = tile_ids - jnp.take_along_axis(tend - ntile_e, texp, axis=1)
    rstart = jnp.where(live, jnp.take_along_axis(cstart, texp, axis=1) + within * MOE_TM, n_list - MOE_TM)
    torder = jnp.take_along_axis(order_e, texp, axis=1)
    tfirst = (live & (within == 0)).astype(jnp.int32)
    tnext = jnp.where(tfirst == 1, jnp.take_along_axis(act, torder + (W_SLOTS - 1), axis=1), -1)
    npairs = (ntiles[:, 0] + 1) // 2

    def flat(x):
        return x.reshape(-1).astype(jnp.int32)

    def smem_list():
        return pl.BlockSpec((None, 1, n_list), lambda b, *_: (b, 0, 0), memory_space=pltpu.SMEM)
    tile_buf = pltpu.VMEM((MOE_TM * ROW_TILES, LANES), F32)
    grid_spec = pltpu.PrefetchScalarGridSpec(
        num_scalar_prefetch=6, grid=(nblk,),
        in_specs=[smem_list(), smem_list(), smem_list(),
                  pl.BlockSpec((block_tokens * ROW_TILES, LANES), lambda b, *_: (b, 0),
                               pipeline_mode=pl.Buffered(1)),
                  pl.BlockSpec(memory_space=pl.ANY), pl.BlockSpec(memory_space=pl.ANY)],
        out_specs=pl.BlockSpec(memory_space=pl.ANY),
        scratch_shapes=[pltpu.VMEM((W_SLOTS, D_MODEL, 2 * D_EXPERT), BF16),
                        pltpu.VMEM((W_SLOTS, D_EXPERT, D_MODEL), BF16),
                        pltpu.SemaphoreType.DMA((W_SLOTS,)), pltpu.SemaphoreType.DMA(()),
                        tile_buf, tile_buf, tile_buf, tile_buf,
                        pltpu.VMEM(((block_tokens + 1) * ROW_TILES, LANES), F32)])
    return pl.pallas_call(
        _moe_kernel, out_shape=jax.ShapeDtypeStruct((t * ROW_TILES, LANES), F32), grid_spec=grid_spec,
        compiler_params=_cparams(("arbitrary",)), name="moe",
    )(flat(rstart), flat(torder % W_SLOTS), flat(tfirst), flat(tnext), flat(npairs), flat(act[:, :W_SLOTS - 1]),
      src_s.reshape(nblk, 1, n_list), dst_s.reshape(nblk, 1, n_list), w_s.reshape(nblk, 1, n_list), h3, wgu, wd)


def _add_kernel(a_ref, b_ref, o_ref):
    o_ref[...] = a_ref[...] + _from_token_tiles(b_ref)


def _add(a, b_tiles):
    t = a.shape[0]
    spec = pl.BlockSpec((PROJ_ROWS, D_MODEL), lambda i: (i, 0))
    tile_spec = pl.BlockSpec((PROJ_ROWS * ROW_TILES, LANES), lambda i: (i, 0))
    return pl.pallas_call(_add_kernel, out_shape=jax.ShapeDtypeStruct(a.shape, a.dtype), grid=(t // PROJ_ROWS,),
                          in_specs=[spec, tile_spec], out_specs=spec, compiler_params=_cparams(("parallel",)),
                          name="residual_add")(a, b_tiles)


def _score_bound(gq, gk):
    return (HEAD_DIM * ATTN_SCALE * LOG2E) * jnp.max(jnp.abs(gq)) * jnp.max(jnp.abs(gk))


def _alibi_tables(shift):
    slopes = (2.0 ** (-8.0 * np.arange(1, N_HEADS_A + 1) / N_HEADS_A)).astype(np.float32)
    tables = []
    for _, dil in DILATED_PATTERNS:
        if dil == DILATED_PATTERNS[-1][1]:
            offs, kw = (0,), QBLK
        else:
            offs, kw = (0, DIL_RADIUS, 2 * DIL_RADIUS), 2 * QBLK
        tiles = []
        for off in offs:
            rel = np.abs((off + np.arange(QBLK))[:, None] - np.arange(kw)[None, :])
            tiles.append(np.where(rel <= DIL_RADIUS, rel * dil, np.inf).astype(np.float32))
        dist = np.stack(tiles)
        tab = -(slopes * LOG2E)[:, None, None, None] * dist[None]
        tab = jnp.asarray(tab) - shift
        tables.append(tab[:, 0] if len(offs) == 1 else tab)
    return tables


def _rpb_table(rpb, shift, rows_total):
    kh = min(NA_ROWS, rows_total)
    c = np.arange(GRID_W)
    c0 = np.clip(c - NA_COLS // 2, 0, GRID_W - NA_COLS)
    col_ok = (c[None, :] >= c0[:, None]) & (c[None, :] < c0[:, None] + NA_COLS)
    dc = np.clip(c[None, :] - c[:, None], -(NA_COLS - 1), NA_COLS - 1) + NA_COLS - 1
    type_rows = list(range(kh // 2)) + [kh // 2] + list(range(rows_total - kh // 2 + 1, rows_total))
    dr = np.array([[int(np.clip(r - kh // 2, 0, rows_total - kh)) + j - r + NA_ROWS - 1 for j in range(kh)]
                   for r in type_rows])
    n_dc = 2 * NA_COLS - 1
    pick = dc[None, None, :, :, None] == np.arange(n_dc)[None, None, None, None, :]
    cols = jnp.sum(jnp.where(pick, rpb[:, :, None, None, :], 0.0), axis=-1)
    tab = jnp.stack([cols[:, int(d)] for d in dr.reshape(-1)], axis=1).reshape(
        rpb.shape[0], len(type_rows), kh, GRID_W, GRID_W)
    tab = jnp.where(col_ok[None, None, None], tab, -jnp.inf)
    tab = tab.transpose(0, 1, 3, 2, 4).reshape(rpb.shape[0], len(type_rows), GRID_W, kh * GRID_W) * LOG2E
    return tab - shift[:, None, None, None]


def kernel(x, mem, norm_mix, w_in, qk_gain, rpb, norm_mem, w_mem_kv, out_gain, w_out, norm_ffn, w_group,
           b_group, w_router, b_router, w_gate, w_up, w_down):
    bsz, seq, d = x.shape
    t = bsz * seq
    depth = w_in.shape[0]
    rows_total = seq // GRID_W

    x2 = x.reshape(t, d)
    moe = None
    for l in range(depth):
        gains = jnp.tile(qk_gain[l], (1, 2))
        col_scale = jnp.concatenate([
            jnp.tile(gains[0] * QSCALE, WIDTH_A // LANES), jnp.tile(gains[1], WIDTH_A // LANES), jnp.ones((WIDTH_A,), F32),
            jnp.tile(gains[2] * QSCALE, WIDTH_B // LANES), jnp.tile(gains[3], WIDTH_B // LANES), jnp.ones((WIDTH_B,), F32),
            jnp.tile(gains[4] * QSCALE, WIDTH_M // LANES)])[None, :]
        w_in_l = w_in[l].astype(BF16)
        if moe is None:
            proj = _proj(x2, None, norm_mix[l][None, :], w_in_l, col_scale)
        else:
            x2, proj = _proj(x2, moe, norm_mix[l][None, :], w_in_l, col_scale)
        proj3 = proj.reshape(bsz, seq, IN_WIDTH)

        shift_a = _score_bound(qk_gain[l, 0], qk_gain[l, 1])
        b1, b4, b16 = _alibi_tables(shift_a)
        oa = _attn_a(proj3, b1, b4, b16)

        shift_b = _score_bound(qk_gain[l, 2], qk_gain[l, 3]) + LOG2E * jnp.max(rpb[l], axis=(1, 2))
        ob = _attn_b(proj3, _rpb_table(rpb[l], shift_b, rows_total))

        kaug, vaug = _memkv(mem, norm_mem[l][None, :], w_mem_kv[l].astype(BF16), gains[5:6])
        shift_m = jnp.full((1, mem.shape[1]), _score_bound(qk_gain[l, 4], qk_gain[l, 5]), F32)
        om = _attn_m(proj3, kaug, vaug, shift_m)

        w_route = jnp.zeros((d, LANES), F32).at[:, :N_EXPERTS].set(w_router[l])
        w_route = w_route.at[:, N_EXPERTS:N_EXPERTS + N_GROUPS].set(w_group[l])
        w_route_hi = w_route.astype(BF16)
        w_route_lo = (w_route - w_route_hi.astype(F32)).astype(BF16)
        b_route = jnp.zeros((1, LANES), F32).at[0, :N_EXPERTS].set(b_router[l])
        b_route = b_route.at[0, N_EXPERTS:N_EXPERTS + N_GROUPS].set(b_group[l])
        x2, h3, route = _mix(oa.reshape(t, WIDTH_A), ob.reshape(t, WIDTH_B), om.reshape(t, WIDTH_M), x2,
                             out_gain[l][None, :], w_out[l].astype(BF16), norm_ffn[l][None, :],
                             jnp.stack([w_route_hi, w_route_lo]), b_route)

        wgu = jnp.concatenate([w_gate[l], w_up[l]], axis=-1).astype(BF16)
        moe = _moe(h3, route, wgu, w_down[l].astype(BF16), MOE_BLOCK_TOKENS)
    return _add(x2, moe).reshape(bsz, seq, d)
```

```python
import functools

import numpy as np
import jax
import jax.numpy as jnp
from jax import lax
from jax.experimental import pallas as pl
from jax.experimental.pallas import tpu as pltpu

F32 = jnp.float32
BF16 = jnp.bfloat16

D_MODEL = 1024
HEAD_DIM = 64
N_HEADS_A = 6
N_HEADS_B = 6
N_HEADS_M = 4
WIDTH_A = N_HEADS_A * HEAD_DIM
WIDTH_B = N_HEADS_B * HEAD_DIM
WIDTH_M = N_HEADS_M * HEAD_DIM
IN_WIDTH = 3 * WIDTH_A + 3 * WIDTH_B + WIDTH_M
DILATED_PATTERNS = ((128, 1), (512, 4), (2048, 16))
DIL_RADIUS = 64
GRID_W = 64
NA_ROWS = 8
NA_COLS = 16
N_GROUPS = 4
EXPERTS_PER_GROUP = 8
N_EXPERTS = N_GROUPS * EXPERTS_PER_GROUP
D_EXPERT = 512
EPS = 1e-6
ATTN_SCALE = HEAD_DIM ** -0.5
LOG2E = float(np.log2(np.e))
QSCALE = ATTN_SCALE * LOG2E

LANES = 128
SUBLANES = 8
ROW_TILES = D_MODEL // LANES
QBLK = 128
PROJ_ROWS = 512
ROUTE_ROWS = 8
MOE_TM = 128
MOE_BLOCK_TOKENS = 4096
VMEM_LIMIT = 56 * 1024 * 1024


def _cparams(sem):
    return pltpu.CompilerParams(dimension_semantics=sem, vmem_limit_bytes=VMEM_LIMIT)


def _from_token_tiles(t_ref):
    rows = t_ref.shape[0] // ROW_TILES
    return jnp.concatenate([t_ref[pl.ds(j, rows, stride=ROW_TILES), :] for j in range(ROW_TILES)], axis=1)


def _qk_slabs():
    a, b, m = WIDTH_A // LANES, WIDTH_B // LANES, WIDTH_M // LANES
    return tuple(range(0, 2 * a)) + tuple(range(3 * a, 3 * a + 2 * b)) + tuple(range(3 * (a + b), 3 * (a + b) + m))


def _proj_kernel(*refs, add):
    if add:
        x_ref, m_ref, g_ref, w_ref, cs_ref, xo_ref, o_ref = refs
        x = x_ref[...] + _from_token_tiles(m_ref)
        xo_ref[...] = x
    else:
        x_ref, g_ref, w_ref, cs_ref, o_ref = refs
        x = x_ref[...]
    ms = jnp.mean(x * x, axis=-1, keepdims=True)
    h = (x * lax.rsqrt(ms + EPS) * g_ref[...]).astype(BF16)
    acc = jnp.dot(h, w_ref[...], preferred_element_type=F32)
    lo = lax.broadcasted_iota(jnp.int32, (x.shape[0], LANES), 1) < HEAD_DIM
    qk = _qk_slabs()
    for s in range(IN_WIDTH // LANES):
        cols = slice(s * LANES, (s + 1) * LANES)
        y = acc[:, cols]
        if s in qk:
            sq = y * y
            ss_lo = jnp.sum(jnp.where(lo, sq, 0.0), axis=-1, keepdims=True)
            ss_hi = jnp.sum(jnp.where(lo, 0.0, sq), axis=-1, keepdims=True)
            inv = jnp.where(lo, lax.rsqrt(ss_lo * (1.0 / HEAD_DIM) + EPS), lax.rsqrt(ss_hi * (1.0 / HEAD_DIM) + EPS))
            y = y * inv * cs_ref[:, cols]
        o_ref[:, cols] = y.astype(BF16)


def _proj(x2, moe2, gain, w, col_scale):
    t = x2.shape[0]
    add = moe2 is not None
    row_spec = pl.BlockSpec((PROJ_ROWS, D_MODEL), lambda i: (i, 0))
    tile_spec = pl.BlockSpec((PROJ_ROWS * ROW_TILES, LANES), lambda i: (i, 0))
    in_specs = [row_spec] + ([tile_spec] if add else []) + [
        pl.BlockSpec((1, D_MODEL), lambda i: (0, 0)),
        pl.BlockSpec((D_MODEL, IN_WIDTH), lambda i: (0, 0)),
        pl.BlockSpec((1, IN_WIDTH), lambda i: (0, 0)),
    ]
    proj_spec = pl.BlockSpec((PROJ_ROWS, IN_WIDTH), lambda i: (i, 0))
    proj_shape = jax.ShapeDtypeStruct((t, IN_WIDTH), BF16)
    if add:
        out_shape = (jax.ShapeDtypeStruct((t, D_MODEL), F32), proj_shape)
        out_specs = (row_spec, proj_spec)
        args = (x2, moe2, gain, w, col_scale)
    else:
        out_shape, out_specs, args = proj_shape, proj_spec, (x2, gain, w, col_scale)
    return pl.pallas_call(
        functools.partial(_proj_kernel, add=add),
        out_shape=out_shape, grid=(t // PROJ_ROWS,), in_specs=in_specs, out_specs=out_specs,
        compiler_params=_cparams(("parallel",)), name="proj_add" if add else "proj",
    )(*args)


def _data_mask(lane, par):
    return (lane < HEAD_DIM) if par == 0 else (lane >= HEAD_DIM)


def _ones_lane(par):
    return HEAD_DIM if par == 0 else 0


def _head_norm(x, dm, gain):
    ss = jnp.sum(jnp.where(dm, x * x, 0.0), axis=-1, keepdims=True)
    return x * lax.rsqrt(ss * (1.0 / HEAD_DIM) + EPS) * gain


def _normalise(tot, lane, par):
    den = jnp.sum(jnp.where(lane == _ones_lane(par), tot, 0.0), axis=-1, keepdims=True)
    return tot / den


def _memkv_kernel(m_ref, g_ref, w_ref, kg_ref, k_ref, v_ref):
    x = m_ref[...]
    ms = jnp.mean(x * x, axis=-1, keepdims=True)
    h = (x * lax.rsqrt(ms + EPS) * g_ref[...]).astype(BF16)
    kv = jnp.dot(h, w_ref[...], preferred_element_type=F32)
    lane = lax.broadcasted_iota(jnp.int32, (x.shape[0], LANES), 1)
    for slab in range(WIDTH_M // LANES):
        kf = kv[:, slab * LANES:(slab + 1) * LANES]
        vf = kv[:, WIDTH_M + slab * LANES:WIDTH_M + (slab + 1) * LANES]
        for par in (0, 1):
            dm = _data_mask(lane, par)
            kn = _head_norm(kf, dm, kg_ref[...])
            k_ref[2 * slab + par] = jnp.where(dm, kn, 0.0).astype(BF16)
            v_ref[2 * slab + par] = jnp.where(dm, vf, jnp.where(lane == _ones_lane(par), 1.0, 0.0)).astype(BF16)


def _memkv(mem, gain, w, kgain):
    b, m, _ = mem.shape
    kv_shape = jax.ShapeDtypeStruct((b, N_HEADS_M, m, LANES), BF16)
    kv_spec = pl.BlockSpec((None, N_HEADS_M, m, LANES), lambda i: (i, 0, 0, 0))
    return pl.pallas_call(
        _memkv_kernel, out_shape=(kv_shape, kv_shape), grid=(b,),
        in_specs=[pl.BlockSpec((None, m, D_MODEL), lambda i: (i, 0, 0)),
                  pl.BlockSpec((1, D_MODEL), lambda i: (0, 0)),
                  pl.BlockSpec((D_MODEL, 2 * WIDTH_M), lambda i: (0, 0)),
                  pl.BlockSpec((1, LANES), lambda i: (0, 0))],
        out_specs=(kv_spec, kv_spec), compiler_params=_cparams(("parallel",)), name="memkv",
    )(mem, gain, w, kgain)


PREP_ROWS = 256
A_UNROLL = 8


def _attn_a_kernel(q_ref, k_ref, v_ref, b1_ref, b4_ref, b16_ref, o_ref,
                   qs, ks, vs, qm, km, vm, q4, q16, kd, vd, accd, accm, tot, s_scr):
    s_len = q_ref.shape[0]
    n_prep = s_len // PREP_ROWS
    lane_p = lax.broadcasted_iota(jnp.int32, (PREP_ROWS, LANES), 1)
    d_mid, d_far = DILATED_PATTERNS[1][1], DILATED_PATTERNS[2][1]
    len_mid, len_far = s_len // d_mid, s_len // d_far
    sub = d_far // d_mid

    def chunk_rows(c):
        return pl.ds(pl.multiple_of(c * PREP_ROWS, PREP_ROWS), PREP_ROWS)

    def mid_rows(r):
        return pl.ds(r, len_mid, stride=d_mid), pl.ds(pl.multiple_of(r * len_mid, QBLK), len_mid)

    def far_rows(c):
        return (pl.ds((c // sub) * len_mid + c % sub, len_far, stride=sub),
                pl.ds(pl.multiple_of(c * len_far, QBLK), len_far))

    def split_mid(pairs):
        def body(r, carry):
            tok, mid = mid_rows(r)
            for x_tok, x_mid, x_bf in pairs:
                x = x_tok[tok, :]
                x_mid[mid, :] = x
                x_bf[mid, :] = x.astype(BF16)
            return carry
        lax.fori_loop(0, d_mid, body, 0)

    def split_far(pairs):
        def body(c, carry):
            mid, far = far_rows(c)
            for x_mid, x_bf in pairs:
                x_bf[far, :] = x_mid[mid, :].astype(BF16)
            return carry
        lax.fori_loop(0, d_far, body, 0)

    def q_cast(c, carry):
        qs[chunk_rows(c), :] = q_ref[chunk_rows(c), :].astype(F32)
        return carry
    lax.fori_loop(0, n_prep, q_cast, 0)
    split_mid(((qs, qm, q4),))
    split_far(((qm, q16),))

    for par in (0, 1):
        def prep(c, carry, par=par):
            rows = chunk_rows(c)
            dm = _data_mask(lane_p, par)
            kf = jnp.where(dm, k_ref[rows, :].astype(F32), 0.0)
            vf = jnp.where(dm, v_ref[rows, :].astype(F32), jnp.where(lane_p == _ones_lane(par), 1.0, 0.0))
            ks[rows, :] = kf
            vs[rows, :] = vf
            kd[rows, :] = kf.astype(BF16)
            vd[rows, :] = vf.astype(BF16)
            return carry
        lax.fori_loop(0, n_prep, prep, 0)

        def run_blocks(q_src, length, b_ref, out, par=par):
            nblk = length // QBLK

            def blocks(o, carry):
                units = []
                for i in range(A_UNROLL):
                    u = o * A_UNROLL + i
                    if nblk == 1:
                        base = pl.multiple_of(u * length, QBLK)
                        qrows = pl.ds(base, QBLK)
                        keys = pl.ds(base, QBLK)
                        bias = b_ref[par]
                        kw = QBLK
                    else:
                        r = u // nblk
                        blk = u % nblk
                        q0 = blk * QBLK
                        ws = jnp.clip(q0 - DIL_RADIUS, 0, length - 2 * QBLK)
                        tid = jnp.where(blk == 0, 0, jnp.where(blk == nblk - 1, 2, 1))
                        qrows = pl.ds(pl.multiple_of(r * length + q0, QBLK), QBLK)
                        keys = pl.ds(pl.multiple_of(r * length + ws, DIL_RADIUS), 2 * QBLK)
                        bias = b_ref[par, tid]
                        kw = 2 * QBLK
                    s_scr[i, :, 0:kw] = lax.dot_general(q_src[qrows, :], kd[keys, :], (((1,), (1,)), ((), ())),
                                                        preferred_element_type=F32)
                    units.append((qrows, keys, bias, kw))
                for i, (qrows, keys, bias, kw) in enumerate(units):
                    p = jnp.exp2(s_scr[i, :, 0:kw] + bias).astype(BF16)
                    out[qrows, :] = jnp.dot(p, vd[keys, :], preferred_element_type=F32)
                return carry
            lax.fori_loop(0, s_len // QBLK // A_UNROLL, blocks, 0)

        run_blocks(q_ref, s_len, b1_ref, accd)

        def first(c, carry, par=par):
            tot[par, chunk_rows(c), :] = accd[chunk_rows(c), :]
            return carry
        lax.fori_loop(0, n_prep, first, 0)

        split_mid(((ks, km, kd), (vs, vm, vd)))
        run_blocks(q4, len_mid, b4_ref, accm)
        split_far(((km, kd), (vm, vd)))
        run_blocks(q16, len_far, b16_ref, accd)

        def merge_far(c, carry):
            mid, far = far_rows(c)
            accm[mid, :] = accm[mid, :] + accd[far, :]
            return carry
        lax.fori_loop(0, d_far, merge_far, 0)

        def merge_mid(r, carry, par=par):
            tok, mid = mid_rows(r)
            tot[par, tok, :] = tot[par, tok, :] + accm[mid, :]
            return carry
        lax.fori_loop(0, d_mid, merge_mid, 0)

    def fin(c, carry):
        rows = chunk_rows(c)
        o0 = _normalise(tot[0, rows, :], lane_p, 0)
        o1 = _normalise(tot[1, rows, :], lane_p, 1)
        o_ref[rows, :] = jnp.where(lane_p < HEAD_DIM, o0, o1).astype(BF16)
        return carry
    lax.fori_loop(0, n_prep, fin, 0)


def _attn_a(proj, s, b1, b4, b16):
    b = proj.shape[0] // s
    n_pairs = WIDTH_A // LANES

    def slab(off):
        return pl.BlockSpec((s, LANES), lambda i, p: (i, off + p))
    return pl.pallas_call(
        _attn_a_kernel, out_shape=jax.ShapeDtypeStruct((b * s, WIDTH_A), BF16), grid=(b, n_pairs),
        in_specs=[slab(0), slab(n_pairs), slab(2 * n_pairs),
                  pl.BlockSpec((2, 3, QBLK, 2 * QBLK), lambda i, p: (p, 0, 0, 0)),
                  pl.BlockSpec((2, 3, QBLK, 2 * QBLK), lambda i, p: (p, 0, 0, 0)),
                  pl.BlockSpec((2, QBLK, QBLK), lambda i, p: (p, 0, 0))],
        out_specs=pl.BlockSpec((s, LANES), lambda i, p: (i, p)),
        scratch_shapes=[pltpu.VMEM((s, LANES), F32)] * 6 + [pltpu.VMEM((s, LANES), BF16)] * 4
        + [pltpu.VMEM((s, LANES), F32)] * 2
        + [pltpu.VMEM((2, s, LANES), F32), pltpu.VMEM((A_UNROLL, QBLK, 2 * QBLK), F32)],
        compiler_params=_cparams(("parallel", "parallel")), name="attn_a",
    )(proj, proj, proj, b1, b4, b16)


def _attn_b_kernel(q_ref, k_ref, v_ref, bias_ref, o_ref, kb, vb, s_scr):
    s_len = q_ref.shape[0]
    n_prep = s_len // PREP_ROWS
    rows_total = s_len // GRID_W
    kh = min(NA_ROWS, rows_total)
    lane_p = lax.broadcasted_iota(jnp.int32, (PREP_ROWS, LANES), 1)
    lane_q = lax.broadcasted_iota(jnp.int32, (GRID_W, LANES), 1)

    def prep(c, carry):
        rows = pl.ds(pl.multiple_of(c * PREP_ROWS, PREP_ROWS), PREP_ROWS)
        kf = k_ref[rows, :].astype(F32)
        vf = v_ref[rows, :].astype(F32)
        for par in (0, 1):
            dm = _data_mask(lane_p, par)
            kb[par, rows, :] = jnp.where(dm, kf, 0.0).astype(BF16)
            vb[par, rows, :] = jnp.where(dm, vf, jnp.where(lane_p == _ones_lane(par), 1.0, 0.0)).astype(BF16)
        return carry
    lax.fori_loop(0, n_prep, prep, 0)

    unroll = 4

    def grid_rows(o, carry):
        units = []
        for i in range(unroll):
            r = o * unroll + i
            r0 = jnp.clip(r - kh // 2, 0, rows_total - kh)
            typ = jnp.where(r < kh // 2, r, jnp.where(r > rows_total - kh // 2, r - (rows_total - kh), kh // 2))
            qrows = pl.ds(pl.multiple_of(r * GRID_W, GRID_W), GRID_W)
            keys = pl.ds(pl.multiple_of(r0 * GRID_W, GRID_W), kh * GRID_W)
            for par in (0, 1):
                s_scr[2 * i + par] = lax.dot_general(q_ref[qrows, :], kb[par, keys, :], (((1,), (1,)), ((), ())),
                                                     preferred_element_type=F32)
            units.append((qrows, keys, typ))
        for i, (qrows, keys, typ) in enumerate(units):
            outs = []
            for par in (0, 1):
                p = jnp.exp2(s_scr[2 * i + par] + bias_ref[par, typ]).astype(BF16)
                acc = jnp.dot(p, vb[par, keys, :], preferred_element_type=F32)
                outs.append(_normalise(acc, lane_q, par))
            o_ref[qrows, :] = jnp.where(lane_q < HEAD_DIM, outs[0], outs[1]).astype(BF16)
        return carry
    lax.fori_loop(0, rows_total // unroll, grid_rows, 0)


def _attn_b(proj, s, bias):
    b = proj.shape[0] // s
    n_pairs = WIDTH_B // LANES
    first = 3 * WIDTH_A // LANES
    n_types, kw = bias.shape[1], bias.shape[3]

    def slab(off):
        return pl.BlockSpec((s, LANES), lambda i, p: (i, first + off + p))
    return pl.pallas_call(
        _attn_b_kernel, out_shape=jax.ShapeDtypeStruct((b * s, WIDTH_B), BF16), grid=(b, n_pairs),
        in_specs=[slab(0), slab(n_pairs), slab(2 * n_pairs),
                  pl.BlockSpec((2, n_types, GRID_W, kw), lambda i, p: (p, 0, 0, 0))],
        out_specs=pl.BlockSpec((s, LANES), lambda i, p: (i, p)),
        scratch_shapes=[pltpu.VMEM((2, s, LANES), BF16)] * 2 + [pltpu.VMEM((8, GRID_W, kw), F32)],
        compiler_params=_cparams(("parallel", "parallel")), name="attn_b",
    )(proj, proj, proj, bias)


MEM_QROWS = 256
MEM_UNROLL = 2


def _attn_m_kernel(q_ref, k_ref, v_ref, shift_ref, o_ref, s_scr):
    s_len = q_ref.shape[0]
    lane = lax.broadcasted_iota(jnp.int32, (MEM_QROWS, LANES), 1)

    def chunks(o, carry):
        all_rows = []
        for i in range(MEM_UNROLL):
            rows = pl.ds(pl.multiple_of((o * MEM_UNROLL + i) * MEM_QROWS, MEM_QROWS), MEM_QROWS)
            for par in (0, 1):
                s_scr[2 * i + par] = lax.dot_general(q_ref[rows, :], k_ref[par], (((1,), (1,)), ((), ())),
                                                     preferred_element_type=F32)
            all_rows.append(rows)
        for i, rows in enumerate(all_rows):
            outs = []
            for par in (0, 1):
                p = jnp.exp2(s_scr[2 * i + par] - shift_ref[...]).astype(BF16)
                acc = jnp.dot(p, v_ref[par], preferred_element_type=F32)
                outs.append(_normalise(acc, lane, par))
            o_ref[rows, :] = jnp.where(lane < HEAD_DIM, outs[0], outs[1]).astype(BF16)
        return carry
    lax.fori_loop(0, s_len // MEM_QROWS // MEM_UNROLL, chunks, 0)


def _attn_m(proj, s, kaug, vaug, shift):
    b = proj.shape[0] // s
    m = kaug.shape[2]
    n_pairs = WIDTH_M // LANES
    first = 3 * (WIDTH_A + WIDTH_B) // LANES
    kv_spec = pl.BlockSpec((None, 2, m, LANES), lambda i, p: (i, p, 0, 0))
    return pl.pallas_call(
        _attn_m_kernel, out_shape=jax.ShapeDtypeStruct((b * s, WIDTH_M), BF16), grid=(b, n_pairs),
        in_specs=[pl.BlockSpec((s, LANES), lambda i, p: (i, first + p)), kv_spec, kv_spec,
                  pl.BlockSpec((1, m), lambda i, p: (0, 0))],
        out_specs=pl.BlockSpec((s, LANES), lambda i, p: (i, p)),
        scratch_shapes=[pltpu.VMEM((2 * MEM_UNROLL, MEM_QROWS, m), F32)],
        compiler_params=_cparams(("parallel", "parallel")), name="attn_m",
    )(proj, kaug, vaug, shift)


def _mix_kernel(oa_ref, ob_ref, om_ref, x_ref, og_ref, wo_ref, gf_ref, wr_ref, br_ref,
                x1_ref, h3_ref, route_ref):
    rows = x_ref.shape[0]
    y = None
    off = 0
    for o_ref in (oa_ref, ob_ref, om_ref):
        width = o_ref.shape[1]
        o = o_ref[...].astype(F32)
        ms = jnp.mean(o * o, axis=-1, keepdims=True)
        mixed = (o * lax.rsqrt(ms + EPS) * og_ref[:, off:off + width]).astype(BF16)
        part = jnp.dot(mixed, wo_ref[off:off + width, :], preferred_element_type=F32)
        y = part if y is None else y + part
        off += width
    x1 = x_ref[...] + y
    x1_ref[...] = x1
    ms = jnp.mean(x1 * x1, axis=-1, keepdims=True)
    h = x1 * lax.rsqrt(ms + EPS) * gf_ref[...]
    for j in range(ROW_TILES):
        h3_ref[pl.ds(j, rows, stride=ROW_TILES), :] = h[:, j * LANES:(j + 1) * LANES]

    h_hi = h.astype(BF16)
    h_lo = (h - h_hi.astype(F32)).astype(BF16)
    logits = (jnp.dot(h_hi, wr_ref[0], preferred_element_type=F32)
              + jnp.dot(h_lo, wr_ref[0], preferred_element_type=F32)
              + jnp.dot(h_hi, wr_ref[1], preferred_element_type=F32)) + br_ref[...]
    lane = lax.broadcasted_iota(jnp.int32, (rows, LANES), 1)
    lane_f = lane.astype(F32)
    neg = -jnp.inf
    far = float(LANES)
    is_g = (lane >> 2) == (N_EXPERTS // N_GROUPS)
    gl = jnp.where(is_g, logits, neg)
    gmax = jnp.max(gl, axis=-1, keepdims=True)
    gidx = jnp.min(jnp.where(gl == gmax, lane_f, far), axis=-1, keepdims=True) - float(N_EXPERTS)
    gsum = jnp.sum(jnp.where(is_g, jnp.exp(gl - gmax), 0.0), axis=-1, keepdims=True)
    g_top = 1.0 / gsum
    in_grp = (lane >> 3).astype(F32) == gidx
    el = jnp.where(in_grp, logits, neg)
    v1 = jnp.max(el, axis=-1, keepdims=True)
    i1 = jnp.min(jnp.where(el == v1, lane_f, far), axis=-1, keepdims=True)
    el2 = jnp.where(lane_f == i1, neg, el)
    v2 = jnp.max(el2, axis=-1, keepdims=True)
    i2 = jnp.min(jnp.where(el2 == v2, lane_f, far), axis=-1, keepdims=True)
    ev = jnp.exp(v2 - v1)
    w1 = g_top / (1.0 + ev)
    w2 = g_top * ev / (1.0 + ev)
    route = jnp.where(lane == 0, i1, jnp.where(lane == 1, i2, jnp.where(lane == 2, w1,
                      jnp.where(lane == 3, w2, 0.0))))
    per_token = jnp.transpose(route)[0:4, :]
    chosen = jnp.where(lane_f == i1, 1.0, jnp.where(lane_f == i2, 1.0, 0.0))
    counts = jnp.sum(chosen, axis=0, keepdims=True)
    route_ref[...] = jnp.zeros_like(route_ref)
    route_ref[0:4, :] = per_token
    route_ref[4:5, 0:LANES] = counts


def _mix(oa, ob, om, x2, out_gain, w_out, gain_ffn, w_route, b_route):
    t = x2.shape[0]

    def rows(width):
        return pl.BlockSpec((PROJ_ROWS, width), lambda i: (i, 0))

    def whole(*shape):
        return pl.BlockSpec(shape, lambda i: (0,) * len(shape))
    return pl.pallas_call(
        _mix_kernel,
        out_shape=(jax.ShapeDtypeStruct((t, D_MODEL), F32),
                   jax.ShapeDtypeStruct((t * ROW_TILES, LANES), F32),
                   jax.ShapeDtypeStruct((t // PROJ_ROWS * ROUTE_ROWS, PROJ_ROWS), F32)),
        grid=(t // PROJ_ROWS,),
        in_specs=[rows(WIDTH_A), rows(WIDTH_B), rows(WIDTH_M), rows(D_MODEL),
                  whole(1, D_MODEL), whole(D_MODEL, D_MODEL), whole(1, D_MODEL),
                  whole(2, D_MODEL, LANES), whole(1, LANES)],
        out_specs=(rows(D_MODEL), pl.BlockSpec((PROJ_ROWS * ROW_TILES, LANES), lambda i: (i, 0)),
                   pl.BlockSpec((ROUTE_ROWS, PROJ_ROWS), lambda i: (i, 0))),
        compiler_params=_cparams(("parallel",)), name="mix",
    )(oa, ob, om, x2, out_gain, w_out, gain_ffn, w_route, b_route)


SCATTER_UNROLL = 8
W_SLOTS = 4
W_CHUNK_ROWS = 256


def _moe_weight_copies(wgu_hbm, wd_hbm, wgu_buf, wd_buf, sems, e, slot):
    copies = []
    for hbm, buf in ((wgu_hbm, wgu_buf), (wd_hbm, wd_buf)):
        for c in range(buf.shape[1] // W_CHUNK_ROWS):
            rows = pl.ds(c * W_CHUNK_ROWS, W_CHUNK_ROWS)
            copies.append(pltpu.make_async_copy(hbm.at[e, rows], buf.at[slot, rows], sems.at[slot]))
    return copies


def _moe_kernel(rstart_ref, tslot_ref, tfirst_ref, tnext_ref, npair_ref, act_ref,
                src_ref, dst_ref, gate_ref, h3_ref, wgu_hbm, wd_hbm, o_hbm,
                wgu_buf, wd_buf, sems, out_sem, xt0, xt1, lhs0, lhs1, yt0, yt1, acc):
    b = pl.program_id(0)
    n_rows = h3_ref.shape[0]
    tiles_per_block = 2 * (n_rows // ROW_TILES) // MOE_TM + N_EXPERTS + 2
    pad_rows = src_ref.shape[1] - MOE_TM
    base = b * tiles_per_block
    bufs = ((xt0, lhs0, yt0), (xt1, lhs1, yt1))
    copies = functools.partial(_moe_weight_copies, wgu_hbm, wd_hbm, wgu_buf, wd_buf, sems)

    def gather(rs, xt):
        for i in range(MOE_TM):
            off = pl.multiple_of(src_ref[0, rs + i], ROW_TILES)
            xt[i * ROW_TILES:(i + 1) * ROW_TILES, :] = h3_ref[pl.ds(off, ROW_TILES), :]

    def scatter(rs, yt):
        for i0 in range(0, MOE_TM, SCATTER_UNROLL):
            vals = []
            for i in range(i0, i0 + SCATTER_UNROLL):
                rows = pl.ds(pl.multiple_of(dst_ref[0, rs + i], ROW_TILES), ROW_TILES)
                vals.append((rows, acc[rows, :] + gate_ref[0, rs + i] * yt[i * ROW_TILES:(i + 1) * ROW_TILES, :]))
            for rows, val in vals:
                acc[rows, :] = val

    for k in range(W_SLOTS - 1):
        e0 = act_ref[b * (W_SLOTS - 1) + k]

        @pl.when(e0 >= 0)
        def _(e0=e0, k=k):
            for c in copies(e0, k):
                c.start()
    acc[...] = jnp.zeros_like(acc)
    for half, (xt, _, yt) in enumerate(bufs):
        yt[...] = jnp.zeros_like(yt)
        gather(rstart_ref[base + half], xt)

    def pair(p, carry):
        tiles = []
        for half in (0, 1):
            idx = base + 2 * p + half
            slot = tslot_ref[idx]

            @pl.when(tfirst_ref[idx] == 1)
            def _(slot=slot):
                for c in copies(0, slot):
                    c.wait()
            prev_rs = jnp.where(p > 0, rstart_ref[jnp.maximum(idx - 2, 0)], pad_rows)
            tiles.append((prev_rs, rstart_ref[idx + 2], slot, tnext_ref[idx]))

        for (prev_rs, _, _, _), (_, _, yt) in zip(tiles, bufs):
            scatter(prev_rs, yt)
        for xt, lhs, _ in bufs:
            for k in range(ROW_TILES):
                lhs[:, k * LANES:(k + 1) * LANES] = xt[pl.ds(k, MOE_TM, stride=ROW_TILES), :].astype(BF16)
        for (_, next_rs, _, _), (xt, _, _) in zip(tiles, bufs):
            gather(next_rs, xt)
        for (_, _, slot, _), (_, lhs, yt) in zip(tiles, bufs):
            gu = jnp.dot(lhs[...], wgu_buf[slot], preferred_element_type=F32)
            g = gu[:, :D_EXPERT]
            hid = (g * (1.0 / (1.0 + jnp.exp(-g)))) * gu[:, D_EXPERT:]
            y = jnp.dot(hid.astype(BF16), wd_buf[slot], preferred_element_type=F32)
            for k in range(ROW_TILES):
                yt[pl.ds(k, MOE_TM, stride=ROW_TILES), :] = y[:, k * LANES:(k + 1) * LANES]

        for _, _, slot, nxt in tiles:
            @pl.when(nxt >= 0)
            def _(slot=slot, nxt=nxt):
                for c in copies(nxt, (slot + W_SLOTS - 1) % W_SLOTS):
                    c.start()
        return carry
    n_pairs = npair_ref[b]
    lax.fori_loop(0, n_pairs, pair, 0)
    for half, (_, _, yt) in enumerate(bufs):
        scatter(rstart_ref[base + 2 * (n_pairs - 1) + half], yt)

    out_copy = pltpu.make_async_copy(acc.at[pl.ds(0, n_rows)],
                                     o_hbm.at[pl.ds(pl.multiple_of(b * n_rows, ROW_TILES), n_rows)], out_sem)
    out_copy.start()
    out_copy.wait()


def _pick(table, index):
    cols = jnp.arange(table.shape[1], dtype=jnp.int32)
    return jnp.sum(jnp.where(index[:, :, None] == cols[None, None, :], table[:, None, :], 0), axis=2)


def _moe(h3, route_t, wgu, wd, block_tokens):
    t = h3.shape[0] // ROW_TILES
    nblk = t // block_tokens
    n_assign = 2 * block_tokens
    nt_max = n_assign // MOE_TM + N_EXPERTS + 2

    n_pad = MOE_TM - 1
    stride = 2 * n_assign
    n_list = n_assign + N_EXPERTS * MOE_TM
    rec = route_t.reshape(nblk, block_tokens // PROJ_ROWS, ROUTE_ROWS, PROJ_ROWS)
    e = rec[:, :, 0:2, :].astype(jnp.int32).reshape(nblk, n_assign)
    w = rec[:, :, 2:4, :].reshape(nblk, n_assign)
    cnt = jnp.sum(rec[:, :, 4, :N_EXPERTS], axis=1).astype(jnp.int32)
    experts = jnp.arange(N_EXPERTS, dtype=jnp.int32)
    key = e * stride + jnp.arange(n_assign, dtype=jnp.int32)[None, :]
    pad_key = (experts[:, None] * stride + n_assign + jnp.arange(n_pad, dtype=jnp.int32)[None, :]).reshape(1, -1)
    tail_key = (N_EXPERTS * stride + n_assign + experts)[None, :]
    extra_key = jnp.broadcast_to(jnp.concatenate([pad_key, tail_key], axis=1), (nblk, n_list - n_assign))
    key_s, w_s = lax.sort((jnp.concatenate([key, extra_key], axis=1),
                           jnp.concatenate([w, jnp.zeros((nblk, n_list - n_assign), F32)], axis=1)),
                          dimension=1, num_keys=1)
    idx_s = key_s & (stride - 1)
    real = idx_s < n_assign
    row_off = ((idx_s // (2 * PROJ_ROWS)) * PROJ_ROWS + idx_s % PROJ_ROWS) * ROW_TILES
    src_s = jnp.where(real, row_off, 0)
    dst_s = jnp.where(real, row_off, block_tokens * ROW_TILES)
    cstart = jnp.cumsum(cnt, axis=1) - cnt + experts[None, :] * n_pad
    ntile_e = (cnt + MOE_TM - 1) // MOE_TM
    tend = jnp.cumsum(ntile_e, axis=1)
    ntiles = tend[:, -1:]
    active = (ntile_e > 0).astype(jnp.int32)
    order_e = jnp.cumsum(active, axis=1) - active
    ranks = jnp.arange(N_EXPERTS + W_SLOTS, dtype=jnp.int32)
    hit = (order_e[:, None, :] == ranks[None, :, None]) & (active[:, None, :] == 1)
    act = jnp.sum(jnp.where(hit, experts[None, None, :] + 1, 0), axis=2) - 1
    tile_ids = jnp.arange(nt_max, dtype=jnp.int32)[None, :]
    live = tile_ids < ntiles
    texp = jnp.minimum(jnp.sum((tile_ids[:, :, None] >= tend[:, None, :]).astype(jnp.int32), axis=2), N_EXPERTS - 1)
    texp = jnp.where(live, texp, _pick(texp, jnp.maximum(ntiles - 1, 0)))
    within = tile_ids - _pick(tend - ntile_e, texp)
    rstart = jnp.where(live, _pick(cstart, texp) + within * MOE_TM, n_list - MOE_TM)
    torder = _pick(order_e, texp)
    tfirst = (live & (within == 0)).astype(jnp.int32)
    tnext = jnp.where(tfirst == 1, _pick(act, torder + (W_SLOTS - 1)), -1)
    npairs = (ntiles[:, 0] + 1) // 2

    def flat(x):
        return x.reshape(-1).astype(jnp.int32)

    def smem_list():
        return pl.BlockSpec((None, 1, n_list), lambda b, *_: (b, 0, 0), memory_space=pltpu.SMEM)
    tile_buf = pltpu.VMEM((MOE_TM * ROW_TILES, LANES), F32)
    grid_spec = pltpu.PrefetchScalarGridSpec(
        num_scalar_prefetch=6, grid=(nblk,),
        in_specs=[smem_list(), smem_list(), smem_list(),
                  pl.BlockSpec((block_tokens * ROW_TILES, LANES), lambda b, *_: (b, 0),
                               pipeline_mode=pl.Buffered(1)),
                  pl.BlockSpec(memory_space=pl.ANY), pl.BlockSpec(memory_space=pl.ANY)],
        out_specs=pl.BlockSpec(memory_space=pl.ANY),
        scratch_shapes=[pltpu.VMEM((W_SLOTS, D_MODEL, 2 * D_EXPERT), BF16),
                        pltpu.VMEM((W_SLOTS, D_EXPERT, D_MODEL), BF16),
                        pltpu.SemaphoreType.DMA((W_SLOTS,)), pltpu.SemaphoreType.DMA(()),
                        tile_buf, tile_buf,
                        pltpu.VMEM((MOE_TM, D_MODEL), BF16), pltpu.VMEM((MOE_TM, D_MODEL), BF16),
                        tile_buf, tile_buf,
                        pltpu.VMEM(((block_tokens + 1) * ROW_TILES, LANES), F32)])
    return pl.pallas_call(
        _moe_kernel, out_shape=jax.ShapeDtypeStruct((t * ROW_TILES, LANES), F32), grid_spec=grid_spec,
        compiler_params=_cparams(("arbitrary",)), name="moe",
    )(flat(rstart), flat(torder % W_SLOTS), flat(tfirst), flat(tnext), flat(npairs), flat(act[:, :W_SLOTS - 1]),
      src_s.reshape(nblk, 1, n_list), dst_s.reshape(nblk, 1, n_list), w_s.reshape(nblk, 1, n_list), h3, wgu, wd)


def _add_kernel(a_ref, b_ref, o_ref):
    o_ref[...] = a_ref[...] + _from_token_tiles(b_ref)


def _add(a, b_tiles):
    t = a.shape[0]
    spec = pl.BlockSpec((PROJ_ROWS, D_MODEL), lambda i: (i, 0))
    tile_spec = pl.BlockSpec((PROJ_ROWS * ROW_TILES, LANES), lambda i: (i, 0))
    return pl.pallas_call(_add_kernel, out_shape=jax.ShapeDtypeStruct(a.shape, a.dtype), grid=(t // PROJ_ROWS,),
                          in_specs=[spec, tile_spec], out_specs=spec, compiler_params=_cparams(("parallel",)),
                          name="residual_add")(a, b_tiles)


def _score_bound(gq, gk):
    return (HEAD_DIM * ATTN_SCALE * LOG2E) * jnp.max(jnp.abs(gq)) * jnp.max(jnp.abs(gk))


def _alibi_tables(shift):
    slopes = (2.0 ** (-8.0 * np.arange(1, N_HEADS_A + 1) / N_HEADS_A)).astype(np.float32)
    tables = []
    for _, dil in DILATED_PATTERNS:
        if dil == DILATED_PATTERNS[-1][1]:
            offs, kw = (0,), QBLK
        else:
            offs, kw = (0, DIL_RADIUS, 2 * DIL_RADIUS), 2 * QBLK
        tiles = []
        for off in offs:
            rel = np.abs((off + np.arange(QBLK))[:, None] - np.arange(kw)[None, :])
            tiles.append(np.where(rel <= DIL_RADIUS, rel * dil, np.inf).astype(np.float32))
        dist = np.stack(tiles)
        tab = -(slopes * LOG2E)[:, None, None, None] * dist[None]
        tab = jnp.asarray(tab) - shift
        tables.append(tab[:, 0] if len(offs) == 1 else tab)
    return tables


def _rpb_table(rpb, shift, rows_total):
    kh = min(NA_ROWS, rows_total)
    c = np.arange(GRID_W)
    c0 = np.clip(c - NA_COLS // 2, 0, GRID_W - NA_COLS)
    col_ok = (c[None, :] >= c0[:, None]) & (c[None, :] < c0[:, None] + NA_COLS)
    dc = np.clip(c[None, :] - c[:, None], -(NA_COLS - 1), NA_COLS - 1) + NA_COLS - 1
    type_rows = list(range(kh // 2)) + [kh // 2] + list(range(rows_total - kh // 2 + 1, rows_total))
    dr = np.array([[int(np.clip(r - kh // 2, 0, rows_total - kh)) + j - r + NA_ROWS - 1 for j in range(kh)]
                   for r in type_rows])
    n_dc = 2 * NA_COLS - 1
    pick = dc[None, None, :, :, None] == np.arange(n_dc)[None, None, None, None, :]
    cols = jnp.sum(jnp.where(pick, rpb[:, :, None, None, :], 0.0), axis=-1)
    tab = jnp.stack([cols[:, int(d)] for d in dr.reshape(-1)], axis=1).reshape(
        rpb.shape[0], len(type_rows), kh, GRID_W, GRID_W)
    tab = jnp.where(col_ok[None, None, None], tab, -jnp.inf)
    tab = tab.transpose(0, 1, 3, 2, 4).reshape(rpb.shape[0], len(type_rows), GRID_W, kh * GRID_W) * LOG2E
    return tab - shift[:, None, None, None]


def kernel(x, mem, norm_mix, w_in, qk_gain, rpb, norm_mem, w_mem_kv, out_gain, w_out, norm_ffn, w_group,
           b_group, w_router, b_router, w_gate, w_up, w_down):
    bsz, seq, d = x.shape
    t = bsz * seq
    depth = w_in.shape[0]
    rows_total = seq // GRID_W

    x2 = x.reshape(t, d)
    moe = None
    for l in range(depth):
        gains = jnp.tile(qk_gain[l], (1, 2))
        col_scale = jnp.concatenate([
            jnp.tile(gains[0] * QSCALE, WIDTH_A // LANES), jnp.tile(gains[1], WIDTH_A // LANES), jnp.ones((WIDTH_A,), F32),
            jnp.tile(gains[2] * QSCALE, WIDTH_B // LANES), jnp.tile(gains[3], WIDTH_B // LANES), jnp.ones((WIDTH_B,), F32),
            jnp.tile(gains[4] * QSCALE, WIDTH_M // LANES)])[None, :]
        w_in_l = w_in[l].astype(BF16)
        if moe is None:
            proj = _proj(x2, None, norm_mix[l][None, :], w_in_l, col_scale)
        else:
            x2, proj = _proj(x2, moe, norm_mix[l][None, :], w_in_l, col_scale)

        shift_a = _score_bound(qk_gain[l, 0], qk_gain[l, 1])
        b1, b4, b16 = _alibi_tables(shift_a)
        oa = _attn_a(proj, seq, b1, b4, b16)

        shift_b = _score_bound(qk_gain[l, 2], qk_gain[l, 3]) + LOG2E * jnp.max(rpb[l], axis=(1, 2))
        ob = _attn_b(proj, seq, _rpb_table(rpb[l], shift_b, rows_total))

        kaug, vaug = _memkv(mem, norm_mem[l][None, :], w_mem_kv[l].astype(BF16), gains[5:6])
        shift_m = jnp.full((1, mem.shape[1]), _score_bound(qk_gain[l, 4], qk_gain[l, 5]), F32)
        om = _attn_m(proj, seq, kaug, vaug, shift_m)

        w_route = jnp.zeros((d, LANES), F32).at[:, :N_EXPERTS].set(w_router[l])
        w_route = w_route.at[:, N_EXPERTS:N_EXPERTS + N_GROUPS].set(w_group[l])
        w_route_hi = w_route.astype(BF16)
        w_route_lo = (w_route - w_route_hi.astype(F32)).astype(BF16)
        b_route = jnp.zeros((1, LANES), F32).at[0, :N_EXPERTS].set(b_router[l])
        b_route = b_route.at[0, N_EXPERTS:N_EXPERTS + N_GROUPS].set(b_group[l])
        x2, h3, route = _mix(oa, ob, om, x2,
                             out_gain[l][None, :], w_out[l].astype(BF16), norm_ffn[l][None, :],
                             jnp.stack([w_route_hi, w_route_lo]), b_route)

        wgu = jnp.concatenate([w_gate[l], w_up[l]], axis=-1).astype(BF16)
        moe = _moe(h3, route, wgu, w_down[l].astype(BF16), MOE_BLOCK_TOKENS)
    return _add(x2, moe).reshape(bsz, seq, d)
```

```python
import functools

import numpy as np
import jax
import jax.numpy as jnp
from jax import lax
from jax.experimental import pallas as pl
from jax.experimental.pallas import tpu as pltpu

F32 = jnp.float32
BF16 = jnp.bfloat16

D_MODEL = 1024
HEAD_DIM = 64
N_HEADS_A = 6
N_HEADS_B = 6
N_HEADS_M = 4
WIDTH_A = N_HEADS_A * HEAD_DIM
WIDTH_B = N_HEADS_B * HEAD_DIM
WIDTH_M = N_HEADS_M * HEAD_DIM
IN_WIDTH = 3 * WIDTH_A + 3 * WIDTH_B + WIDTH_M
DILATED_PATTERNS = ((128, 1), (512, 4), (2048, 16))
DIL_RADIUS = 64
GRID_W = 64
NA_ROWS = 8
NA_COLS = 16
N_GROUPS = 4
EXPERTS_PER_GROUP = 8
N_EXPERTS = N_GROUPS * EXPERTS_PER_GROUP
D_EXPERT = 512
EPS = 1e-6
ATTN_SCALE = HEAD_DIM ** -0.5
LOG2E = float(np.log2(np.e))
QSCALE = ATTN_SCALE * LOG2E

LANES = 128
SUBLANES = 8
ROW_TILES = D_MODEL // LANES
QBLK = 128
PROJ_ROWS = 512
ROUTE_ROWS = 8
MOE_TM = 128
MOE_BLOCK_TOKENS = 4096
VMEM_LIMIT = 56 * 1024 * 1024


def _cparams(sem):
    return pltpu.CompilerParams(dimension_semantics=sem, vmem_limit_bytes=VMEM_LIMIT)


def _from_token_tiles(t_ref):
    rows = t_ref.shape[0] // ROW_TILES
    return jnp.concatenate([t_ref[pl.ds(j, rows, stride=ROW_TILES), :] for j in range(ROW_TILES)], axis=1)


def _qk_slabs():
    a, b, m = WIDTH_A // LANES, WIDTH_B // LANES, WIDTH_M // LANES
    return tuple(range(0, 2 * a)) + tuple(range(3 * a, 3 * a + 2 * b)) + tuple(range(3 * (a + b), 3 * (a + b) + m))


def _proj_kernel(*refs, add):
    if add:
        x_ref, m_ref, g_ref, w_ref, cs_ref, xo_ref, o_ref = refs
        x = x_ref[...] + _from_token_tiles(m_ref)
        xo_ref[...] = x
    else:
        x_ref, g_ref, w_ref, cs_ref, o_ref = refs
        x = x_ref[...]
    ms = jnp.mean(x * x, axis=-1, keepdims=True)
    h = (x * lax.rsqrt(ms + EPS) * g_ref[...]).astype(BF16)
    acc = jnp.dot(h, w_ref[...], preferred_element_type=F32)
    lo = lax.broadcasted_iota(jnp.int32, (x.shape[0], LANES), 1) < HEAD_DIM
    qk = _qk_slabs()
    for s in range(IN_WIDTH // LANES):
        cols = slice(s * LANES, (s + 1) * LANES)
        y = acc[:, cols]
        if s in qk:
            sq = y * y
            ss_lo = jnp.sum(jnp.where(lo, sq, 0.0), axis=-1, keepdims=True)
            ss_hi = jnp.sum(jnp.where(lo, 0.0, sq), axis=-1, keepdims=True)
            inv = jnp.where(lo, lax.rsqrt(ss_lo * (1.0 / HEAD_DIM) + EPS), lax.rsqrt(ss_hi * (1.0 / HEAD_DIM) + EPS))
            y = y * inv * cs_ref[:, cols]
        o_ref[:, cols] = y.astype(BF16)


def _proj(x2, moe2, gain, w, col_scale):
    t = x2.shape[0]
    add = moe2 is not None
    row_spec = pl.BlockSpec((PROJ_ROWS, D_MODEL), lambda i: (i, 0))
    tile_spec = pl.BlockSpec((PROJ_ROWS * ROW_TILES, LANES), lambda i: (i, 0))
    in_specs = [row_spec] + ([tile_spec] if add else []) + [
        pl.BlockSpec((1, D_MODEL), lambda i: (0, 0)),
        pl.BlockSpec((D_MODEL, IN_WIDTH), lambda i: (0, 0)),
        pl.BlockSpec((1, IN_WIDTH), lambda i: (0, 0)),
    ]
    proj_spec = pl.BlockSpec((PROJ_ROWS, IN_WIDTH), lambda i: (i, 0))
    proj_shape = jax.ShapeDtypeStruct((t, IN_WIDTH), BF16)
    if add:
        out_shape = (jax.ShapeDtypeStruct((t, D_MODEL), F32), proj_shape)
        out_specs = (row_spec, proj_spec)
        args = (x2, moe2, gain, w, col_scale)
    else:
        out_shape, out_specs, args = proj_shape, proj_spec, (x2, gain, w, col_scale)
    return pl.pallas_call(
        functools.partial(_proj_kernel, add=add),
        out_shape=out_shape, grid=(t // PROJ_ROWS,), in_specs=in_specs, out_specs=out_specs,
        compiler_params=_cparams(("parallel",)), name="proj_add" if add else "proj",
    )(*args)


def _data_mask(lane, par):
    return (lane < HEAD_DIM) if par == 0 else (lane >= HEAD_DIM)


def _ones_lane(par):
    return HEAD_DIM if par == 0 else 0


def _head_norm(x, dm, gain):
    ss = jnp.sum(jnp.where(dm, x * x, 0.0), axis=-1, keepdims=True)
    return x * lax.rsqrt(ss * (1.0 / HEAD_DIM) + EPS) * gain


def _normalise(tot, lane, par):
    den = jnp.sum(jnp.where(lane == _ones_lane(par), tot, 0.0), axis=-1, keepdims=True)
    return tot / den


def _memkv_kernel(m_ref, g_ref, w_ref, kg_ref, k_ref, v_ref):
    x = m_ref[...]
    ms = jnp.mean(x * x, axis=-1, keepdims=True)
    h = (x * lax.rsqrt(ms + EPS) * g_ref[...]).astype(BF16)
    kv = jnp.dot(h, w_ref[...], preferred_element_type=F32)
    lane = lax.broadcasted_iota(jnp.int32, (x.shape[0], LANES), 1)
    for slab in range(WIDTH_M // LANES):
        kf = kv[:, slab * LANES:(slab + 1) * LANES]
        vf = kv[:, WIDTH_M + slab * LANES:WIDTH_M + (slab + 1) * LANES]
        for par in (0, 1):
            dm = _data_mask(lane, par)
            kn = _head_norm(kf, dm, kg_ref[...])
            k_ref[2 * slab + par] = jnp.where(dm, kn, 0.0).astype(BF16)
            v_ref[2 * slab + par] = jnp.where(dm, vf, jnp.where(lane == _ones_lane(par), 1.0, 0.0)).astype(BF16)


def _memkv(mem, gain, w, kgain):
    b, m, _ = mem.shape
    kv_shape = jax.ShapeDtypeStruct((b, N_HEADS_M, m, LANES), BF16)
    kv_spec = pl.BlockSpec((None, N_HEADS_M, m, LANES), lambda i: (i, 0, 0, 0))
    return pl.pallas_call(
        _memkv_kernel, out_shape=(kv_shape, kv_shape), grid=(b,),
        in_specs=[pl.BlockSpec((None, m, D_MODEL), lambda i: (i, 0, 0)),
                  pl.BlockSpec((1, D_MODEL), lambda i: (0, 0)),
                  pl.BlockSpec((D_MODEL, 2 * WIDTH_M), lambda i: (0, 0)),
                  pl.BlockSpec((1, LANES), lambda i: (0, 0))],
        out_specs=(kv_spec, kv_spec), compiler_params=_cparams(("parallel",)), name="memkv",
    )(mem, gain, w, kgain)


PREP_ROWS = 256
A_UNROLL = 16


def _attn_a_kernel(q_ref, k_ref, v_ref, b1_ref, b4_ref, b16_ref, o_ref,
                   qs, ks, vs, qm, km, vm, q4, q16, kd, vd, accd, accm, tot, s_scr):
    s_len = q_ref.shape[0]
    n_prep = s_len // PREP_ROWS
    lane_p = lax.broadcasted_iota(jnp.int32, (PREP_ROWS, LANES), 1)
    d_mid, d_far = DILATED_PATTERNS[1][1], DILATED_PATTERNS[2][1]
    len_mid, len_far = s_len // d_mid, s_len // d_far
    sub = d_far // d_mid

    def chunk_rows(c):
        return pl.ds(pl.multiple_of(c * PREP_ROWS, PREP_ROWS), PREP_ROWS)

    def mid_rows(r):
        return pl.ds(r, len_mid, stride=d_mid), pl.ds(pl.multiple_of(r * len_mid, QBLK), len_mid)

    def far_rows(c):
        return (pl.ds((c // sub) * len_mid + c % sub, len_far, stride=sub),
                pl.ds(pl.multiple_of(c * len_far, QBLK), len_far))

    def split_mid(pairs):
        def body(r, carry):
            tok, mid = mid_rows(r)
            for x_tok, x_mid, x_bf in pairs:
                x = x_tok[tok, :]
                x_mid[mid, :] = x
                x_bf[mid, :] = x.astype(BF16)
            return carry
        lax.fori_loop(0, d_mid, body, 0)

    def split_far(pairs):
        def body(c, carry):
            mid, far = far_rows(c)
            for x_mid, x_bf in pairs:
                x_bf[far, :] = x_mid[mid, :].astype(BF16)
            return carry
        lax.fori_loop(0, d_far, body, 0)

    def q_cast(c, carry):
        qs[chunk_rows(c), :] = q_ref[chunk_rows(c), :].astype(F32)
        return carry
    lax.fori_loop(0, n_prep, q_cast, 0)
    split_mid(((qs, qm, q4),))
    split_far(((qm, q16),))

    for par in (0, 1):
        def prep(c, carry, par=par):
            rows = chunk_rows(c)
            dm = _data_mask(lane_p, par)
            kf = jnp.where(dm, k_ref[rows, :].astype(F32), 0.0)
            vf = jnp.where(dm, v_ref[rows, :].astype(F32), jnp.where(lane_p == _ones_lane(par), 1.0, 0.0))
            ks[rows, :] = kf
            vs[rows, :] = vf
            kd[rows, :] = kf.astype(BF16)
            vd[rows, :] = vf.astype(BF16)
            return carry
        lax.fori_loop(0, n_prep, prep, 0)

        def run_blocks(q_src, length, b_ref, out, par=par):
            nblk = length // QBLK

            def blocks(o, carry):
                units = []
                for i in range(A_UNROLL):
                    u = o * A_UNROLL + i
                    if nblk == 1:
                        base = pl.multiple_of(u * length, QBLK)
                        qrows = pl.ds(base, QBLK)
                        keys = pl.ds(base, QBLK)
                        bias = b_ref[par]
                        kw = QBLK
                    else:
                        r = u // nblk
                        blk = u % nblk
                        q0 = blk * QBLK
                        ws = jnp.clip(q0 - DIL_RADIUS, 0, length - 2 * QBLK)
                        tid = jnp.where(blk == 0, 0, jnp.where(blk == nblk - 1, 2, 1))
                        qrows = pl.ds(pl.multiple_of(r * length + q0, QBLK), QBLK)
                        keys = pl.ds(pl.multiple_of(r * length + ws, DIL_RADIUS), 2 * QBLK)
                        bias = b_ref[par, tid]
                        kw = 2 * QBLK
                    s_scr[i, :, 0:kw] = lax.dot_general(q_src[qrows, :], kd[keys, :], (((1,), (1,)), ((), ())),
                                                        preferred_element_type=F32)
                    units.append((qrows, keys, bias, kw))
                for i, (qrows, keys, bias, kw) in enumerate(units):
                    p = jnp.exp2(s_scr[i, :, 0:kw] + bias).astype(BF16)
                    out[qrows, :] = jnp.dot(p, vd[keys, :], preferred_element_type=F32)
                return carry
            lax.fori_loop(0, s_len // QBLK // A_UNROLL, blocks, 0)

        run_blocks(q_ref, s_len, b1_ref, accd)

        def first(c, carry, par=par):
            tot[par, chunk_rows(c), :] = accd[chunk_rows(c), :]
            return carry
        lax.fori_loop(0, n_prep, first, 0)

        split_mid(((ks, km, kd), (vs, vm, vd)))
        run_blocks(q4, len_mid, b4_ref, accm)
        split_far(((km, kd), (vm, vd)))
        run_blocks(q16, len_far, b16_ref, accd)

        def merge_far(c, carry):
            mid, far = far_rows(c)
            accm[mid, :] = accm[mid, :] + accd[far, :]
            return carry
        lax.fori_loop(0, d_far, merge_far, 0)

        def merge_mid(r, carry, par=par):
            tok, mid = mid_rows(r)
            tot[par, tok, :] = tot[par, tok, :] + accm[mid, :]
            return carry
        lax.fori_loop(0, d_mid, merge_mid, 0)

    def fin(c, carry):
        rows = chunk_rows(c)
        o0 = _normalise(tot[0, rows, :], lane_p, 0)
        o1 = _normalise(tot[1, rows, :], lane_p, 1)
        o_ref[rows, :] = jnp.where(lane_p < HEAD_DIM, o0, o1).astype(BF16)
        return carry
    lax.fori_loop(0, n_prep, fin, 0)


def _attn_a(proj, s, b1, b4, b16):
    b = proj.shape[0] // s
    n_pairs = WIDTH_A // LANES

    def slab(off):
        return pl.BlockSpec((s, LANES), lambda i, p: (i, off + p))
    return pl.pallas_call(
        _attn_a_kernel, out_shape=jax.ShapeDtypeStruct((b * s, WIDTH_A), BF16), grid=(b, n_pairs),
        in_specs=[slab(0), slab(n_pairs), slab(2 * n_pairs),
                  pl.BlockSpec((2, 3, QBLK, 2 * QBLK), lambda i, p: (p, 0, 0, 0)),
                  pl.BlockSpec((2, 3, QBLK, 2 * QBLK), lambda i, p: (p, 0, 0, 0)),
                  pl.BlockSpec((2, QBLK, QBLK), lambda i, p: (p, 0, 0))],
        out_specs=pl.BlockSpec((s, LANES), lambda i, p: (i, p)),
        scratch_shapes=[pltpu.VMEM((s, LANES), F32)] * 6 + [pltpu.VMEM((s, LANES), BF16)] * 4
        + [pltpu.VMEM((s, LANES), F32)] * 2
        + [pltpu.VMEM((2, s, LANES), F32), pltpu.VMEM((A_UNROLL, QBLK, 2 * QBLK), F32)],
        compiler_params=_cparams(("parallel", "parallel")), name="attn_a",
    )(proj, proj, proj, b1, b4, b16)


NB_GROUP = 4
NB_UNROLL = 2


def _attn_b_kernel(q_ref, k_ref, v_ref, bias_ref, o_ref, kb, vb, s_scr):
    s_len = q_ref.shape[0]
    n_prep = s_len // PREP_ROWS
    rows_total = s_len // GRID_W
    kh = min(NA_ROWS, rows_total)
    n_groups = rows_total // NB_GROUP
    q_len = NB_GROUP * GRID_W
    k_len = (NB_GROUP + kh) * GRID_W
    lane_p = lax.broadcasted_iota(jnp.int32, (PREP_ROWS, LANES), 1)
    lane_q = lax.broadcasted_iota(jnp.int32, (q_len, LANES), 1)

    def prep(c, carry):
        rows = pl.ds(pl.multiple_of(c * PREP_ROWS, PREP_ROWS), PREP_ROWS)
        kf = k_ref[rows, :].astype(F32)
        vf = v_ref[rows, :].astype(F32)
        for par in (0, 1):
            dm = _data_mask(lane_p, par)
            kb[par, rows, :] = jnp.where(dm, kf, 0.0).astype(BF16)
            vb[par, rows, :] = jnp.where(dm, vf, jnp.where(lane_p == _ones_lane(par), 1.0, 0.0)).astype(BF16)
        return carry
    lax.fori_loop(0, n_prep, prep, 0)

    def groups(o, carry):
        units = []
        for i in range(NB_UNROLL):
            g = o * NB_UNROLL + i
            ws = jnp.clip(g * NB_GROUP - kh // 2, 0, rows_total - (NB_GROUP + kh))
            typ = jnp.where(g == 0, 0, jnp.where(g == n_groups - 1, 2, 1))
            qrows = pl.ds(pl.multiple_of(g * q_len, q_len), q_len)
            keys = pl.ds(pl.multiple_of(ws * GRID_W, GRID_W), k_len)
            for par in (0, 1):
                s_scr[2 * i + par] = lax.dot_general(q_ref[qrows, :], kb[par, keys, :], (((1,), (1,)), ((), ())),
                                                     preferred_element_type=F32)
            units.append((qrows, keys, typ))
        for i, (qrows, keys, typ) in enumerate(units):
            outs = []
            for par in (0, 1):
                p = jnp.exp2(s_scr[2 * i + par] + bias_ref[par, typ]).astype(BF16)
                acc = jnp.dot(p, vb[par, keys, :], preferred_element_type=F32)
                outs.append(_normalise(acc, lane_q, par))
            o_ref[qrows, :] = jnp.where(lane_q < HEAD_DIM, outs[0], outs[1]).astype(BF16)
        return carry
    lax.fori_loop(0, n_groups // NB_UNROLL, groups, 0)


def _attn_b(proj, s, bias):
    b = proj.shape[0] // s
    n_pairs = WIDTH_B // LANES
    first = 3 * WIDTH_A // LANES
    n_types, q_len, k_len = bias.shape[1:]

    def slab(off):
        return pl.BlockSpec((s, LANES), lambda i, p: (i, first + off + p))
    return pl.pallas_call(
        _attn_b_kernel, out_shape=jax.ShapeDtypeStruct((b * s, WIDTH_B), BF16), grid=(b, n_pairs),
        in_specs=[slab(0), slab(n_pairs), slab(2 * n_pairs),
                  pl.BlockSpec((2, n_types, q_len, k_len), lambda i, p: (p, 0, 0, 0))],
        out_specs=pl.BlockSpec((s, LANES), lambda i, p: (i, p)),
        scratch_shapes=[pltpu.VMEM((2, s, LANES), BF16)] * 2 + [pltpu.VMEM((2 * NB_UNROLL, q_len, k_len), F32)],
        compiler_params=_cparams(("parallel", "parallel")), name="attn_b",
    )(proj, proj, proj, bias)


MEM_QROWS = 256
MEM_UNROLL = 2


def _attn_m_kernel(q_ref, k_ref, v_ref, shift_ref, o_ref, s_scr):
    s_len = q_ref.shape[0]
    lane = lax.broadcasted_iota(jnp.int32, (MEM_QROWS, LANES), 1)

    def chunks(o, carry):
        all_rows = []
        for i in range(MEM_UNROLL):
            rows = pl.ds(pl.multiple_of((o * MEM_UNROLL + i) * MEM_QROWS, MEM_QROWS), MEM_QROWS)
            for par in (0, 1):
                s_scr[2 * i + par] = lax.dot_general(q_ref[rows, :], k_ref[par], (((1,), (1,)), ((), ())),
                                                     preferred_element_type=F32)
            all_rows.append(rows)
        for i, rows in enumerate(all_rows):
            outs = []
            for par in (0, 1):
                p = jnp.exp2(s_scr[2 * i + par] - shift_ref[...]).astype(BF16)
                acc = jnp.dot(p, v_ref[par], preferred_element_type=F32)
                outs.append(_normalise(acc, lane, par))
            o_ref[rows, :] = jnp.where(lane < HEAD_DIM, outs[0], outs[1]).astype(BF16)
        return carry
    lax.fori_loop(0, s_len // MEM_QROWS // MEM_UNROLL, chunks, 0)


def _attn_m(proj, s, kaug, vaug, shift):
    b = proj.shape[0] // s
    m = kaug.shape[2]
    n_pairs = WIDTH_M // LANES
    first = 3 * (WIDTH_A + WIDTH_B) // LANES
    kv_spec = pl.BlockSpec((None, 2, m, LANES), lambda i, p: (i, p, 0, 0))
    return pl.pallas_call(
        _attn_m_kernel, out_shape=jax.ShapeDtypeStruct((b * s, WIDTH_M), BF16), grid=(b, n_pairs),
        in_specs=[pl.BlockSpec((s, LANES), lambda i, p: (i, first + p)), kv_spec, kv_spec,
                  pl.BlockSpec((1, m), lambda i, p: (0, 0))],
        out_specs=pl.BlockSpec((s, LANES), lambda i, p: (i, p)),
        scratch_shapes=[pltpu.VMEM((2 * MEM_UNROLL, MEM_QROWS, m), F32)],
        compiler_params=_cparams(("parallel", "parallel")), name="attn_m",
    )(proj, kaug, vaug, shift)


def _mix_kernel(oa_ref, ob_ref, om_ref, x_ref, og_ref, wo_ref, gf_ref, wr_ref, br_ref,
                x1_ref, h3_ref, route_ref):
    rows = x_ref.shape[0]
    y = None
    off = 0
    for o_ref in (oa_ref, ob_ref, om_ref):
        width = o_ref.shape[1]
        o = o_ref[...].astype(F32)
        ms = jnp.mean(o * o, axis=-1, keepdims=True)
        mixed = (o * lax.rsqrt(ms + EPS) * og_ref[:, off:off + width]).astype(BF16)
        part = jnp.dot(mixed, wo_ref[off:off + width, :], preferred_element_type=F32)
        y = part if y is None else y + part
        off += width
    x1 = x_ref[...] + y
    x1_ref[...] = x1
    ms = jnp.mean(x1 * x1, axis=-1, keepdims=True)
    h = x1 * lax.rsqrt(ms + EPS) * gf_ref[...]
    for j in range(ROW_TILES):
        h3_ref[pl.ds(j, rows, stride=ROW_TILES), :] = h[:, j * LANES:(j + 1) * LANES]

    h_hi = h.astype(BF16)
    h_lo = (h - h_hi.astype(F32)).astype(BF16)
    logits = (jnp.dot(h_hi, wr_ref[0], preferred_element_type=F32)
              + jnp.dot(h_lo, wr_ref[0], preferred_element_type=F32)
              + jnp.dot(h_hi, wr_ref[1], preferred_element_type=F32)) + br_ref[...]
    lane = lax.broadcasted_iota(jnp.int32, (rows, LANES), 1)
    lane_f = lane.astype(F32)
    neg = -jnp.inf
    far = float(LANES)
    is_g = (lane >> 2) == (N_EXPERTS // N_GROUPS)
    gl = jnp.where(is_g, logits, neg)
    gmax = jnp.max(gl, axis=-1, keepdims=True)
    gidx = jnp.min(jnp.where(gl == gmax, lane_f, far), axis=-1, keepdims=True) - float(N_EXPERTS)
    gsum = jnp.sum(jnp.where(is_g, jnp.exp(gl - gmax), 0.0), axis=-1, keepdims=True)
    g_top = 1.0 / gsum
    in_grp = (lane >> 3).astype(F32) == gidx
    el = jnp.where(in_grp, logits, neg)
    v1 = jnp.max(el, axis=-1, keepdims=True)
    i1 = jnp.min(jnp.where(el == v1, lane_f, far), axis=-1, keepdims=True)
    el2 = jnp.where(lane_f == i1, neg, el)
    v2 = jnp.max(el2, axis=-1, keepdims=True)
    i2 = jnp.min(jnp.where(el2 == v2, lane_f, far), axis=-1, keepdims=True)
    ev = jnp.exp(v2 - v1)
    w1 = g_top / (1.0 + ev)
    w2 = g_top * ev / (1.0 + ev)
    route = jnp.where(lane == 0, i1, jnp.where(lane == 1, i2, jnp.where(lane == 2, w1,
                      jnp.where(lane == 3, w2, 0.0))))
    per_token = jnp.transpose(route)[0:4, :]
    chosen = jnp.where(lane_f == i1, 1.0, jnp.where(lane_f == i2, 1.0, 0.0))
    counts = jnp.sum(chosen, axis=0, keepdims=True)
    route_ref[...] = jnp.zeros_like(route_ref)
    route_ref[0:4, :] = per_token
    route_ref[4:5, 0:LANES] = counts


def _mix(oa, ob, om, x2, out_gain, w_out, gain_ffn, w_route, b_route):
    t = x2.shape[0]

    def rows(width):
        return pl.BlockSpec((PROJ_ROWS, width), lambda i: (i, 0))

    def whole(*shape):
        return pl.BlockSpec(shape, lambda i: (0,) * len(shape))
    return pl.pallas_call(
        _mix_kernel,
        out_shape=(jax.ShapeDtypeStruct((t, D_MODEL), F32),
                   jax.ShapeDtypeStruct((t * ROW_TILES, LANES), F32),
                   jax.ShapeDtypeStruct((t // PROJ_ROWS * ROUTE_ROWS, PROJ_ROWS), F32)),
        grid=(t // PROJ_ROWS,),
        in_specs=[rows(WIDTH_A), rows(WIDTH_B), rows(WIDTH_M), rows(D_MODEL),
                  whole(1, D_MODEL), whole(D_MODEL, D_MODEL), whole(1, D_MODEL),
                  whole(2, D_MODEL, LANES), whole(1, LANES)],
        out_specs=(rows(D_MODEL), pl.BlockSpec((PROJ_ROWS * ROW_TILES, LANES), lambda i: (i, 0)),
                   pl.BlockSpec((ROUTE_ROWS, PROJ_ROWS), lambda i: (i, 0))),
        compiler_params=_cparams(("parallel",)), name="mix",
    )(oa, ob, om, x2, out_gain, w_out, gain_ffn, w_route, b_route)


SCATTER_UNROLL = 8
W_SLOTS = 4
W_CHUNK_ROWS = 256


def _moe_weight_copies(wgu_hbm, wd_hbm, wgu_buf, wd_buf, sems, first_expert, e, slot):
    copies = []
    for hbm, buf in ((wgu_hbm, wgu_buf), (wd_hbm, wd_buf)):
        for c in range(buf.shape[1] // W_CHUNK_ROWS):
            rows = pl.ds(c * W_CHUNK_ROWS, W_CHUNK_ROWS)
            copies.append(pltpu.make_async_copy(hbm.at[first_expert + e, rows], buf.at[slot, rows], sems.at[slot]))
    return copies


def _moe_kernel(rstart_ref, tslot_ref, tfirst_ref, tnext_ref, npair_ref, act_ref,
                src_ref, dst_ref, gate_ref, h3_ref, wgu_hbm, wd_hbm, o_hbm,
                wgu_buf, wd_buf, sems, out_sem, xt0, xt1, lhs0, lhs1, yt0, yt1, acc, *, first_expert):
    b = pl.program_id(0)
    n_rows = h3_ref.shape[0]
    tiles_per_block = 2 * (n_rows // ROW_TILES) // MOE_TM + N_EXPERTS + 2
    pad_rows = src_ref.shape[1] - MOE_TM
    base = b * tiles_per_block
    bufs = ((xt0, lhs0, yt0), (xt1, lhs1, yt1))
    copies = functools.partial(_moe_weight_copies, wgu_hbm, wd_hbm, wgu_buf, wd_buf, sems, first_expert)

    def gather(rs, xt):
        for i in range(MOE_TM):
            off = pl.multiple_of(src_ref[0, rs + i], ROW_TILES)
            xt[i * ROW_TILES:(i + 1) * ROW_TILES, :] = h3_ref[pl.ds(off, ROW_TILES), :]

    def scatter(rs, yt):
        for i0 in range(0, MOE_TM, SCATTER_UNROLL):
            vals = []
            for i in range(i0, i0 + SCATTER_UNROLL):
                rows = pl.ds(pl.multiple_of(dst_ref[0, rs + i], ROW_TILES), ROW_TILES)
                vals.append((rows, acc[rows, :] + gate_ref[0, rs + i] * yt[i * ROW_TILES:(i + 1) * ROW_TILES, :]))
            for rows, val in vals:
                acc[rows, :] = val

    for k in range(W_SLOTS - 1):
        e0 = act_ref[b * (W_SLOTS - 1) + k]

        @pl.when(e0 >= 0)
        def _(e0=e0, k=k):
            for c in copies(e0, k):
                c.start()
    acc[...] = jnp.zeros_like(acc)
    for half, (xt, _, yt) in enumerate(bufs):
        yt[...] = jnp.zeros_like(yt)
        gather(rstart_ref[base + half], xt)

    def pair(p, carry):
        tiles = []
        for half in (0, 1):
            idx = base + 2 * p + half
            slot = tslot_ref[idx]

            @pl.when(tfirst_ref[idx] == 1)
            def _(slot=slot):
                for c in copies(0, slot):
                    c.wait()
            prev_rs = jnp.where(p > 0, rstart_ref[jnp.maximum(idx - 2, 0)], pad_rows)
            tiles.append((prev_rs, rstart_ref[idx + 2], slot, tnext_ref[idx]))

        for (prev_rs, _, _, _), (_, _, yt) in zip(tiles, bufs):
            scatter(prev_rs, yt)
        for xt, lhs, _ in bufs:
            for k in range(ROW_TILES):
                lhs[:, k * LANES:(k + 1) * LANES] = xt[pl.ds(k, MOE_TM, stride=ROW_TILES), :].astype(BF16)
        for (_, next_rs, _, _), (xt, _, _) in zip(tiles, bufs):
            gather(next_rs, xt)
        for (_, _, slot, _), (_, lhs, yt) in zip(tiles, bufs):
            gu = jnp.dot(lhs[...], wgu_buf[slot], preferred_element_type=F32)
            g = gu[:, :D_EXPERT]
            hid = (g * (1.0 / (1.0 + jnp.exp(-g)))) * gu[:, D_EXPERT:]
            y = jnp.dot(hid.astype(BF16), wd_buf[slot], preferred_element_type=F32)
            for k in range(ROW_TILES):
                yt[pl.ds(k, MOE_TM, stride=ROW_TILES), :] = y[:, k * LANES:(k + 1) * LANES]

        for _, _, slot, nxt in tiles:
            @pl.when(nxt >= 0)
            def _(slot=slot, nxt=nxt):
                for c in copies(nxt, (slot + W_SLOTS - 1) % W_SLOTS):
                    c.start()
        return carry
    n_pairs = npair_ref[b]
    lax.fori_loop(0, n_pairs, pair, 0)
    for half, (_, _, yt) in enumerate(bufs):
        scatter(rstart_ref[base + 2 * (n_pairs - 1) + half], yt)

    out_copy = pltpu.make_async_copy(acc.at[pl.ds(0, n_rows)],
                                     o_hbm.at[pl.ds(pl.multiple_of(b * n_rows, ROW_TILES), n_rows)], out_sem)
    out_copy.start()
    out_copy.wait()


def _pick(table, index):
    cols = jnp.arange(table.shape[1], dtype=jnp.int32)
    return jnp.sum(jnp.where(index[:, :, None] == cols[None, None, :], table[:, None, :], 0), axis=2)


def _moe(h3, route_t, wgu, wd, first_expert, block_tokens):
    t = h3.shape[0] // ROW_TILES
    nblk = t // block_tokens
    n_assign = 2 * block_tokens
    nt_max = n_assign // MOE_TM + N_EXPERTS + 2

    n_pad = MOE_TM - 1
    stride = 2 * n_assign
    n_list = n_assign + N_EXPERTS * MOE_TM
    rec = route_t.reshape(nblk, block_tokens // PROJ_ROWS, ROUTE_ROWS, PROJ_ROWS)
    e = rec[:, :, 0:2, :].astype(jnp.int32).reshape(nblk, n_assign)
    w = rec[:, :, 2:4, :].reshape(nblk, n_assign)
    cnt = jnp.sum(rec[:, :, 4, :N_EXPERTS], axis=1).astype(jnp.int32)
    experts = jnp.arange(N_EXPERTS, dtype=jnp.int32)
    key = e * stride + jnp.arange(n_assign, dtype=jnp.int32)[None, :]
    pad_key = (experts[:, None] * stride + n_assign + jnp.arange(n_pad, dtype=jnp.int32)[None, :]).reshape(1, -1)
    tail_key = (N_EXPERTS * stride + n_assign + experts)[None, :]
    extra_key = jnp.broadcast_to(jnp.concatenate([pad_key, tail_key], axis=1), (nblk, n_list - n_assign))
    key_s, w_s = lax.sort((jnp.concatenate([key, extra_key], axis=1),
                           jnp.concatenate([w, jnp.zeros((nblk, n_list - n_assign), F32)], axis=1)),
                          dimension=1, num_keys=1)
    idx_s = key_s & (stride - 1)
    real = idx_s < n_assign
    row_off = ((idx_s // (2 * PROJ_ROWS)) * PROJ_ROWS + idx_s % PROJ_ROWS) * ROW_TILES
    src_s = jnp.where(real, row_off, 0)
    dst_s = jnp.where(real, row_off, block_tokens * ROW_TILES)
    cstart = jnp.cumsum(cnt, axis=1) - cnt + experts[None, :] * n_pad
    ntile_e = (cnt + MOE_TM - 1) // MOE_TM
    tend = jnp.cumsum(ntile_e, axis=1)
    ntiles = tend[:, -1:]
    active = (ntile_e > 0).astype(jnp.int32)
    order_e = jnp.cumsum(active, axis=1) - active
    ranks = jnp.arange(N_EXPERTS + W_SLOTS, dtype=jnp.int32)
    hit = (order_e[:, None, :] == ranks[None, :, None]) & (active[:, None, :] == 1)
    act = jnp.sum(jnp.where(hit, experts[None, None, :] + 1, 0), axis=2) - 1
    tile_ids = jnp.arange(nt_max, dtype=jnp.int32)[None, :]
    live = tile_ids < ntiles
    texp = jnp.minimum(jnp.sum((tile_ids[:, :, None] >= tend[:, None, :]).astype(jnp.int32), axis=2), N_EXPERTS - 1)
    texp = jnp.where(live, texp, _pick(texp, jnp.maximum(ntiles - 1, 0)))
    within = tile_ids - _pick(tend - ntile_e, texp)
    rstart = jnp.where(live, _pick(cstart, texp) + within * MOE_TM, n_list - MOE_TM)
    torder = _pick(order_e, texp)
    tfirst = (live & (within == 0)).astype(jnp.int32)
    tnext = jnp.where(tfirst == 1, _pick(act, torder + (W_SLOTS - 1)), -1)
    npairs = (ntiles[:, 0] + 1) // 2

    def flat(x):
        return x.reshape(-1).astype(jnp.int32)

    def smem_list():
        return pl.BlockSpec((None, 1, n_list), lambda b, *_: (b, 0, 0), memory_space=pltpu.SMEM)
    tile_buf = pltpu.VMEM((MOE_TM * ROW_TILES, LANES), F32)
    grid_spec = pltpu.PrefetchScalarGridSpec(
        num_scalar_prefetch=6, grid=(nblk,),
        in_specs=[smem_list(), smem_list(), smem_list(),
                  pl.BlockSpec((block_tokens * ROW_TILES, LANES), lambda b, *_: (b, 0),
                               pipeline_mode=pl.Buffered(1)),
                  pl.BlockSpec(memory_space=pl.ANY), pl.BlockSpec(memory_space=pl.ANY)],
        out_specs=pl.BlockSpec(memory_space=pl.ANY),
        scratch_shapes=[pltpu.VMEM((W_SLOTS, D_MODEL, 2 * D_EXPERT), BF16),
                        pltpu.VMEM((W_SLOTS, D_EXPERT, D_MODEL), BF16),
                        pltpu.SemaphoreType.DMA((W_SLOTS,)), pltpu.SemaphoreType.DMA(()),
                        tile_buf, tile_buf,
                        pltpu.VMEM((MOE_TM, D_MODEL), BF16), pltpu.VMEM((MOE_TM, D_MODEL), BF16),
                        tile_buf, tile_buf,
                        pltpu.VMEM(((block_tokens + 1) * ROW_TILES, LANES), F32)])
    return pl.pallas_call(
        functools.partial(_moe_kernel, first_expert=first_expert),
        out_shape=jax.ShapeDtypeStruct((t * ROW_TILES, LANES), F32), grid_spec=grid_spec,
        compiler_params=_cparams(("arbitrary",)), name="moe",
    )(flat(rstart), flat(torder % W_SLOTS), flat(tfirst), flat(tnext), flat(npairs), flat(act[:, :W_SLOTS - 1]),
      src_s.reshape(nblk, 1, n_list), dst_s.reshape(nblk, 1, n_list), w_s.reshape(nblk, 1, n_list), h3, wgu, wd)


def _add_kernel(a_ref, b_ref, o_ref):
    o_ref[...] = a_ref[...] + _from_token_tiles(b_ref)


def _add(a, b_tiles):
    t = a.shape[0]
    spec = pl.BlockSpec((PROJ_ROWS, D_MODEL), lambda i: (i, 0))
    tile_spec = pl.BlockSpec((PROJ_ROWS * ROW_TILES, LANES), lambda i: (i, 0))
    return pl.pallas_call(_add_kernel, out_shape=jax.ShapeDtypeStruct(a.shape, a.dtype), grid=(t // PROJ_ROWS,),
                          in_specs=[spec, tile_spec], out_specs=spec, compiler_params=_cparams(("parallel",)),
                          name="residual_add")(a, b_tiles)


def _score_bound(gq, gk):
    return (HEAD_DIM * ATTN_SCALE * LOG2E) * jnp.max(jnp.abs(gq)) * jnp.max(jnp.abs(gk))


def _alibi_tables(shift):
    slopes = (2.0 ** (-8.0 * np.arange(1, N_HEADS_A + 1) / N_HEADS_A)).astype(np.float32)
    tables = []
    for _, dil in DILATED_PATTERNS:
        if dil == DILATED_PATTERNS[-1][1]:
            offs, kw = (0,), QBLK
        else:
            offs, kw = (0, DIL_RADIUS, 2 * DIL_RADIUS), 2 * QBLK
        tiles = []
        for off in offs:
            rel = np.abs((off + np.arange(QBLK))[:, None] - np.arange(kw)[None, :])
            tiles.append(np.where(rel <= DIL_RADIUS, rel * dil, np.inf).astype(np.float32))
        dist = np.stack(tiles)
        tab = -(slopes * LOG2E)[:, None, None, None] * dist[None]
        tab = jnp.asarray(tab) - shift
        tables.append(tab[:, 0] if len(offs) == 1 else tab)
    return tables


def _nb_window_start(g, rows_total):
    kh = min(NA_ROWS, rows_total)
    return int(np.clip(g * NB_GROUP - kh // 2, 0, rows_total - (NB_GROUP + kh)))


def _rpb_table(rpb, shift, rows_total):
    kh = min(NA_ROWS, rows_total)
    n_groups = rows_total // NB_GROUP
    win = NB_GROUP + kh
    assert n_groups >= 3 and all(_nb_window_start(g, rows_total) == g * NB_GROUP - kh // 2 for g in range(1, n_groups - 1))
    c = np.arange(GRID_W)
    c0 = np.clip(c - NA_COLS // 2, 0, GRID_W - NA_COLS)
    col_ok = (c[None, :] >= c0[:, None]) & (c[None, :] < c0[:, None] + NA_COLS)
    dc = np.clip(c[None, :] - c[:, None], -(NA_COLS - 1), NA_COLS - 1) + NA_COLS - 1
    n_dc = 2 * NA_COLS - 1
    pick = dc[None, None, :, :, None] == np.arange(n_dc)[None, None, None, None, :]
    cols = jnp.sum(jnp.where(pick, rpb[:, :, None, None, :], 0.0), axis=-1)
    cols = jnp.where(col_ok[None, None], cols, -jnp.inf)
    masked = jnp.full((rpb.shape[0], GRID_W, GRID_W), -jnp.inf, F32)
    types = []
    for g in (0, 1, n_groups - 1):
        ws = _nb_window_start(g, rows_total)
        q_rows = []
        for i in range(NB_GROUP):
            r = g * NB_GROUP + i
            r0 = int(np.clip(r - kh // 2, 0, rows_total - kh))
            blocks = [cols[:, ws + j - r + NA_ROWS - 1] if r0 <= ws + j < r0 + kh else masked for j in range(win)]
            q_rows.append(jnp.concatenate(blocks, axis=2))
        types.append(jnp.concatenate(q_rows, axis=1))
    tab = jnp.stack(types, axis=1) * LOG2E
    return tab - shift[:, None, None, None]


def kernel(x, mem, norm_mix, w_in, qk_gain, rpb, norm_mem, w_mem_kv, out_gain, w_out, norm_ffn, w_group,
           b_group, w_router, b_router, w_gate, w_up, w_down):
    bsz, seq, d = x.shape
    t = bsz * seq
    depth = w_in.shape[0]
    rows_total = seq // GRID_W

    x2 = x.reshape(t, d)
    wgu_all = jnp.concatenate([w_gate, w_up], axis=-1).astype(BF16).reshape(depth * N_EXPERTS, d, 2 * D_EXPERT)
    wd_all = w_down.astype(BF16).reshape(depth * N_EXPERTS, D_EXPERT, d)
    moe = None
    for l in range(depth):
        gains = jnp.tile(qk_gain[l], (1, 2))
        col_scale = jnp.concatenate([
            jnp.tile(gains[0] * QSCALE, WIDTH_A // LANES), jnp.tile(gains[1], WIDTH_A // LANES), jnp.ones((WIDTH_A,), F32),
            jnp.tile(gains[2] * QSCALE, WIDTH_B // LANES), jnp.tile(gains[3], WIDTH_B // LANES), jnp.ones((WIDTH_B,), F32),
            jnp.tile(gains[4] * QSCALE, WIDTH_M // LANES)])[None, :]
        w_in_l = w_in[l].astype(BF16)
        if moe is None:
            proj = _proj(x2, None, norm_mix[l][None, :], w_in_l, col_scale)
        else:
            x2, proj = _proj(x2, moe, norm_mix[l][None, :], w_in_l, col_scale)

        shift_a = _score_bound(qk_gain[l, 0], qk_gain[l, 1])
        b1, b4, b16 = _alibi_tables(shift_a)
        oa = _attn_a(proj, seq, b1, b4, b16)

        shift_b = _score_bound(qk_gain[l, 2], qk_gain[l, 3]) + LOG2E * jnp.max(rpb[l], axis=(1, 2))
        ob = _attn_b(proj, seq, _rpb_table(rpb[l], shift_b, rows_total))

        kaug, vaug = _memkv(mem, norm_mem[l][None, :], w_mem_kv[l].astype(BF16), gains[5:6])
        shift_m = jnp.full((1, mem.shape[1]), _score_bound(qk_gain[l, 4], qk_gain[l, 5]), F32)
        om = _attn_m(proj, seq, kaug, vaug, shift_m)

        w_route = jnp.zeros((d, LANES), F32).at[:, :N_EXPERTS].set(w_router[l])
        w_route = w_route.at[:, N_EXPERTS:N_EXPERTS + N_GROUPS].set(w_group[l])
        w_route_hi = w_route.astype(BF16)
        w_route_lo = (w_route - w_route_hi.astype(F32)).astype(BF16)
        b_route = jnp.zeros((1, LANES), F32).at[0, :N_EXPERTS].set(b_router[l])
        b_route = b_route.at[0, N_EXPERTS:N_EXPERTS + N_GROUPS].set(b_group[l])
        x2, h3, route = _mix(oa, ob, om, x2,
                             out_gain[l][None, :], w_out[l].astype(BF16), norm_ffn[l][None, :],
                             jnp.stack([w_route_hi, w_route_lo]), b_route)

        moe = _moe(h3, route, wgu_all, wd_all, l * N_EXPERTS, MOE_BLOCK_TOKENS)
    return _add(x2, moe).reshape(bsz, seq, d)
```

```python
import functools

import numpy as np
import jax
import jax.numpy as jnp
from jax import lax
from jax.experimental import pallas as pl
from jax.experimental.pallas import tpu as pltpu

F32 = jnp.float32
BF16 = jnp.bfloat16

D_MODEL = 1024
HEAD_DIM = 64
N_HEADS_A = 6
N_HEADS_B = 6
N_HEADS_M = 4
WIDTH_A = N_HEADS_A * HEAD_DIM
WIDTH_B = N_HEADS_B * HEAD_DIM
WIDTH_M = N_HEADS_M * HEAD_DIM
IN_WIDTH = 3 * WIDTH_A + 3 * WIDTH_B + WIDTH_M
DILATED_PATTERNS = ((128, 1), (512, 4), (2048, 16))
DIL_RADIUS = 64
GRID_W = 64
NA_ROWS = 8
NA_COLS = 16
N_GROUPS = 4
EXPERTS_PER_GROUP = 8
N_EXPERTS = N_GROUPS * EXPERTS_PER_GROUP
D_EXPERT = 512
EPS = 1e-6
ATTN_SCALE = HEAD_DIM ** -0.5
LOG2E = float(np.log2(np.e))
QSCALE = ATTN_SCALE * LOG2E

LANES = 128
SUBLANES = 8
ROW_TILES = D_MODEL // LANES
QBLK = 128
PROJ_ROWS = 512
ROUTE_ROWS = 8
MOE_TM = 128
MOE_BLOCK_TOKENS = 4096
VMEM_LIMIT = 56 * 1024 * 1024


def _cparams(sem):
    return pltpu.CompilerParams(dimension_semantics=sem, vmem_limit_bytes=VMEM_LIMIT)


def _from_token_tiles(t_ref):
    rows = t_ref.shape[0] // ROW_TILES
    return jnp.concatenate([t_ref[pl.ds(j, rows, stride=ROW_TILES), :] for j in range(ROW_TILES)], axis=1)


def _qk_slabs():
    a, b, m = WIDTH_A // LANES, WIDTH_B // LANES, WIDTH_M // LANES
    return tuple(range(0, 2 * a)) + tuple(range(3 * a, 3 * a + 2 * b)) + tuple(range(3 * (a + b), 3 * (a + b) + m))


def _proj_kernel(*refs, add):
    if add:
        x_ref, m_ref, g_ref, w_ref, cs_ref, xo_ref, o_ref = refs
        x = x_ref[...] + _from_token_tiles(m_ref)
        xo_ref[...] = x
    else:
        x_ref, g_ref, w_ref, cs_ref, o_ref = refs
        x = x_ref[...]
    ms = jnp.mean(x * x, axis=-1, keepdims=True)
    h = (x * lax.rsqrt(ms + EPS) * g_ref[...]).astype(BF16)
    acc = jnp.dot(h, w_ref[...], preferred_element_type=F32)
    lo = lax.broadcasted_iota(jnp.int32, (x.shape[0], LANES), 1) < HEAD_DIM
    qk = _qk_slabs()
    for s in range(IN_WIDTH // LANES):
        cols = slice(s * LANES, (s + 1) * LANES)
        y = acc[:, cols]
        if s in qk:
            sq = y * y
            ss_lo = jnp.sum(jnp.where(lo, sq, 0.0), axis=-1, keepdims=True)
            ss_hi = jnp.sum(jnp.where(lo, 0.0, sq), axis=-1, keepdims=True)
            inv = jnp.where(lo, lax.rsqrt(ss_lo * (1.0 / HEAD_DIM) + EPS), lax.rsqrt(ss_hi * (1.0 / HEAD_DIM) + EPS))
            y = y * inv * cs_ref[:, cols]
        o_ref[:, cols] = y.astype(BF16)


def _proj(x2, moe2, gain, w, col_scale):
    t = x2.shape[0]
    add = moe2 is not None
    row_spec = pl.BlockSpec((PROJ_ROWS, D_MODEL), lambda i: (i, 0))
    tile_spec = pl.BlockSpec((PROJ_ROWS * ROW_TILES, LANES), lambda i: (i, 0))
    in_specs = [row_spec] + ([tile_spec] if add else []) + [
        pl.BlockSpec((1, D_MODEL), lambda i: (0, 0)),
        pl.BlockSpec((D_MODEL, IN_WIDTH), lambda i: (0, 0)),
        pl.BlockSpec((1, IN_WIDTH), lambda i: (0, 0)),
    ]
    proj_spec = pl.BlockSpec((PROJ_ROWS, IN_WIDTH), lambda i: (i, 0))
    proj_shape = jax.ShapeDtypeStruct((t, IN_WIDTH), BF16)
    if add:
        out_shape = (jax.ShapeDtypeStruct((t, D_MODEL), F32), proj_shape)
        out_specs = (row_spec, proj_spec)
        args = (x2, moe2, gain, w, col_scale)
    else:
        out_shape, out_specs, args = proj_shape, proj_spec, (x2, gain, w, col_scale)
    return pl.pallas_call(
        functools.partial(_proj_kernel, add=add),
        out_shape=out_shape, grid=(t // PROJ_ROWS,), in_specs=in_specs, out_specs=out_specs,
        compiler_params=_cparams(("parallel",)), name="proj_add" if add else "proj",
    )(*args)


def _data_mask(lane, par):
    return (lane < HEAD_DIM) if par == 0 else (lane >= HEAD_DIM)


def _with_ones(v, dm):
    return jnp.where(dm, v, 1.0)


def _head_norm(x, dm, gain):
    ss = jnp.sum(jnp.where(dm, x * x, 0.0), axis=-1, keepdims=True)
    return x * lax.rsqrt(ss * (1.0 / HEAD_DIM) + EPS) * gain


def _normalise(tot):
    return tot / pltpu.roll(tot, HEAD_DIM, axis=1)


def _memkv_kernel(m_ref, g_ref, w_ref, kg_ref, k_ref, v_ref):
    x = m_ref[...]
    ms = jnp.mean(x * x, axis=-1, keepdims=True)
    h = (x * lax.rsqrt(ms + EPS) * g_ref[...]).astype(BF16)
    kv = jnp.dot(h, w_ref[...], preferred_element_type=F32)
    lane = lax.broadcasted_iota(jnp.int32, (x.shape[0], LANES), 1)
    for slab in range(WIDTH_M // LANES):
        kf = kv[:, slab * LANES:(slab + 1) * LANES]
        vf = kv[:, WIDTH_M + slab * LANES:WIDTH_M + (slab + 1) * LANES]
        for par in (0, 1):
            dm = _data_mask(lane, par)
            kn = _head_norm(kf, dm, kg_ref[...])
            k_ref[2 * slab + par] = jnp.where(dm, kn, 0.0).astype(BF16)
            v_ref[2 * slab + par] = _with_ones(vf, dm).astype(BF16)


def _memkv(mem, gain, w, kgain):
    b, m, _ = mem.shape
    kv_shape = jax.ShapeDtypeStruct((b, N_HEADS_M, m, LANES), BF16)
    kv_spec = pl.BlockSpec((None, N_HEADS_M, m, LANES), lambda i: (i, 0, 0, 0))
    return pl.pallas_call(
        _memkv_kernel, out_shape=(kv_shape, kv_shape), grid=(b,),
        in_specs=[pl.BlockSpec((None, m, D_MODEL), lambda i: (i, 0, 0)),
                  pl.BlockSpec((1, D_MODEL), lambda i: (0, 0)),
                  pl.BlockSpec((D_MODEL, 2 * WIDTH_M), lambda i: (0, 0)),
                  pl.BlockSpec((1, LANES), lambda i: (0, 0))],
        out_specs=(kv_spec, kv_spec), compiler_params=_cparams(("parallel",)), name="memkv",
    )(mem, gain, w, kgain)


PREP_ROWS = 256
A_UNROLL = 16


def _attn_a_kernel(q_ref, k_ref, v_ref, b1_ref, b4_ref, b16_ref, o_ref,
                   qs, ks, vs, qm, km, vm, q4, q16, kd, vd, accd, accm, tot, s_scr):
    s_len = q_ref.shape[0]
    n_prep = s_len // PREP_ROWS
    lane_p = lax.broadcasted_iota(jnp.int32, (PREP_ROWS, LANES), 1)
    d_mid, d_far = DILATED_PATTERNS[1][1], DILATED_PATTERNS[2][1]
    len_mid, len_far = s_len // d_mid, s_len // d_far
    sub = d_far // d_mid

    def chunk_rows(c):
        return pl.ds(pl.multiple_of(c * PREP_ROWS, PREP_ROWS), PREP_ROWS)

    def mid_rows(r):
        return pl.ds(r, len_mid, stride=d_mid), pl.ds(pl.multiple_of(r * len_mid, QBLK), len_mid)

    def far_rows(c):
        return (pl.ds((c // sub) * len_mid + c % sub, len_far, stride=sub),
                pl.ds(pl.multiple_of(c * len_far, QBLK), len_far))

    def split_mid(pairs):
        def body(r, carry):
            tok, mid = mid_rows(r)
            for x_tok, x_mid, x_bf in pairs:
                x = x_tok[tok, :]
                x_mid[mid, :] = x
                x_bf[mid, :] = x.astype(BF16)
            return carry
        lax.fori_loop(0, d_mid, body, 0)

    def split_far(pairs):
        def body(c, carry):
            mid, far = far_rows(c)
            for x_mid, x_bf in pairs:
                x_bf[far, :] = x_mid[mid, :].astype(BF16)
            return carry
        lax.fori_loop(0, d_far, body, 0)

    def q_cast(c, carry):
        qs[chunk_rows(c), :] = q_ref[chunk_rows(c), :].astype(F32)
        return carry
    lax.fori_loop(0, n_prep, q_cast, 0)
    split_mid(((qs, qm, q4),))
    split_far(((qm, q16),))

    for par in (0, 1):
        def prep(c, carry, par=par):
            rows = chunk_rows(c)
            dm = _data_mask(lane_p, par)
            kf = jnp.where(dm, k_ref[rows, :].astype(F32), 0.0)
            vf = _with_ones(v_ref[rows, :].astype(F32), dm)
            ks[rows, :] = kf
            vs[rows, :] = vf
            kd[rows, :] = kf.astype(BF16)
            vd[rows, :] = vf.astype(BF16)
            return carry
        lax.fori_loop(0, n_prep, prep, 0)

        def run_blocks(q_src, length, b_ref, out, par=par):
            nblk = length // QBLK

            def blocks(o, carry):
                units = []
                for i in range(A_UNROLL):
                    u = o * A_UNROLL + i
                    if nblk == 1:
                        base = pl.multiple_of(u * length, QBLK)
                        qrows = pl.ds(base, QBLK)
                        keys = pl.ds(base, QBLK)
                        bias = b_ref[par]
                        kw = QBLK
                    else:
                        r = u // nblk
                        blk = u % nblk
                        q0 = blk * QBLK
                        ws = jnp.clip(q0 - DIL_RADIUS, 0, length - 2 * QBLK)
                        tid = jnp.where(blk == 0, 0, jnp.where(blk == nblk - 1, 2, 1))
                        qrows = pl.ds(pl.multiple_of(r * length + q0, QBLK), QBLK)
                        keys = pl.ds(pl.multiple_of(r * length + ws, DIL_RADIUS), 2 * QBLK)
                        bias = b_ref[par, tid]
                        kw = 2 * QBLK
                    s_scr[i, :, 0:kw] = lax.dot_general(q_src[qrows, :], kd[keys, :], (((1,), (1,)), ((), ())),
                                                        preferred_element_type=F32)
                    units.append((qrows, keys, bias, kw))
                for i, (qrows, keys, bias, kw) in enumerate(units):
                    p = jnp.exp2(s_scr[i, :, 0:kw] + bias).astype(BF16)
                    out[qrows, :] = jnp.dot(p, vd[keys, :], preferred_element_type=F32)
                return carry
            lax.fori_loop(0, s_len // QBLK // A_UNROLL, blocks, 0)

        run_blocks(q_ref, s_len, b1_ref, accd)

        def first(c, carry, par=par):
            tot[par, chunk_rows(c), :] = accd[chunk_rows(c), :]
            return carry
        lax.fori_loop(0, n_prep, first, 0)

        split_mid(((ks, km, kd), (vs, vm, vd)))
        run_blocks(q4, len_mid, b4_ref, accm)
        split_far(((km, kd), (vm, vd)))
        run_blocks(q16, len_far, b16_ref, accd)

        def merge_far(c, carry):
            mid, far = far_rows(c)
            accm[mid, :] = accm[mid, :] + accd[far, :]
            return carry
        lax.fori_loop(0, d_far, merge_far, 0)

        def merge_mid(r, carry, par=par):
            tok, mid = mid_rows(r)
            tot[par, tok, :] = tot[par, tok, :] + accm[mid, :]
            return carry
        lax.fori_loop(0, d_mid, merge_mid, 0)

    def fin(c, carry):
        rows = chunk_rows(c)
        o0 = _normalise(tot[0, rows, :])
        o1 = _normalise(tot[1, rows, :])
        o_ref[rows, :] = jnp.where(lane_p < HEAD_DIM, o0, o1).astype(BF16)
        return carry
    lax.fori_loop(0, n_prep, fin, 0)


def _attn_a(proj, s, b1, b4, b16):
    b = proj.shape[0] // s
    n_pairs = WIDTH_A // LANES

    def slab(off):
        return pl.BlockSpec((s, LANES), lambda i, p: (i, off + p))
    return pl.pallas_call(
        _attn_a_kernel, out_shape=jax.ShapeDtypeStruct((b * s, WIDTH_A), BF16), grid=(b, n_pairs),
        in_specs=[slab(0), slab(n_pairs), slab(2 * n_pairs),
                  pl.BlockSpec((2, 3, QBLK, 2 * QBLK), lambda i, p: (p, 0, 0, 0)),
                  pl.BlockSpec((2, 3, QBLK, 2 * QBLK), lambda i, p: (p, 0, 0, 0)),
                  pl.BlockSpec((2, QBLK, QBLK), lambda i, p: (p, 0, 0))],
        out_specs=pl.BlockSpec((s, LANES), lambda i, p: (i, p)),
        scratch_shapes=[pltpu.VMEM((s, LANES), F32)] * 6 + [pltpu.VMEM((s, LANES), BF16)] * 4
        + [pltpu.VMEM((s, LANES), F32)] * 2
        + [pltpu.VMEM((2, s, LANES), F32), pltpu.VMEM((A_UNROLL, QBLK, 2 * QBLK), F32)],
        compiler_params=_cparams(("parallel", "parallel")), name="attn_a",
    )(proj, proj, proj, b1, b4, b16)


NB_GROUP = 4
NB_UNROLL = 2


def _attn_b_kernel(q_ref, k_ref, v_ref, bias_ref, o_ref, kb, vb, s_scr):
    s_len = q_ref.shape[0]
    n_prep = s_len // PREP_ROWS
    rows_total = s_len // GRID_W
    kh = min(NA_ROWS, rows_total)
    n_groups = rows_total // NB_GROUP
    q_len = NB_GROUP * GRID_W
    k_len = (NB_GROUP + kh) * GRID_W
    lane_p = lax.broadcasted_iota(jnp.int32, (PREP_ROWS, LANES), 1)
    lane_q = lax.broadcasted_iota(jnp.int32, (q_len, LANES), 1)

    def prep(c, carry):
        rows = pl.ds(pl.multiple_of(c * PREP_ROWS, PREP_ROWS), PREP_ROWS)
        kf = k_ref[rows, :].astype(F32)
        vf = v_ref[rows, :].astype(F32)
        for par in (0, 1):
            dm = _data_mask(lane_p, par)
            kb[par, rows, :] = jnp.where(dm, kf, 0.0).astype(BF16)
            vb[par, rows, :] = _with_ones(vf, dm).astype(BF16)
        return carry
    lax.fori_loop(0, n_prep, prep, 0)

    def groups(o, carry):
        units = []
        for i in range(NB_UNROLL):
            g = o * NB_UNROLL + i
            ws = jnp.clip(g * NB_GROUP - kh // 2, 0, rows_total - (NB_GROUP + kh))
            typ = jnp.where(g == 0, 0, jnp.where(g == n_groups - 1, 2, 1))
            qrows = pl.ds(pl.multiple_of(g * q_len, q_len), q_len)
            keys = pl.ds(pl.multiple_of(ws * GRID_W, GRID_W), k_len)
            for par in (0, 1):
                s_scr[2 * i + par] = lax.dot_general(q_ref[qrows, :], kb[par, keys, :], (((1,), (1,)), ((), ())),
                                                     preferred_element_type=F32)
            units.append((qrows, keys, typ))
        for i, (qrows, keys, typ) in enumerate(units):
            outs = []
            for par in (0, 1):
                p = jnp.exp2(s_scr[2 * i + par] + bias_ref[par, typ]).astype(BF16)
                acc = jnp.dot(p, vb[par, keys, :], preferred_element_type=F32)
                outs.append(_normalise(acc))
            o_ref[qrows, :] = jnp.where(lane_q < HEAD_DIM, outs[0], outs[1]).astype(BF16)
        return carry
    lax.fori_loop(0, n_groups // NB_UNROLL, groups, 0)


def _attn_b(proj, s, bias):
    b = proj.shape[0] // s
    n_pairs = WIDTH_B // LANES
    first = 3 * WIDTH_A // LANES
    n_types, q_len, k_len = bias.shape[1:]

    def slab(off):
        return pl.BlockSpec((s, LANES), lambda i, p: (i, first + off + p))
    return pl.pallas_call(
        _attn_b_kernel, out_shape=jax.ShapeDtypeStruct((b * s, WIDTH_B), BF16), grid=(b, n_pairs),
        in_specs=[slab(0), slab(n_pairs), slab(2 * n_pairs),
                  pl.BlockSpec((2, n_types, q_len, k_len), lambda i, p: (p, 0, 0, 0))],
        out_specs=pl.BlockSpec((s, LANES), lambda i, p: (i, p)),
        scratch_shapes=[pltpu.VMEM((2, s, LANES), BF16)] * 2 + [pltpu.VMEM((2 * NB_UNROLL, q_len, k_len), F32)],
        compiler_params=_cparams(("parallel", "parallel")), name="attn_b",
    )(proj, proj, proj, bias)


MEM_QROWS = 256
MEM_UNROLL = 2


def _attn_m_kernel(q_ref, k_ref, v_ref, shift_ref, o_ref, s_scr):
    s_len = q_ref.shape[0]
    lane = lax.broadcasted_iota(jnp.int32, (MEM_QROWS, LANES), 1)

    def chunks(o, carry):
        all_rows = []
        for i in range(MEM_UNROLL):
            rows = pl.ds(pl.multiple_of((o * MEM_UNROLL + i) * MEM_QROWS, MEM_QROWS), MEM_QROWS)
            for par in (0, 1):
                s_scr[2 * i + par] = lax.dot_general(q_ref[rows, :], k_ref[par], (((1,), (1,)), ((), ())),
                                                     preferred_element_type=F32)
            all_rows.append(rows)
        for i, rows in enumerate(all_rows):
            outs = []
            for par in (0, 1):
                p = jnp.exp2(s_scr[2 * i + par] - shift_ref[...]).astype(BF16)
                acc = jnp.dot(p, v_ref[par], preferred_element_type=F32)
                outs.append(_normalise(acc))
            o_ref[rows, :] = jnp.where(lane < HEAD_DIM, outs[0], outs[1]).astype(BF16)
        return carry
    lax.fori_loop(0, s_len // MEM_QROWS // MEM_UNROLL, chunks, 0)


def _attn_m(proj, s, kaug, vaug, shift):
    b = proj.shape[0] // s
    m = kaug.shape[2]
    n_pairs = WIDTH_M // LANES
    first = 3 * (WIDTH_A + WIDTH_B) // LANES
    kv_spec = pl.BlockSpec((None, 2, m, LANES), lambda i, p: (i, p, 0, 0))
    return pl.pallas_call(
        _attn_m_kernel, out_shape=jax.ShapeDtypeStruct((b * s, WIDTH_M), BF16), grid=(b, n_pairs),
        in_specs=[pl.BlockSpec((s, LANES), lambda i, p: (i, first + p)), kv_spec, kv_spec,
                  pl.BlockSpec((1, m), lambda i, p: (0, 0))],
        out_specs=pl.BlockSpec((s, LANES), lambda i, p: (i, p)),
        scratch_shapes=[pltpu.VMEM((2 * MEM_UNROLL, MEM_QROWS, m), F32)],
        compiler_params=_cparams(("parallel", "parallel")), name="attn_m",
    )(proj, kaug, vaug, shift)


def _mix_kernel(oa_ref, ob_ref, om_ref, x_ref, og_ref, wo_ref, gf_ref, wr_ref, br_ref,
                x1_ref, h3_ref, route_ref):
    rows = x_ref.shape[0]
    y = None
    off = 0
    for o_ref in (oa_ref, ob_ref, om_ref):
        width = o_ref.shape[1]
        o = o_ref[...].astype(F32)
        ms = jnp.mean(o * o, axis=-1, keepdims=True)
        mixed = (o * lax.rsqrt(ms + EPS) * og_ref[:, off:off + width]).astype(BF16)
        part = jnp.dot(mixed, wo_ref[off:off + width, :], preferred_element_type=F32)
        y = part if y is None else y + part
        off += width
    x1 = x_ref[...] + y
    x1_ref[...] = x1
    ms = jnp.mean(x1 * x1, axis=-1, keepdims=True)
    h = x1 * lax.rsqrt(ms + EPS) * gf_ref[...]
    for j in range(ROW_TILES):
        h3_ref[pl.ds(j, rows, stride=ROW_TILES), :] = h[:, j * LANES:(j + 1) * LANES]

    h_hi = h.astype(BF16)
    h_lo = (h - h_hi.astype(F32)).astype(BF16)
    logits = (jnp.dot(h_hi, wr_ref[0], preferred_element_type=F32)
              + jnp.dot(h_lo, wr_ref[0], preferred_element_type=F32)
              + jnp.dot(h_hi, wr_ref[1], preferred_element_type=F32)) + br_ref[...]
    lane = lax.broadcasted_iota(jnp.int32, (rows, LANES), 1)
    lane_f = lane.astype(F32)
    neg = -jnp.inf
    far = float(LANES)
    is_g = (lane >> 2) == (N_EXPERTS // N_GROUPS)
    gl = jnp.where(is_g, logits, neg)
    gmax = jnp.max(gl, axis=-1, keepdims=True)
    gidx = jnp.min(jnp.where(gl == gmax, lane_f, far), axis=-1, keepdims=True) - float(N_EXPERTS)
    gsum = jnp.sum(jnp.where(is_g, jnp.exp(gl - gmax), 0.0), axis=-1, keepdims=True)
    g_top = 1.0 / gsum
    in_grp = (lane >> 3).astype(F32) == gidx
    el = jnp.where(in_grp, logits, neg)
    v1 = jnp.max(el, axis=-1, keepdims=True)
    i1 = jnp.min(jnp.where(el == v1, lane_f, far), axis=-1, keepdims=True)
    el2 = jnp.where(lane_f == i1, neg, el)
    v2 = jnp.max(el2, axis=-1, keepdims=True)
    i2 = jnp.min(jnp.where(el2 == v2, lane_f, far), axis=-1, keepdims=True)
    ev = jnp.exp(v2 - v1)
    w1 = g_top / (1.0 + ev)
    w2 = g_top * ev / (1.0 + ev)
    route = jnp.where(lane == 0, i1, jnp.where(lane == 1, i2, jnp.where(lane == 2, w1,
                      jnp.where(lane == 3, w2, 0.0))))
    per_token = jnp.transpose(route)[0:4, :]
    chosen = jnp.where(lane_f == i1, 1.0, jnp.where(lane_f == i2, 1.0, 0.0))
    counts = jnp.sum(chosen, axis=0, keepdims=True)
    route_ref[...] = jnp.zeros_like(route_ref)
    route_ref[0:4, :] = per_token
    route_ref[4:5, 0:LANES] = counts


def _mix(oa, ob, om, x2, out_gain, w_out, gain_ffn, w_route, b_route):
    t = x2.shape[0]

    def rows(width):
        return pl.BlockSpec((PROJ_ROWS, width), lambda i: (i, 0))

    def whole(*shape):
        return pl.BlockSpec(shape, lambda i: (0,) * len(shape))
    return pl.pallas_call(
        _mix_kernel,
        out_shape=(jax.ShapeDtypeStruct((t, D_MODEL), F32),
                   jax.ShapeDtypeStruct((t * ROW_TILES, LANES), F32),
                   jax.ShapeDtypeStruct((t // PROJ_ROWS * ROUTE_ROWS, PROJ_ROWS), F32)),
        grid=(t // PROJ_ROWS,),
        in_specs=[rows(WIDTH_A), rows(WIDTH_B), rows(WIDTH_M), rows(D_MODEL),
                  whole(1, D_MODEL), whole(D_MODEL, D_MODEL), whole(1, D_MODEL),
                  whole(2, D_MODEL, LANES), whole(1, LANES)],
        out_specs=(rows(D_MODEL), pl.BlockSpec((PROJ_ROWS * ROW_TILES, LANES), lambda i: (i, 0)),
                   pl.BlockSpec((ROUTE_ROWS, PROJ_ROWS), lambda i: (i, 0))),
        compiler_params=_cparams(("parallel",)), name="mix",
    )(oa, ob, om, x2, out_gain, w_out, gain_ffn, w_route, b_route)


SCATTER_UNROLL = 8
W_SLOTS = 4
W_CHUNK_ROWS = 512


def _moe_weight_copies(wg_hbm, wu_hbm, wd_hbm, wgu_buf, wd_buf, sems, first_expert, e, slot):
    copies = []
    targets = ((wg_hbm, wgu_buf, pl.ds(0, D_EXPERT)), (wu_hbm, wgu_buf, pl.ds(D_EXPERT, D_EXPERT)),
               (wd_hbm, wd_buf, pl.ds(0, D_MODEL)))
    for hbm, buf, cols in targets:
        for c in range(hbm.shape[1] // W_CHUNK_ROWS):
            rows = pl.ds(c * W_CHUNK_ROWS, W_CHUNK_ROWS)
            copies.append(pltpu.make_async_copy(hbm.at[first_expert + e, rows], buf.at[slot, rows, cols],
                                                sems.at[slot]))
    return copies


def _moe_kernel(rstart_ref, tslot_ref, tfirst_ref, tnext_ref, npair_ref, act_ref,
                src_ref, dst_ref, gate_ref, h3_ref, wg_hbm, wu_hbm, wd_hbm, o_hbm,
                wgu_buf, wd_buf, sems, out_sem, xt0, xt1, lhs0, lhs1, yt0, yt1, acc, *, first_expert):
    b = pl.program_id(0)
    n_rows = h3_ref.shape[0]
    tiles_per_block = 2 * (n_rows // ROW_TILES) // MOE_TM + N_EXPERTS + 2
    pad_rows = src_ref.shape[1] - MOE_TM
    base = b * tiles_per_block
    bufs = ((xt0, lhs0, yt0), (xt1, lhs1, yt1))
    copies = functools.partial(_moe_weight_copies, wg_hbm, wu_hbm, wd_hbm, wgu_buf, wd_buf, sems, first_expert)

    def gather(rs, xt):
        for i in range(MOE_TM):
            off = pl.multiple_of(src_ref[0, rs + i], ROW_TILES)
            xt[i * ROW_TILES:(i + 1) * ROW_TILES, :] = h3_ref[pl.ds(off, ROW_TILES), :]

    def scatter(rs, yt):
        for i0 in range(0, MOE_TM, SCATTER_UNROLL):
            vals = []
            for i in range(i0, i0 + SCATTER_UNROLL):
                rows = pl.ds(pl.multiple_of(dst_ref[0, rs + i], ROW_TILES), ROW_TILES)
                vals.append((rows, acc[rows, :] + gate_ref[0, rs + i] * yt[i * ROW_TILES:(i + 1) * ROW_TILES, :]))
            for rows, val in vals:
                acc[rows, :] = val

    for k in range(W_SLOTS - 1):
        e0 = act_ref[b * (W_SLOTS - 1) + k]

        @pl.when(e0 >= 0)
        def _(e0=e0, k=k):
            for c in copies(e0, k):
                c.start()
    acc[...] = jnp.zeros_like(acc)
    for half, (xt, _, yt) in enumerate(bufs):
        yt[...] = jnp.zeros_like(yt)
        gather(rstart_ref[base + half], xt)

    def pair(p, carry):
        tiles = []
        for half in (0, 1):
            idx = base + 2 * p + half
            slot = tslot_ref[idx]

            @pl.when(tfirst_ref[idx] == 1)
            def _(slot=slot):
                for c in copies(0, slot):
                    c.wait()
            prev_rs = jnp.where(p > 0, rstart_ref[jnp.maximum(idx - 2, 0)], pad_rows)
            tiles.append((prev_rs, rstart_ref[idx + 2], slot, tnext_ref[idx]))

        for (prev_rs, _, _, _), (_, _, yt) in zip(tiles, bufs):
            scatter(prev_rs, yt)
        for xt, lhs, _ in bufs:
            for k in range(ROW_TILES):
                lhs[:, k * LANES:(k + 1) * LANES] = xt[pl.ds(k, MOE_TM, stride=ROW_TILES), :].astype(BF16)
        for (_, next_rs, _, _), (xt, _, _) in zip(tiles, bufs):
            gather(next_rs, xt)
        for (_, _, slot, _), (_, lhs, yt) in zip(tiles, bufs):
            gu = jnp.dot(lhs[...], wgu_buf[slot], preferred_element_type=F32)
            g = gu[:, :D_EXPERT]
            hid = (g * (1.0 / (1.0 + jnp.exp(-g)))) * gu[:, D_EXPERT:]
            y = jnp.dot(hid.astype(BF16), wd_buf[slot], preferred_element_type=F32)
            for k in range(ROW_TILES):
                yt[pl.ds(k, MOE_TM, stride=ROW_TILES), :] = y[:, k * LANES:(k + 1) * LANES]

        for _, _, slot, nxt in tiles:
            @pl.when(nxt >= 0)
            def _(slot=slot, nxt=nxt):
                for c in copies(nxt, (slot + W_SLOTS - 1) % W_SLOTS):
                    c.start()
        return carry
    n_pairs = npair_ref[b]
    lax.fori_loop(0, n_pairs, pair, 0)
    for half, (_, _, yt) in enumerate(bufs):
        scatter(rstart_ref[base + 2 * (n_pairs - 1) + half], yt)

    out_copy = pltpu.make_async_copy(acc.at[pl.ds(0, n_rows)],
                                     o_hbm.at[pl.ds(pl.multiple_of(b * n_rows, ROW_TILES), n_rows)], out_sem)
    out_copy.start()
    out_copy.wait()


def _pick(table, index):
    cols = jnp.arange(table.shape[1], dtype=jnp.int32)
    return jnp.sum(jnp.where(index[:, :, None] == cols[None, None, :], table[:, None, :], 0), axis=2)


def _moe(h3, route_t, wg, wu, wd, first_expert, block_tokens):
    t = h3.shape[0] // ROW_TILES
    nblk = t // block_tokens
    n_assign = 2 * block_tokens
    nt_max = n_assign // MOE_TM + N_EXPERTS + 2

    n_pad = MOE_TM - 1
    stride = 2 * n_assign
    n_list = n_assign + N_EXPERTS * MOE_TM
    rec = route_t.reshape(nblk, block_tokens // PROJ_ROWS, ROUTE_ROWS, PROJ_ROWS)
    e = rec[:, :, 0:2, :].astype(jnp.int32).reshape(nblk, n_assign)
    w = rec[:, :, 2:4, :].reshape(nblk, n_assign)
    cnt = jnp.sum(rec[:, :, 4, :N_EXPERTS], axis=1).astype(jnp.int32)
    experts = jnp.arange(N_EXPERTS, dtype=jnp.int32)
    key = e * stride + jnp.arange(n_assign, dtype=jnp.int32)[None, :]
    pad_key = (experts[:, None] * stride + n_assign + jnp.arange(n_pad, dtype=jnp.int32)[None, :]).reshape(1, -1)
    tail_key = (N_EXPERTS * stride + n_assign + experts)[None, :]
    extra_key = jnp.broadcast_to(jnp.concatenate([pad_key, tail_key], axis=1), (nblk, n_list - n_assign))
    key_s, w_s = lax.sort((jnp.concatenate([key, extra_key], axis=1),
                           jnp.concatenate([w, jnp.zeros((nblk, n_list - n_assign), F32)], axis=1)),
                          dimension=1, num_keys=1)
    idx_s = key_s & (stride - 1)
    real = idx_s < n_assign
    row_off = ((idx_s // (2 * PROJ_ROWS)) * PROJ_ROWS + idx_s % PROJ_ROWS) * ROW_TILES
    src_s = jnp.where(real, row_off, 0)
    dst_s = jnp.where(real, row_off, block_tokens * ROW_TILES)
    cstart = jnp.cumsum(cnt, axis=1) - cnt + experts[None, :] * n_pad
    ntile_e = (cnt + MOE_TM - 1) // MOE_TM
    tend = jnp.cumsum(ntile_e, axis=1)
    ntiles = tend[:, -1:]
    active = (ntile_e > 0).astype(jnp.int32)
    order_e = jnp.cumsum(active, axis=1) - active
    ranks = jnp.arange(N_EXPERTS + W_SLOTS, dtype=jnp.int32)
    hit = (order_e[:, None, :] == ranks[None, :, None]) & (active[:, None, :] == 1)
    act = jnp.sum(jnp.where(hit, experts[None, None, :] + 1, 0), axis=2) - 1
    tile_ids = jnp.arange(nt_max, dtype=jnp.int32)[None, :]
    live = tile_ids < ntiles
    texp = jnp.minimum(jnp.sum((tile_ids[:, :, None] >= tend[:, None, :]).astype(jnp.int32), axis=2), N_EXPERTS - 1)
    texp = jnp.where(live, texp, _pick(texp, jnp.maximum(ntiles - 1, 0)))
    within = tile_ids - _pick(tend - ntile_e, texp)
    rstart = jnp.where(live, _pick(cstart, texp) + within * MOE_TM, n_list - MOE_TM)
    torder = _pick(order_e, texp)
    tfirst = (live & (within == 0)).astype(jnp.int32)
    tnext = jnp.where(tfirst == 1, _pick(act, torder + (W_SLOTS - 1)), -1)
    npairs = (ntiles[:, 0] + 1) // 2

    def flat(x):
        return x.reshape(-1).astype(jnp.int32)

    def smem_list():
        return pl.BlockSpec((None, 1, n_list), lambda b, *_: (b, 0, 0), memory_space=pltpu.SMEM)
    tile_buf = pltpu.VMEM((MOE_TM * ROW_TILES, LANES), F32)
    grid_spec = pltpu.PrefetchScalarGridSpec(
        num_scalar_prefetch=6, grid=(nblk,),
        in_specs=[smem_list(), smem_list(), smem_list(),
                  pl.BlockSpec((block_tokens * ROW_TILES, LANES), lambda b, *_: (b, 0),
                               pipeline_mode=pl.Buffered(1)),
                  pl.BlockSpec(memory_space=pl.ANY), pl.BlockSpec(memory_space=pl.ANY),
                  pl.BlockSpec(memory_space=pl.ANY)],
        out_specs=pl.BlockSpec(memory_space=pl.ANY),
        scratch_shapes=[pltpu.VMEM((W_SLOTS, D_MODEL, 2 * D_EXPERT), BF16),
                        pltpu.VMEM((W_SLOTS, D_EXPERT, D_MODEL), BF16),
                        pltpu.SemaphoreType.DMA((W_SLOTS,)), pltpu.SemaphoreType.DMA(()),
                        tile_buf, tile_buf,
                        pltpu.VMEM((MOE_TM, D_MODEL), BF16), pltpu.VMEM((MOE_TM, D_MODEL), BF16),
                        tile_buf, tile_buf,
                        pltpu.VMEM(((block_tokens + 1) * ROW_TILES, LANES), F32)])
    return pl.pallas_call(
        functools.partial(_moe_kernel, first_expert=first_expert),
        out_shape=jax.ShapeDtypeStruct((t * ROW_TILES, LANES), F32), grid_spec=grid_spec,
        compiler_params=_cparams(("arbitrary",)), name="moe",
    )(flat(rstart), flat(torder % W_SLOTS), flat(tfirst), flat(tnext), flat(npairs), flat(act[:, :W_SLOTS - 1]),
      src_s.reshape(nblk, 1, n_list), dst_s.reshape(nblk, 1, n_list), w_s.reshape(nblk, 1, n_list), h3, wg, wu, wd)


def _add_kernel(a_ref, b_ref, o_ref):
    o_ref[...] = a_ref[...] + _from_token_tiles(b_ref)


def _add(a, b_tiles):
    t = a.shape[0]
    spec = pl.BlockSpec((PROJ_ROWS, D_MODEL), lambda i: (i, 0))
    tile_spec = pl.BlockSpec((PROJ_ROWS * ROW_TILES, LANES), lambda i: (i, 0))
    return pl.pallas_call(_add_kernel, out_shape=jax.ShapeDtypeStruct(a.shape, a.dtype), grid=(t // PROJ_ROWS,),
                          in_specs=[spec, tile_spec], out_specs=spec, compiler_params=_cparams(("parallel",)),
                          name="residual_add")(a, b_tiles)


def _score_bound(gq, gk):
    return (HEAD_DIM * ATTN_SCALE * LOG2E) * jnp.max(jnp.abs(gq)) * jnp.max(jnp.abs(gk))


def _alibi_tables(shift):
    slopes = (2.0 ** (-8.0 * np.arange(1, N_HEADS_A + 1) / N_HEADS_A)).astype(np.float32)
    tables = []
    for _, dil in DILATED_PATTERNS:
        if dil == DILATED_PATTERNS[-1][1]:
            offs, kw = (0,), QBLK
        else:
            offs, kw = (0, DIL_RADIUS, 2 * DIL_RADIUS), 2 * QBLK
        tiles = []
        for off in offs:
            rel = np.abs((off + np.arange(QBLK))[:, None] - np.arange(kw)[None, :])
            tiles.append(np.where(rel <= DIL_RADIUS, rel * dil, np.inf).astype(np.float32))
        dist = np.stack(tiles)
        tab = -(slopes * LOG2E)[:, None, None, None] * dist[None]
        tab = jnp.asarray(tab) - shift
        tables.append(tab[:, 0] if len(offs) == 1 else tab)
    return tables


def _nb_window_start(g, rows_total):
    kh = min(NA_ROWS, rows_total)
    return int(np.clip(g * NB_GROUP - kh // 2, 0, rows_total - (NB_GROUP + kh)))


def _rpb_table(rpb, shift, rows_total):
    kh = min(NA_ROWS, rows_total)
    n_groups = rows_total // NB_GROUP
    win = NB_GROUP + kh
    assert n_groups >= 3 and all(_nb_window_start(g, rows_total) == g * NB_GROUP - kh // 2 for g in range(1, n_groups - 1))
    c = np.arange(GRID_W)
    c0 = np.clip(c - NA_COLS // 2, 0, GRID_W - NA_COLS)
    col_ok = (c[None, :] >= c0[:, None]) & (c[None, :] < c0[:, None] + NA_COLS)
    dc = np.clip(c[None, :] - c[:, None], -(NA_COLS - 1), NA_COLS - 1) + NA_COLS - 1
    n_dc = 2 * NA_COLS - 1
    pick = dc[None, None, :, :, None] == np.arange(n_dc)[None, None, None, None, :]
    cols = jnp.sum(jnp.where(pick, rpb[:, :, None, None, :], 0.0), axis=-1)
    cols = jnp.where(col_ok[None, None], cols, -jnp.inf)
    masked = jnp.full((rpb.shape[0], GRID_W, GRID_W), -jnp.inf, F32)
    types = []
    for g in (0, 1, n_groups - 1):
        ws = _nb_window_start(g, rows_total)
        q_rows = []
        for i in range(NB_GROUP):
            r = g * NB_GROUP + i
            r0 = int(np.clip(r - kh // 2, 0, rows_total - kh))
            blocks = [cols[:, ws + j - r + NA_ROWS - 1] if r0 <= ws + j < r0 + kh else masked for j in range(win)]
            q_rows.append(jnp.concatenate(blocks, axis=2))
        types.append(jnp.concatenate(q_rows, axis=1))
    tab = jnp.stack(types, axis=1) * LOG2E
    return tab - shift[:, None, None, None]


def kernel(x, mem, norm_mix, w_in, qk_gain, rpb, norm_mem, w_mem_kv, out_gain, w_out, norm_ffn, w_group,
           b_group, w_router, b_router, w_gate, w_up, w_down):
    bsz, seq, d = x.shape
    t = bsz * seq
    depth = w_in.shape[0]
    rows_total = seq // GRID_W

    x2 = x.reshape(t, d)
    wg_all = w_gate.astype(BF16).reshape(depth * N_EXPERTS, d, D_EXPERT)
    wu_all = w_up.astype(BF16).reshape(depth * N_EXPERTS, d, D_EXPERT)
    wd_all = w_down.astype(BF16).reshape(depth * N_EXPERTS, D_EXPERT, d)
    moe = None
    for l in range(depth):
        gains = jnp.tile(qk_gain[l], (1, 2))
        col_scale = jnp.concatenate([
            jnp.tile(gains[0] * QSCALE, WIDTH_A // LANES), jnp.tile(gains[1], WIDTH_A // LANES), jnp.ones((WIDTH_A,), F32),
            jnp.tile(gains[2] * QSCALE, WIDTH_B // LANES), jnp.tile(gains[3], WIDTH_B // LANES), jnp.ones((WIDTH_B,), F32),
            jnp.tile(gains[4] * QSCALE, WIDTH_M // LANES)])[None, :]
        w_in_l = w_in[l].astype(BF16)
        if moe is None:
            proj = _proj(x2, None, norm_mix[l][None, :], w_in_l, col_scale)
        else:
            x2, proj = _proj(x2, moe, norm_mix[l][None, :], w_in_l, col_scale)

        shift_a = _score_bound(qk_gain[l, 0], qk_gain[l, 1])
        b1, b4, b16 = _alibi_tables(shift_a)
        oa = _attn_a(proj, seq, b1, b4, b16)

        shift_b = _score_bound(qk_gain[l, 2], qk_gain[l, 3]) + LOG2E * jnp.max(rpb[l], axis=(1, 2))
        ob = _attn_b(proj, seq, _rpb_table(rpb[l], shift_b, rows_total))

        kaug, vaug = _memkv(mem, norm_mem[l][None, :], w_mem_kv[l].astype(BF16), gains[5:6])
        shift_m = jnp.full((1, mem.shape[1]), _score_bound(qk_gain[l, 4], qk_gain[l, 5]), F32)
        om = _attn_m(proj, seq, kaug, vaug, shift_m)

        w_route = jnp.zeros((d, LANES), F32).at[:, :N_EXPERTS].set(w_router[l])
        w_route = w_route.at[:, N_EXPERTS:N_EXPERTS + N_GROUPS].set(w_group[l])
        w_route_hi = w_route.astype(BF16)
        w_route_lo = (w_route - w_route_hi.astype(F32)).astype(BF16)
        b_route = jnp.zeros((1, LANES), F32).at[0, :N_EXPERTS].set(b_router[l])
        b_route = b_route.at[0, N_EXPERTS:N_EXPERTS + N_GROUPS].set(b_group[l])
        x2, h3, route = _mix(oa, ob, om, x2,
                             out_gain[l][None, :], w_out[l].astype(BF16), norm_ffn[l][None, :],
                             jnp.stack([w_route_hi, w_route_lo]), b_route)

        moe = _moe(h3, route, wg_all, wu_all, wd_all, l * N_EXPERTS, MOE_BLOCK_TOKENS)
    return _add(x2, moe).reshape(bsz, seq, d)
```

```python
import functools

import numpy as np
import jax
import jax.numpy as jnp
from jax import lax
from jax.experimental import pallas as pl
from jax.experimental.pallas import tpu as pltpu

F32 = jnp.float32
BF16 = jnp.bfloat16

D_MODEL = 1024
HEAD_DIM = 64
N_HEADS_A = 6
N_HEADS_B = 6
N_HEADS_M = 4
WIDTH_A = N_HEADS_A * HEAD_DIM
WIDTH_B = N_HEADS_B * HEAD_DIM
WIDTH_M = N_HEADS_M * HEAD_DIM
IN_WIDTH = 3 * WIDTH_A + 3 * WIDTH_B + WIDTH_M
DILATED_PATTERNS = ((128, 1), (512, 4), (2048, 16))
DIL_RADIUS = 64
GRID_W = 64
NA_ROWS = 8
NA_COLS = 16
N_GROUPS = 4
EXPERTS_PER_GROUP = 8
N_EXPERTS = N_GROUPS * EXPERTS_PER_GROUP
D_EXPERT = 512
EPS = 1e-6
ATTN_SCALE = HEAD_DIM ** -0.5
LOG2E = float(np.log2(np.e))
QSCALE = ATTN_SCALE * LOG2E

LANES = 128
ROW_TILES = D_MODEL // LANES
QBLK = 128
PROJ_ROWS = 1024
ROUTE_ROWS = 8
MOE_TM = 128
MOE_BLOCK_TOKENS = 4096
VMEM_LIMIT = 56 * 1024 * 1024


def _cparams(sem):
    return pltpu.CompilerParams(dimension_semantics=sem, vmem_limit_bytes=VMEM_LIMIT)


def _from_token_tiles(t_ref):
    rows = t_ref.shape[0] // ROW_TILES
    return jnp.concatenate([t_ref[pl.ds(j, rows, stride=ROW_TILES), :] for j in range(ROW_TILES)], axis=1)


def _qk_slabs():
    a, b, m = WIDTH_A // LANES, WIDTH_B // LANES, WIDTH_M // LANES
    return tuple(range(0, 2 * a)) + tuple(range(3 * a, 3 * a + 2 * b)) + tuple(range(3 * (a + b), 3 * (a + b) + m))


def _proj_kernel(*refs, add):
    if add:
        x_ref, m_ref, g_ref, w_ref, cs_ref, xo_ref, o_ref = refs
        x = x_ref[...] + _from_token_tiles(m_ref)
        xo_ref[...] = x
    else:
        x_ref, g_ref, w_ref, cs_ref, o_ref = refs
        x = x_ref[...]
    ms = jnp.mean(x * x, axis=-1, keepdims=True)
    h = (x * lax.rsqrt(ms + EPS) * g_ref[...]).astype(BF16)
    acc = jnp.dot(h, w_ref[...], preferred_element_type=F32)
    lo = lax.broadcasted_iota(jnp.int32, (x.shape[0], LANES), 1) < HEAD_DIM
    qk = _qk_slabs()
    for s in range(IN_WIDTH // LANES):
        cols = slice(s * LANES, (s + 1) * LANES)
        y = acc[:, cols]
        if s in qk:
            sq = y * y
            ss_lo = jnp.sum(jnp.where(lo, sq, 0.0), axis=-1, keepdims=True)
            ss_hi = jnp.sum(jnp.where(lo, 0.0, sq), axis=-1, keepdims=True)
            inv = jnp.where(lo, lax.rsqrt(ss_lo * (1.0 / HEAD_DIM) + EPS), lax.rsqrt(ss_hi * (1.0 / HEAD_DIM) + EPS))
            y = y * inv * cs_ref[:, cols]
        o_ref[:, cols] = y.astype(BF16)


def _proj(x2, moe2, gain, w, col_scale):
    t = x2.shape[0]
    add = moe2 is not None
    row_spec = pl.BlockSpec((PROJ_ROWS, D_MODEL), lambda i: (i, 0))
    tile_spec = pl.BlockSpec((PROJ_ROWS * ROW_TILES, LANES), lambda i: (i, 0))
    in_specs = [row_spec] + ([tile_spec] if add else []) + [
        pl.BlockSpec((1, D_MODEL), lambda i: (0, 0)),
        pl.BlockSpec((D_MODEL, IN_WIDTH), lambda i: (0, 0)),
        pl.BlockSpec((1, IN_WIDTH), lambda i: (0, 0)),
    ]
    proj_spec = pl.BlockSpec((PROJ_ROWS, IN_WIDTH), lambda i: (i, 0))
    proj_shape = jax.ShapeDtypeStruct((t, IN_WIDTH), BF16)
    if add:
        out_shape = (jax.ShapeDtypeStruct((t, D_MODEL), F32), proj_shape)
        out_specs = (row_spec, proj_spec)
        args = (x2, moe2, gain, w, col_scale)
    else:
        out_shape, out_specs, args = proj_shape, proj_spec, (x2, gain, w, col_scale)
    return pl.pallas_call(
        functools.partial(_proj_kernel, add=add),
        out_shape=out_shape, grid=(t // PROJ_ROWS,), in_specs=in_specs, out_specs=out_specs,
        compiler_params=_cparams(("parallel",)), name="proj_add" if add else "proj",
    )(*args)


def _data_mask(lane, par):
    return (lane < HEAD_DIM) if par == 0 else (lane >= HEAD_DIM)


def _with_ones(v, dm):
    return jnp.where(dm, v, 1.0)


def _head_norm(x, dm, gain):
    ss = jnp.sum(jnp.where(dm, x * x, 0.0), axis=-1, keepdims=True)
    return x * lax.rsqrt(ss * (1.0 / HEAD_DIM) + EPS) * gain


def _normalise(tot):
    return tot / pltpu.roll(tot, HEAD_DIM, axis=1)


def _memkv_kernel(m_ref, g_ref, w_ref, kg_ref, k_ref, v_ref):
    x = m_ref[...]
    ms = jnp.mean(x * x, axis=-1, keepdims=True)
    h = (x * lax.rsqrt(ms + EPS) * g_ref[...]).astype(BF16)
    kv = jnp.dot(h, w_ref[...], preferred_element_type=F32)
    lane = lax.broadcasted_iota(jnp.int32, (x.shape[0], LANES), 1)
    for slab in range(WIDTH_M // LANES):
        kf = kv[:, slab * LANES:(slab + 1) * LANES]
        vf = kv[:, WIDTH_M + slab * LANES:WIDTH_M + (slab + 1) * LANES]
        for par in (0, 1):
            dm = _data_mask(lane, par)
            kn = _head_norm(kf, dm, kg_ref[...])
            k_ref[2 * slab + par] = jnp.where(dm, kn, 0.0).astype(BF16)
            v_ref[2 * slab + par] = _with_ones(vf, dm).astype(BF16)


def _memkv(mem, gain, w, kgain):
    b, m, _ = mem.shape
    kv_shape = jax.ShapeDtypeStruct((b, N_HEADS_M, m, LANES), BF16)
    kv_spec = pl.BlockSpec((None, N_HEADS_M, m, LANES), lambda i: (i, 0, 0, 0))
    return pl.pallas_call(
        _memkv_kernel, out_shape=(kv_shape, kv_shape), grid=(b,),
        in_specs=[pl.BlockSpec((None, m, D_MODEL), lambda i: (i, 0, 0)),
                  pl.BlockSpec((1, D_MODEL), lambda i: (0, 0)),
                  pl.BlockSpec((D_MODEL, 2 * WIDTH_M), lambda i: (0, 0)),
                  pl.BlockSpec((1, LANES), lambda i: (0, 0))],
        out_specs=(kv_spec, kv_spec), compiler_params=_cparams(("parallel",)), name="memkv",
    )(mem, gain, w, kgain)


PREP_ROWS = 256
A_UNROLL = 16


def _attn_a_kernel(q_ref, k_ref, v_ref, b1_ref, b4_ref, b16_ref, o_ref,
                   qs, ks, vs, qm, km, vm, q4, q16, kd, vd, accd, accm, tot, s_scr):
    s_len = q_ref.shape[0]
    n_prep = s_len // PREP_ROWS
    lane_p = lax.broadcasted_iota(jnp.int32, (PREP_ROWS, LANES), 1)
    d_mid, d_far = DILATED_PATTERNS[1][1], DILATED_PATTERNS[2][1]
    len_mid, len_far = s_len // d_mid, s_len // d_far
    sub = d_far // d_mid

    def chunk_rows(c):
        return pl.ds(pl.multiple_of(c * PREP_ROWS, PREP_ROWS), PREP_ROWS)

    def mid_rows(r):
        return pl.ds(r, len_mid, stride=d_mid), pl.ds(pl.multiple_of(r * len_mid, QBLK), len_mid)

    def far_rows(c):
        return (pl.ds((c // sub) * len_mid + c % sub, len_far, stride=sub),
                pl.ds(pl.multiple_of(c * len_far, QBLK), len_far))

    def split_mid(pairs):
        def body(r, carry):
            tok, mid = mid_rows(r)
            for x_tok, x_mid, x_bf in pairs:
                x = x_tok[tok, :]
                x_mid[mid, :] = x
                x_bf[mid, :] = x.astype(BF16)
            return carry
        lax.fori_loop(0, d_mid, body, 0)

    def split_far(pairs):
        def body(c, carry):
            mid, far = far_rows(c)
            for x_mid, x_bf in pairs:
                x_bf[far, :] = x_mid[mid, :].astype(BF16)
            return carry
        lax.fori_loop(0, d_far, body, 0)

    def q_cast(c, carry):
        qs[chunk_rows(c), :] = q_ref[chunk_rows(c), :].astype(F32)
        return carry
    lax.fori_loop(0, n_prep, q_cast, 0)
    split_mid(((qs, qm, q4),))
    split_far(((qm, q16),))

    for par in (0, 1):
        def prep(c, carry, par=par):
            rows = chunk_rows(c)
            dm = _data_mask(lane_p, par)
            kf = jnp.where(dm, k_ref[rows, :].astype(F32), 0.0)
            vf = _with_ones(v_ref[rows, :].astype(F32), dm)
            ks[rows, :] = kf
            vs[rows, :] = vf
            kd[rows, :] = kf.astype(BF16)
            vd[rows, :] = vf.astype(BF16)
            return carry
        lax.fori_loop(0, n_prep, prep, 0)

        def run_blocks(q_src, length, b_ref, out, par=par):
            nblk = length // QBLK

            def blocks(o, carry):
                units = []
                for i in range(A_UNROLL):
                    u = o * A_UNROLL + i
                    if nblk == 1:
                        base = pl.multiple_of(u * length, QBLK)
                        qrows = pl.ds(base, QBLK)
                        keys = pl.ds(base, QBLK)
                        bias = b_ref[par]
                        kw = QBLK
                    else:
                        r = u // nblk
                        blk = u % nblk
                        q0 = blk * QBLK
                        ws = jnp.clip(q0 - DIL_RADIUS, 0, length - 2 * QBLK)
                        tid = jnp.where(blk == 0, 0, jnp.where(blk == nblk - 1, 2, 1))
                        qrows = pl.ds(pl.multiple_of(r * length + q0, QBLK), QBLK)
                        keys = pl.ds(pl.multiple_of(r * length + ws, DIL_RADIUS), 2 * QBLK)
                        bias = b_ref[par, tid]
                        kw = 2 * QBLK
                    s_scr[i, :, 0:kw] = lax.dot_general(q_src[qrows, :], kd[keys, :], (((1,), (1,)), ((), ())),
                                                        preferred_element_type=F32)
                    units.append((qrows, keys, bias, kw))
                for i, (qrows, keys, bias, kw) in enumerate(units):
                    p = jnp.exp2(s_scr[i, :, 0:kw] + bias).astype(BF16)
                    out[qrows, :] = jnp.dot(p, vd[keys, :], preferred_element_type=F32)
                return carry
            lax.fori_loop(0, s_len // QBLK // A_UNROLL, blocks, 0)

        run_blocks(q_ref, s_len, b1_ref, tot.at[par])

        split_mid(((ks, km, kd), (vs, vm, vd)))
        run_blocks(q4, len_mid, b4_ref, accm)
        split_far(((km, kd), (vm, vd)))
        run_blocks(q16, len_far, b16_ref, accd)

        def merge_far(c, carry):
            mid, far = far_rows(c)
            accm[mid, :] = accm[mid, :] + accd[far, :]
            return carry
        lax.fori_loop(0, d_far, merge_far, 0)

        def merge_mid(r, carry, par=par):
            tok, mid = mid_rows(r)
            tot[par, tok, :] = tot[par, tok, :] + accm[mid, :]
            return carry
        lax.fori_loop(0, d_mid, merge_mid, 0)

    def fin(c, carry):
        rows = chunk_rows(c)
        o0 = _normalise(tot[0, rows, :])
        o1 = _normalise(tot[1, rows, :])
        o_ref[rows, :] = jnp.where(lane_p < HEAD_DIM, o0, o1).astype(BF16)
        return carry
    lax.fori_loop(0, n_prep, fin, 0)


def _attn_a(proj, s, b1, b4, b16):
    b = proj.shape[0] // s
    n_pairs = WIDTH_A // LANES

    def slab(off):
        return pl.BlockSpec((s, LANES), lambda i, p: (i, off + p))
    return pl.pallas_call(
        _attn_a_kernel, out_shape=jax.ShapeDtypeStruct((b * s, WIDTH_A), BF16), grid=(b, n_pairs),
        in_specs=[slab(0), slab(n_pairs), slab(2 * n_pairs),
                  pl.BlockSpec((2, 3, QBLK, 2 * QBLK), lambda i, p: (p, 0, 0, 0)),
                  pl.BlockSpec((2, 3, QBLK, 2 * QBLK), lambda i, p: (p, 0, 0, 0)),
                  pl.BlockSpec((2, QBLK, QBLK), lambda i, p: (p, 0, 0))],
        out_specs=pl.BlockSpec((s, LANES), lambda i, p: (i, p)),
        scratch_shapes=[pltpu.VMEM((s, LANES), F32)] * 6 + [pltpu.VMEM((s, LANES), BF16)] * 4
        + [pltpu.VMEM((s, LANES), F32)] * 2
        + [pltpu.VMEM((2, s, LANES), F32), pltpu.VMEM((A_UNROLL, QBLK, 2 * QBLK), F32)],
        compiler_params=_cparams(("parallel", "parallel")), name="attn_a",
    )(proj, proj, proj, b1, b4, b16)


NB_GROUP = 4
NB_UNROLL = 2


def _attn_b_kernel(q_ref, k_ref, v_ref, bias_ref, o_ref, kb, vb, s_scr):
    s_len = q_ref.shape[0]
    n_prep = s_len // PREP_ROWS
    rows_total = s_len // GRID_W
    kh = min(NA_ROWS, rows_total)
    n_groups = rows_total // NB_GROUP
    q_len = NB_GROUP * GRID_W
    k_len = (NB_GROUP + kh) * GRID_W
    lane_p = lax.broadcasted_iota(jnp.int32, (PREP_ROWS, LANES), 1)
    lane_q = lax.broadcasted_iota(jnp.int32, (q_len, LANES), 1)

    def prep(c, carry):
        rows = pl.ds(pl.multiple_of(c * PREP_ROWS, PREP_ROWS), PREP_ROWS)
        kf = k_ref[rows, :].astype(F32)
        vf = v_ref[rows, :].astype(F32)
        for par in (0, 1):
            dm = _data_mask(lane_p, par)
            kb[par, rows, :] = jnp.where(dm, kf, 0.0).astype(BF16)
            vb[par, rows, :] = _with_ones(vf, dm).astype(BF16)
        return carry
    lax.fori_loop(0, n_prep, prep, 0)

    def groups(o, carry):
        units = []
        for i in range(NB_UNROLL):
            g = o * NB_UNROLL + i
            ws = jnp.clip(g * NB_GROUP - kh // 2, 0, rows_total - (NB_GROUP + kh))
            typ = jnp.where(g == 0, 0, jnp.where(g == n_groups - 1, 2, 1))
            qrows = pl.ds(pl.multiple_of(g * q_len, q_len), q_len)
            keys = pl.ds(pl.multiple_of(ws * GRID_W, GRID_W), k_len)
            for par in (0, 1):
                s_scr[2 * i + par] = lax.dot_general(q_ref[qrows, :], kb[par, keys, :], (((1,), (1,)), ((), ())),
                                                     preferred_element_type=F32)
            units.append((qrows, keys, typ))
        for i, (qrows, keys, typ) in enumerate(units):
            outs = []
            for par in (0, 1):
                p = jnp.exp2(s_scr[2 * i + par] + bias_ref[par, typ]).astype(BF16)
                acc = jnp.dot(p, vb[par, keys, :], preferred_element_type=F32)
                outs.append(_normalise(acc))
            o_ref[qrows, :] = jnp.where(lane_q < HEAD_DIM, outs[0], outs[1]).astype(BF16)
        return carry
    lax.fori_loop(0, n_groups // NB_UNROLL, groups, 0)


def _attn_b(proj, s, bias):
    b = proj.shape[0] // s
    n_pairs = WIDTH_B // LANES
    first = 3 * WIDTH_A // LANES
    n_types, q_len, k_len = bias.shape[1:]

    def slab(off):
        return pl.BlockSpec((s, LANES), lambda i, p: (i, first + off + p))
    return pl.pallas_call(
        _attn_b_kernel, out_shape=jax.ShapeDtypeStruct((b * s, WIDTH_B), BF16), grid=(b, n_pairs),
        in_specs=[slab(0), slab(n_pairs), slab(2 * n_pairs),
                  pl.BlockSpec((2, n_types, q_len, k_len), lambda i, p: (p, 0, 0, 0))],
        out_specs=pl.BlockSpec((s, LANES), lambda i, p: (i, p)),
        scratch_shapes=[pltpu.VMEM((2, s, LANES), BF16)] * 2 + [pltpu.VMEM((2 * NB_UNROLL, q_len, k_len), F32)],
        compiler_params=_cparams(("parallel", "parallel")), name="attn_b",
    )(proj, proj, proj, bias)


MEM_QROWS = 256
MEM_UNROLL = 2


def _attn_m_kernel(q_ref, k_ref, v_ref, shift_ref, o_ref, s_scr):
    s_len = q_ref.shape[0]
    lane = lax.broadcasted_iota(jnp.int32, (MEM_QROWS, LANES), 1)

    def chunks(o, carry):
        all_rows = []
        for i in range(MEM_UNROLL):
            rows = pl.ds(pl.multiple_of((o * MEM_UNROLL + i) * MEM_QROWS, MEM_QROWS), MEM_QROWS)
            for par in (0, 1):
                s_scr[2 * i + par] = lax.dot_general(q_ref[rows, :], k_ref[par], (((1,), (1,)), ((), ())),
                                                     preferred_element_type=F32)
            all_rows.append(rows)
        for i, rows in enumerate(all_rows):
            outs = []
            for par in (0, 1):
                p = jnp.exp2(s_scr[2 * i + par] - shift_ref[...]).astype(BF16)
                acc = jnp.dot(p, v_ref[par], preferred_element_type=F32)
                outs.append(_normalise(acc))
            o_ref[rows, :] = jnp.where(lane < HEAD_DIM, outs[0], outs[1]).astype(BF16)
        return carry
    lax.fori_loop(0, s_len // MEM_QROWS // MEM_UNROLL, chunks, 0)


def _attn_m(proj, s, kaug, vaug, shift):
    b = proj.shape[0] // s
    m = kaug.shape[2]
    n_pairs = WIDTH_M // LANES
    first = 3 * (WIDTH_A + WIDTH_B) // LANES
    kv_spec = pl.BlockSpec((None, 2, m, LANES), lambda i, p: (i, p, 0, 0))
    return pl.pallas_call(
        _attn_m_kernel, out_shape=jax.ShapeDtypeStruct((b * s, WIDTH_M), BF16), grid=(b, n_pairs),
        in_specs=[pl.BlockSpec((s, LANES), lambda i, p: (i, first + p)), kv_spec, kv_spec,
                  pl.BlockSpec((1, m), lambda i, p: (0, 0))],
        out_specs=pl.BlockSpec((s, LANES), lambda i, p: (i, p)),
        scratch_shapes=[pltpu.VMEM((2 * MEM_UNROLL, MEM_QROWS, m), F32)],
        compiler_params=_cparams(("parallel", "parallel")), name="attn_m",
    )(proj, kaug, vaug, shift)


def _mix_kernel(oa_ref, ob_ref, om_ref, x_ref, og_ref, wo_ref, gf_ref, wr_ref, br_ref,
                x1_ref, h3_ref, route_ref):
    rows = x_ref.shape[0]
    y = None
    off = 0
    for o_ref in (oa_ref, ob_ref, om_ref):
        width = o_ref.shape[1]
        o = o_ref[...].astype(F32)
        ms = jnp.mean(o * o, axis=-1, keepdims=True)
        mixed = (o * lax.rsqrt(ms + EPS) * og_ref[:, off:off + width]).astype(BF16)
        part = jnp.dot(mixed, wo_ref[off:off + width, :], preferred_element_type=F32)
        y = part if y is None else y + part
        off += width
    x1 = x_ref[...] + y
    x1_ref[...] = x1
    ms = jnp.mean(x1 * x1, axis=-1, keepdims=True)
    h = x1 * lax.rsqrt(ms + EPS) * gf_ref[...]
    for j in range(ROW_TILES):
        h3_ref[pl.ds(j, rows, stride=ROW_TILES), :] = h[:, j * LANES:(j + 1) * LANES]

    h_hi = h.astype(BF16)
    h_lo = (h - h_hi.astype(F32)).astype(BF16)
    logits = (jnp.dot(h_hi, wr_ref[0], preferred_element_type=F32)
              + jnp.dot(h_lo, wr_ref[0], preferred_element_type=F32)
              + jnp.dot(h_hi, wr_ref[1], preferred_element_type=F32)) + br_ref[...]
    lane = lax.broadcasted_iota(jnp.int32, (rows, LANES), 1)
    lane_f = lane.astype(F32)
    neg = -jnp.inf
    far = float(LANES)
    is_g = (lane >> 2) == (N_EXPERTS // N_GROUPS)
    gl = jnp.where(is_g, logits, neg)
    gmax = jnp.max(gl, axis=-1, keepdims=True)
    gidx = jnp.min(jnp.where(gl == gmax, lane_f, far), axis=-1, keepdims=True) - float(N_EXPERTS)
    gsum = jnp.sum(jnp.where(is_g, jnp.exp(gl - gmax), 0.0), axis=-1, keepdims=True)
    g_top = 1.0 / gsum
    in_grp = (lane >> 3).astype(F32) == gidx
    el = jnp.where(in_grp, logits, neg)
    v1 = jnp.max(el, axis=-1, keepdims=True)
    i1 = jnp.min(jnp.where(el == v1, lane_f, far), axis=-1, keepdims=True)
    el2 = jnp.where(lane_f == i1, neg, el)
    v2 = jnp.max(el2, axis=-1, keepdims=True)
    i2 = jnp.min(jnp.where(el2 == v2, lane_f, far), axis=-1, keepdims=True)
    ev = jnp.exp(v2 - v1)
    w1 = g_top / (1.0 + ev)
    w2 = g_top * ev / (1.0 + ev)
    route = jnp.where(lane == 0, i1, jnp.where(lane == 1, i2, jnp.where(lane == 2, w1,
                      jnp.where(lane == 3, w2, 0.0))))
    per_token = jnp.transpose(route)[0:4, :]
    chosen = jnp.where(lane_f == i1, 1.0, jnp.where(lane_f == i2, 1.0, 0.0))
    counts = jnp.sum(chosen, axis=0, keepdims=True)
    route_ref[...] = jnp.zeros_like(route_ref)
    route_ref[0:4, :] = per_token
    route_ref[4:5, 0:LANES] = counts


def _mix(oa, ob, om, x2, out_gain, w_out, gain_ffn, w_route, b_route):
    t = x2.shape[0]

    def rows(width):
        return pl.BlockSpec((PROJ_ROWS, width), lambda i: (i, 0))

    def whole(*shape):
        return pl.BlockSpec(shape, lambda i: (0,) * len(shape))
    return pl.pallas_call(
        _mix_kernel,
        out_shape=(jax.ShapeDtypeStruct((t, D_MODEL), F32),
                   jax.ShapeDtypeStruct((t * ROW_TILES, LANES), F32),
                   jax.ShapeDtypeStruct((t // PROJ_ROWS * ROUTE_ROWS, PROJ_ROWS), F32)),
        grid=(t // PROJ_ROWS,),
        in_specs=[rows(WIDTH_A), rows(WIDTH_B), rows(WIDTH_M), rows(D_MODEL),
                  whole(1, D_MODEL), whole(D_MODEL, D_MODEL), whole(1, D_MODEL),
                  whole(2, D_MODEL, LANES), whole(1, LANES)],
        out_specs=(rows(D_MODEL), pl.BlockSpec((PROJ_ROWS * ROW_TILES, LANES), lambda i: (i, 0)),
                   pl.BlockSpec((ROUTE_ROWS, PROJ_ROWS), lambda i: (i, 0))),
        compiler_params=_cparams(("parallel",)), name="mix",
    )(oa, ob, om, x2, out_gain, w_out, gain_ffn, w_route, b_route)


SCATTER_UNROLL = 8
W_SLOTS = 4
W_CHUNK_ROWS = 512


def _moe_weight_copies(wg_hbm, wu_hbm, wd_hbm, wgu_buf, wd_buf, sems, first_expert, e, slot):
    copies = []
    targets = ((wg_hbm, wgu_buf, pl.ds(0, D_EXPERT)), (wu_hbm, wgu_buf, pl.ds(D_EXPERT, D_EXPERT)),
               (wd_hbm, wd_buf, pl.ds(0, D_MODEL)))
    for hbm, buf, cols in targets:
        for c in range(hbm.shape[1] // W_CHUNK_ROWS):
            rows = pl.ds(c * W_CHUNK_ROWS, W_CHUNK_ROWS)
            copies.append(pltpu.make_async_copy(hbm.at[first_expert + e, rows], buf.at[slot, rows, cols],
                                                sems.at[slot]))
    return copies


def _moe_kernel(rstart_ref, tslot_ref, tfirst_ref, tnext_ref, npair_ref, act_ref,
                src_ref, dst_ref, gate_ref, h3_ref, wg_hbm, wu_hbm, wd_hbm, o_hbm,
                wgu_buf, wd_buf, sems, out_sem, xt0, xt1, lhs0, lhs1, yt0, yt1, acc, *, first_expert):
    b = pl.program_id(0)
    n_rows = h3_ref.shape[0]
    tiles_per_block = 2 * (n_rows // ROW_TILES) // MOE_TM + N_EXPERTS + 2
    pad_rows = src_ref.shape[1] - MOE_TM
    base = b * tiles_per_block
    bufs = ((xt0, lhs0, yt0), (xt1, lhs1, yt1))
    copies = functools.partial(_moe_weight_copies, wg_hbm, wu_hbm, wd_hbm, wgu_buf, wd_buf, sems, first_expert)

    def gather(rs, xt):
        for i in range(MOE_TM):
            off = pl.multiple_of(src_ref[0, rs + i], ROW_TILES)
            xt[i * ROW_TILES:(i + 1) * ROW_TILES, :] = h3_ref[pl.ds(off, ROW_TILES), :]

    def scatter(rs, yt):
        for i0 in range(0, MOE_TM, SCATTER_UNROLL):
            vals = []
            for i in range(i0, i0 + SCATTER_UNROLL):
                rows = pl.ds(pl.multiple_of(dst_ref[0, rs + i], ROW_TILES), ROW_TILES)
                vals.append((rows, acc[rows, :] + gate_ref[0, rs + i] * yt[i * ROW_TILES:(i + 1) * ROW_TILES, :]))
            for rows, val in vals:
                acc[rows, :] = val

    for k in range(W_SLOTS - 1):
        e0 = act_ref[b * (W_SLOTS - 1) + k]

        @pl.when(e0 >= 0)
        def _(e0=e0, k=k):
            for c in copies(e0, k):
                c.start()

    def out_copy(block):
        return pltpu.make_async_copy(acc.at[pl.ds(0, n_rows)],
                                     o_hbm.at[pl.ds(pl.multiple_of(block * n_rows, ROW_TILES), n_rows)], out_sem)

    @pl.when(b > 0)
    def _():
        out_copy(b - 1).wait()
    acc[...] = jnp.zeros_like(acc)
    for half, (xt, _, yt) in enumerate(bufs):
        yt[...] = jnp.zeros_like(yt)
        gather(rstart_ref[base + half], xt)

    def pair(p, carry):
        tiles = []
        for half in (0, 1):
            idx = base + 2 * p + half
            slot = tslot_ref[idx]

            @pl.when(tfirst_ref[idx] == 1)
            def _(slot=slot):
                for c in copies(0, slot):
                    c.wait()
            prev_rs = jnp.where(p > 0, rstart_ref[jnp.maximum(idx - 2, 0)], pad_rows)
            tiles.append((prev_rs, rstart_ref[idx + 2], slot, tnext_ref[idx]))

        for (prev_rs, _, _, _), (_, _, yt) in zip(tiles, bufs):
            scatter(prev_rs, yt)
        for xt, lhs, _ in bufs:
            for k in range(ROW_TILES):
                lhs[:, k * LANES:(k + 1) * LANES] = xt[pl.ds(k, MOE_TM, stride=ROW_TILES), :].astype(BF16)
        for (_, next_rs, _, _), (xt, _, _) in zip(tiles, bufs):
            gather(next_rs, xt)
        for (_, _, slot, _), (_, lhs, yt) in zip(tiles, bufs):
            gu = jnp.dot(lhs[...], wgu_buf[slot], preferred_element_type=F32)
            g = gu[:, :D_EXPERT]
            hid = (g * (1.0 / (1.0 + jnp.exp(-g)))) * gu[:, D_EXPERT:]
            y = jnp.dot(hid.astype(BF16), wd_buf[slot], preferred_element_type=F32)
            for k in range(ROW_TILES):
                yt[pl.ds(k, MOE_TM, stride=ROW_TILES), :] = y[:, k * LANES:(k + 1) * LANES]

        for _, _, slot, nxt in tiles:
            @pl.when(nxt >= 0)
            def _(slot=slot, nxt=nxt):
                for c in copies(nxt, (slot + W_SLOTS - 1) % W_SLOTS):
                    c.start()
        return carry
    n_pairs = npair_ref[b]
    lax.fori_loop(0, n_pairs, pair, 0)
    for half, (_, _, yt) in enumerate(bufs):
        scatter(rstart_ref[base + 2 * (n_pairs - 1) + half], yt)

    out_copy(b).start()

    @pl.when(b == pl.num_programs(0) - 1)
    def _():
        out_copy(b).wait()


def _pick(table, index):
    cols = jnp.arange(table.shape[1], dtype=jnp.int32)
    return jnp.sum(jnp.where(index[:, :, None] == cols[None, None, :], table[:, None, :], 0), axis=2)


def _moe(h3, route_t, wg, wu, wd, first_expert, block_tokens):
    t = h3.shape[0] // ROW_TILES
    nblk = t // block_tokens
    n_assign = 2 * block_tokens
    nt_max = n_assign // MOE_TM + N_EXPERTS + 2

    n_pad = MOE_TM - 1
    stride = 2 * n_assign
    n_list = n_assign + N_EXPERTS * MOE_TM
    rec = route_t.reshape(nblk, block_tokens // PROJ_ROWS, ROUTE_ROWS, PROJ_ROWS)
    e = rec[:, :, 0:2, :].astype(jnp.int32).reshape(nblk, n_assign)
    w = rec[:, :, 2:4, :].reshape(nblk, n_assign)
    cnt = jnp.sum(rec[:, :, 4, :N_EXPERTS], axis=1).astype(jnp.int32)
    experts = jnp.arange(N_EXPERTS, dtype=jnp.int32)
    key = e * stride + jnp.arange(n_assign, dtype=jnp.int32)[None, :]
    pad_key = (experts[:, None] * stride + n_assign + jnp.arange(n_pad, dtype=jnp.int32)[None, :]).reshape(1, -1)
    tail_key = (N_EXPERTS * stride + n_assign + experts)[None, :]
    extra_key = jnp.broadcast_to(jnp.concatenate([pad_key, tail_key], axis=1), (nblk, n_list - n_assign))
    key_s, w_s = lax.sort((jnp.concatenate([key, extra_key], axis=1),
                           jnp.concatenate([w, jnp.zeros((nblk, n_list - n_assign), F32)], axis=1)),
                          dimension=1, num_keys=1)
    idx_s = key_s & (stride - 1)
    real = idx_s < n_assign
    row_off = ((idx_s // (2 * PROJ_ROWS)) * PROJ_ROWS + idx_s % PROJ_ROWS) * ROW_TILES
    src_s = jnp.where(real, row_off, 0)
    dst_s = jnp.where(real, row_off, block_tokens * ROW_TILES)
    cstart = jnp.cumsum(cnt, axis=1) - cnt + experts[None, :] * n_pad
    ntile_e = (cnt + MOE_TM - 1) // MOE_TM
    tend = jnp.cumsum(ntile_e, axis=1)
    ntiles = tend[:, -1:]
    active = (ntile_e > 0).astype(jnp.int32)
    order_e = jnp.cumsum(active, axis=1) - active
    ranks = jnp.arange(N_EXPERTS + W_SLOTS, dtype=jnp.int32)
    hit = (order_e[:, None, :] == ranks[None, :, None]) & (active[:, None, :] == 1)
    act = jnp.sum(jnp.where(hit, experts[None, None, :] + 1, 0), axis=2) - 1
    tile_ids = jnp.arange(nt_max, dtype=jnp.int32)[None, :]
    live = tile_ids < ntiles
    texp = jnp.minimum(jnp.sum((tile_ids[:, :, None] >= tend[:, None, :]).astype(jnp.int32), axis=2), N_EXPERTS - 1)
    texp = jnp.where(live, texp, _pick(texp, jnp.maximum(ntiles - 1, 0)))
    within = tile_ids - _pick(tend - ntile_e, texp)
    rstart = jnp.where(live, _pick(cstart, texp) + within * MOE_TM, n_list - MOE_TM)
    torder = _pick(order_e, texp)
    tfirst = (live & (within == 0)).astype(jnp.int32)
    tnext = jnp.where(tfirst == 1, _pick(act, torder + (W_SLOTS - 1)), -1)
    npairs = (ntiles[:, 0] + 1) // 2

    def flat(x):
        return x.reshape(-1).astype(jnp.int32)

    def smem_list():
        return pl.BlockSpec((None, 1, n_list), lambda b, *_: (b, 0, 0), memory_space=pltpu.SMEM)
    tile_buf = pltpu.VMEM((MOE_TM * ROW_TILES, LANES), F32)
    grid_spec = pltpu.PrefetchScalarGridSpec(
        num_scalar_prefetch=6, grid=(nblk,),
        in_specs=[smem_list(), smem_list(), smem_list(),
                  pl.BlockSpec((block_tokens * ROW_TILES, LANES), lambda b, *_: (b, 0),
                               pipeline_mode=pl.Buffered(1)),
                  pl.BlockSpec(memory_space=pl.ANY), pl.BlockSpec(memory_space=pl.ANY),
                  pl.BlockSpec(memory_space=pl.ANY)],
        out_specs=pl.BlockSpec(memory_space=pl.ANY),
        scratch_shapes=[pltpu.VMEM((W_SLOTS, D_MODEL, 2 * D_EXPERT), BF16),
                        pltpu.VMEM((W_SLOTS, D_EXPERT, D_MODEL), BF16),
                        pltpu.SemaphoreType.DMA((W_SLOTS,)), pltpu.SemaphoreType.DMA(()),
                        tile_buf, tile_buf,
                        pltpu.VMEM((MOE_TM, D_MODEL), BF16), pltpu.VMEM((MOE_TM, D_MODEL), BF16),
                        tile_buf, tile_buf,
                        pltpu.VMEM(((block_tokens + 1) * ROW_TILES, LANES), F32)])
    return pl.pallas_call(
        functools.partial(_moe_kernel, first_expert=first_expert),
        out_shape=jax.ShapeDtypeStruct((t * ROW_TILES, LANES), F32), grid_spec=grid_spec,
        compiler_params=_cparams(("arbitrary",)), name="moe",
    )(flat(rstart), flat(torder % W_SLOTS), flat(tfirst), flat(tnext), flat(npairs), flat(act[:, :W_SLOTS - 1]),
      src_s.reshape(nblk, 1, n_list), dst_s.reshape(nblk, 1, n_list), w_s.reshape(nblk, 1, n_list), h3, wg, wu, wd)


def _add_kernel(a_ref, b_ref, o_ref):
    o_ref[...] = a_ref[...] + _from_token_tiles(b_ref)


def _add(a, b_tiles):
    t = a.shape[0]
    spec = pl.BlockSpec((PROJ_ROWS, D_MODEL), lambda i: (i, 0))
    tile_spec = pl.BlockSpec((PROJ_ROWS * ROW_TILES, LANES), lambda i: (i, 0))
    return pl.pallas_call(_add_kernel, out_shape=jax.ShapeDtypeStruct(a.shape, a.dtype), grid=(t // PROJ_ROWS,),
                          in_specs=[spec, tile_spec], out_specs=spec, compiler_params=_cparams(("parallel",)),
                          name="residual_add")(a, b_tiles)


def _score_bound(gq, gk):
    return (HEAD_DIM * ATTN_SCALE * LOG2E) * jnp.max(jnp.abs(gq)) * jnp.max(jnp.abs(gk))


def _alibi_tables(shift):
    slopes = (2.0 ** (-8.0 * np.arange(1, N_HEADS_A + 1) / N_HEADS_A)).astype(np.float32)
    tables = []
    for _, dil in DILATED_PATTERNS:
        if dil == DILATED_PATTERNS[-1][1]:
            offs, kw = (0,), QBLK
        else:
            offs, kw = (0, DIL_RADIUS, 2 * DIL_RADIUS), 2 * QBLK
        tiles = []
        for off in offs:
            rel = np.abs((off + np.arange(QBLK))[:, None] - np.arange(kw)[None, :])
            tiles.append(np.where(rel <= DIL_RADIUS, rel * dil, np.inf).astype(np.float32))
        dist = np.stack(tiles)
        tab = -(slopes * LOG2E)[:, None, None, None] * dist[None]
        tab = jnp.asarray(tab) - shift
        tables.append(tab[:, 0] if len(offs) == 1 else tab)
    return tables


def _nb_window_start(g, rows_total):
    kh = min(NA_ROWS, rows_total)
    return int(np.clip(g * NB_GROUP - kh // 2, 0, rows_total - (NB_GROUP + kh)))


def _rpb_table(rpb, shift, rows_total):
    kh = min(NA_ROWS, rows_total)
    n_groups = rows_total // NB_GROUP
    win = NB_GROUP + kh
    assert n_groups >= 3 and all(_nb_window_start(g, rows_total) == g * NB_GROUP - kh // 2 for g in range(1, n_groups - 1))
    c = np.arange(GRID_W)
    c0 = np.clip(c - NA_COLS // 2, 0, GRID_W - NA_COLS)
    col_ok = (c[None, :] >= c0[:, None]) & (c[None, :] < c0[:, None] + NA_COLS)
    dc = np.clip(c[None, :] - c[:, None], -(NA_COLS - 1), NA_COLS - 1) + NA_COLS - 1
    n_dc = 2 * NA_COLS - 1
    pick = dc[None, None, :, :, None] == np.arange(n_dc)[None, None, None, None, :]
    cols = jnp.sum(jnp.where(pick, rpb[:, :, None, None, :], 0.0), axis=-1)
    cols = jnp.where(col_ok[None, None], cols, -jnp.inf)
    masked = jnp.full((rpb.shape[0], GRID_W, GRID_W), -jnp.inf, F32)
    types = []
    for g in (0, 1, n_groups - 1):
        ws = _nb_window_start(g, rows_total)
        q_rows = []
        for i in range(NB_GROUP):
            r = g * NB_GROUP + i
            r0 = int(np.clip(r - kh // 2, 0, rows_total - kh))
            blocks = [cols[:, ws + j - r + NA_ROWS - 1] if r0 <= ws + j < r0 + kh else masked for j in range(win)]
            q_rows.append(jnp.concatenate(blocks, axis=2))
        types.append(jnp.concatenate(q_rows, axis=1))
    tab = jnp.stack(types, axis=1) * LOG2E
    return tab - shift[:, None, None, None]


def kernel(x, mem, norm_mix, w_in, qk_gain, rpb, norm_mem, w_mem_kv, out_gain, w_out, norm_ffn, w_group,
           b_group, w_router, b_router, w_gate, w_up, w_down):
    bsz, seq, d = x.shape
    t = bsz * seq
    depth = w_in.shape[0]
    rows_total = seq // GRID_W

    x2 = x.reshape(t, d)
    wg_all = w_gate.astype(BF16).reshape(depth * N_EXPERTS, d, D_EXPERT)
    wu_all = w_up.astype(BF16).reshape(depth * N_EXPERTS, d, D_EXPERT)
    wd_all = w_down.astype(BF16).reshape(depth * N_EXPERTS, D_EXPERT, d)
    moe = None
    for l in range(depth):
        gains = jnp.tile(qk_gain[l], (1, 2))
        col_scale = jnp.concatenate([
            jnp.tile(gains[0] * QSCALE, WIDTH_A // LANES), jnp.tile(gains[1], WIDTH_A // LANES), jnp.ones((WIDTH_A,), F32),
            jnp.tile(gains[2] * QSCALE, WIDTH_B // LANES), jnp.tile(gains[3], WIDTH_B // LANES), jnp.ones((WIDTH_B,), F32),
            jnp.tile(gains[4] * QSCALE, WIDTH_M // LANES)])[None, :]
        w_in_l = w_in[l].astype(BF16)
        if moe is None:
            proj = _proj(x2, None, norm_mix[l][None, :], w_in_l, col_scale)
        else:
            x2, proj = _proj(x2, moe, norm_mix[l][None, :], w_in_l, col_scale)

        shift_a = _score_bound(qk_gain[l, 0], qk_gain[l, 1])
        b1, b4, b16 = _alibi_tables(shift_a)
        oa = _attn_a(proj, seq, b1, b4, b16)

        shift_b = _score_bound(qk_gain[l, 2], qk_gain[l, 3]) + LOG2E * jnp.max(rpb[l], axis=(1, 2))
        ob = _attn_b(proj, seq, _rpb_table(rpb[l], shift_b, rows_total))

        kaug, vaug = _memkv(mem, norm_mem[l][None, :], w_mem_kv[l].astype(BF16), gains[5:6])
        shift_m = jnp.full((1, mem.shape[1]), _score_bound(qk_gain[l, 4], qk_gain[l, 5]), F32)
        om = _attn_m(proj, seq, kaug, vaug, shift_m)

        w_route = jnp.zeros((d, LANES), F32).at[:, :N_EXPERTS].set(w_router[l])
        w_route = w_route.at[:, N_EXPERTS:N_EXPERTS + N_GROUPS].set(w_group[l])
        w_route_hi = w_route.astype(BF16)
        w_route_lo = (w_route - w_route_hi.astype(F32)).astype(BF16)
        b_route = jnp.zeros((1, LANES), F32).at[0, :N_EXPERTS].set(b_router[l])
        b_route = b_route.at[0, N_EXPERTS:N_EXPERTS + N_GROUPS].set(b_group[l])
        x2, h3, route = _mix(oa, ob, om, x2,
                             out_gain[l][None, :], w_out[l].astype(BF16), norm_ffn[l][None, :],
                             jnp.stack([w_route_hi, w_route_lo]), b_route)

        moe = _moe(h3, route, wg_all, wu_all, wd_all, l * N_EXPERTS, MOE_BLOCK_TOKENS)
    return _add(x2, moe).reshape(bsz, seq, d)
```

```python
import functools

import numpy as np
import jax
import jax.numpy as jnp
from jax import lax
from jax.experimental import pallas as pl
from jax.experimental.pallas import tpu as pltpu

F32 = jnp.float32
BF16 = jnp.bfloat16

D_MODEL = 1024
HEAD_DIM = 64
N_HEADS_A = 6
N_HEADS_B = 6
N_HEADS_M = 4
WIDTH_A = N_HEADS_A * HEAD_DIM
WIDTH_B = N_HEADS_B * HEAD_DIM
WIDTH_M = N_HEADS_M * HEAD_DIM
IN_WIDTH = 3 * WIDTH_A + 3 * WIDTH_B + WIDTH_M
DILATED_PATTERNS = ((128, 1), (512, 4), (2048, 16))
DIL_RADIUS = 64
GRID_W = 64
NA_ROWS = 8
NA_COLS = 16
N_GROUPS = 4
EXPERTS_PER_GROUP = 8
N_EXPERTS = N_GROUPS * EXPERTS_PER_GROUP
D_EXPERT = 512
EPS = 1e-6
ATTN_SCALE = HEAD_DIM ** -0.5
LOG2E = float(np.log2(np.e))
QSCALE = ATTN_SCALE * LOG2E

LANES = 128
ROW_TILES = D_MODEL // LANES
QBLK = 128
PROJ_ROWS = 1024
ROUTE_ROWS = 8
MOE_TM = 128
MOE_BLOCK_TOKENS = 4096
VMEM_LIMIT = 56 * 1024 * 1024


def _cparams(sem):
    return pltpu.CompilerParams(dimension_semantics=sem, vmem_limit_bytes=VMEM_LIMIT)


def _from_token_tiles(t_ref):
    rows = t_ref.shape[0] // ROW_TILES
    return jnp.concatenate([t_ref[pl.ds(j, rows, stride=ROW_TILES), :] for j in range(ROW_TILES)], axis=1)


def _qk_slabs():
    a, b, m = WIDTH_A // LANES, WIDTH_B // LANES, WIDTH_M // LANES
    return tuple(range(0, 2 * a)) + tuple(range(3 * a, 3 * a + 2 * b)) + tuple(range(3 * (a + b), 3 * (a + b) + m))


def _proj_kernel(*refs, add):
    if add:
        x_ref, m_ref, g_ref, w_ref, cs_ref, xo_ref, o_ref = refs
        x = x_ref[...] + _from_token_tiles(m_ref)
        xo_ref[...] = x
    else:
        x_ref, g_ref, w_ref, cs_ref, o_ref = refs
        x = x_ref[...]
    ms = jnp.mean(x * x, axis=-1, keepdims=True)
    h = (x * lax.rsqrt(ms + EPS) * g_ref[...]).astype(BF16)
    acc = jnp.dot(h, w_ref[...], preferred_element_type=F32)
    lo = lax.broadcasted_iota(jnp.int32, (x.shape[0], LANES), 1) < HEAD_DIM
    qk = _qk_slabs()
    for s in range(IN_WIDTH // LANES):
        cols = slice(s * LANES, (s + 1) * LANES)
        y = acc[:, cols]
        if s in qk:
            sq = y * y
            ss_lo = jnp.sum(jnp.where(lo, sq, 0.0), axis=-1, keepdims=True)
            ss_hi = jnp.sum(jnp.where(lo, 0.0, sq), axis=-1, keepdims=True)
            inv = jnp.where(lo, lax.rsqrt(ss_lo * (1.0 / HEAD_DIM) + EPS), lax.rsqrt(ss_hi * (1.0 / HEAD_DIM) + EPS))
            y = y * inv * cs_ref[:, cols]
        o_ref[:, cols] = y.astype(BF16)


def _proj(x2, moe2, gain, w, col_scale):
    t = x2.shape[0]
    add = moe2 is not None
    row_spec = pl.BlockSpec((PROJ_ROWS, D_MODEL), lambda i: (i, 0))
    tile_spec = pl.BlockSpec((PROJ_ROWS * ROW_TILES, LANES), lambda i: (i, 0))
    in_specs = [row_spec] + ([tile_spec] if add else []) + [
        pl.BlockSpec((1, D_MODEL), lambda i: (0, 0)),
        pl.BlockSpec((D_MODEL, IN_WIDTH), lambda i: (0, 0)),
        pl.BlockSpec((1, IN_WIDTH), lambda i: (0, 0)),
    ]
    proj_spec = pl.BlockSpec((PROJ_ROWS, IN_WIDTH), lambda i: (i, 0))
    proj_shape = jax.ShapeDtypeStruct((t, IN_WIDTH), BF16)
    if add:
        out_shape = (jax.ShapeDtypeStruct((t, D_MODEL), F32), proj_shape)
        out_specs = (row_spec, proj_spec)
        args = (x2, moe2, gain, w, col_scale)
    else:
        out_shape, out_specs, args = proj_shape, proj_spec, (x2, gain, w, col_scale)
    return pl.pallas_call(
        functools.partial(_proj_kernel, add=add),
        out_shape=out_shape, grid=(t // PROJ_ROWS,), in_specs=in_specs, out_specs=out_specs,
        compiler_params=_cparams(("parallel",)), name="proj_add" if add else "proj",
    )(*args)


def _data_mask(lane, par):
    return (lane < HEAD_DIM) if par == 0 else (lane >= HEAD_DIM)


def _with_ones(v, dm):
    return jnp.where(dm, v, 1.0)


def _head_norm(x, dm, gain):
    ss = jnp.sum(jnp.where(dm, x * x, 0.0), axis=-1, keepdims=True)
    return x * lax.rsqrt(ss * (1.0 / HEAD_DIM) + EPS) * gain


def _normalise(tot):
    return tot / pltpu.roll(tot, HEAD_DIM, axis=1)


def _memkv_kernel(m_ref, g_ref, w_ref, kg_ref, k_ref, v_ref):
    x = m_ref[...]
    ms = jnp.mean(x * x, axis=-1, keepdims=True)
    h = (x * lax.rsqrt(ms + EPS) * g_ref[...]).astype(BF16)
    kv = jnp.dot(h, w_ref[...], preferred_element_type=F32)
    lane = lax.broadcasted_iota(jnp.int32, (x.shape[0], LANES), 1)
    for slab in range(WIDTH_M // LANES):
        kf = kv[:, slab * LANES:(slab + 1) * LANES]
        vf = kv[:, WIDTH_M + slab * LANES:WIDTH_M + (slab + 1) * LANES]
        for par in (0, 1):
            dm = _data_mask(lane, par)
            kn = _head_norm(kf, dm, kg_ref[...])
            k_ref[2 * slab + par] = jnp.where(dm, kn, 0.0).astype(BF16)
            v_ref[2 * slab + par] = _with_ones(vf, dm).astype(BF16)


def _memkv(mem, gain, w, kgain):
    b, m, _ = mem.shape
    kv_shape = jax.ShapeDtypeStruct((b, N_HEADS_M, m, LANES), BF16)
    kv_spec = pl.BlockSpec((None, N_HEADS_M, m, LANES), lambda i: (i, 0, 0, 0))
    return pl.pallas_call(
        _memkv_kernel, out_shape=(kv_shape, kv_shape), grid=(b,),
        in_specs=[pl.BlockSpec((None, m, D_MODEL), lambda i: (i, 0, 0)),
                  pl.BlockSpec((1, D_MODEL), lambda i: (0, 0)),
                  pl.BlockSpec((D_MODEL, 2 * WIDTH_M), lambda i: (0, 0)),
                  pl.BlockSpec((1, LANES), lambda i: (0, 0))],
        out_specs=(kv_spec, kv_spec), compiler_params=_cparams(("parallel",)), name="memkv",
    )(mem, gain, w, kgain)


PREP_ROWS = 256
A_UNROLL = 16


def _attn_a_kernel(q_ref, k_ref, v_ref, b1_ref, b4_ref, b16_ref, o_ref,
                   qs, ks, vs, qm, km, vm, q4, q16, kd, vd, accd, accm, tot, s_scr):
    s_len = q_ref.shape[0]
    n_prep = s_len // PREP_ROWS
    lane_p = lax.broadcasted_iota(jnp.int32, (PREP_ROWS, LANES), 1)
    d_mid, d_far = DILATED_PATTERNS[1][1], DILATED_PATTERNS[2][1]
    len_mid, len_far = s_len // d_mid, s_len // d_far
    sub = d_far // d_mid

    def chunk_rows(c):
        return pl.ds(pl.multiple_of(c * PREP_ROWS, PREP_ROWS), PREP_ROWS)

    def mid_rows(r):
        return pl.ds(r, len_mid, stride=d_mid), pl.ds(pl.multiple_of(r * len_mid, QBLK), len_mid)

    def far_rows(c):
        return (pl.ds((c // sub) * len_mid + c % sub, len_far, stride=sub),
                pl.ds(pl.multiple_of(c * len_far, QBLK), len_far))

    def split_mid(pairs):
        def body(r, carry):
            tok, mid = mid_rows(r)
            for x_tok, x_mid, x_bf in pairs:
                x = x_tok[tok, :]
                x_mid[mid, :] = x
                x_bf[mid, :] = x.astype(BF16)
            return carry
        lax.fori_loop(0, d_mid, body, 0)

    def split_far(pairs):
        def body(c, carry):
            mid, far = far_rows(c)
            for x_mid, x_bf in pairs:
                x_bf[far, :] = x_mid[mid, :].astype(BF16)
            return carry
        lax.fori_loop(0, d_far, body, 0)

    def q_cast(c, carry):
        qs[chunk_rows(c), :] = q_ref[chunk_rows(c), :].astype(F32)
        return carry
    lax.fori_loop(0, n_prep, q_cast, 0)
    split_mid(((qs, qm, q4),))
    split_far(((qm, q16),))

    for par in (0, 1):
        def prep(c, carry, par=par):
            rows = chunk_rows(c)
            dm = _data_mask(lane_p, par)
            kf = jnp.where(dm, k_ref[rows, :].astype(F32), 0.0)
            vf = _with_ones(v_ref[rows, :].astype(F32), dm)
            ks[rows, :] = kf
            vs[rows, :] = vf
            kd[rows, :] = kf.astype(BF16)
            vd[rows, :] = vf.astype(BF16)
            return carry
        lax.fori_loop(0, n_prep, prep, 0)

        def run_blocks(q_src, length, b_ref, out, par=par):
            nblk = length // QBLK

            def blocks(o, carry):
                units = []
                for i in range(A_UNROLL):
                    u = o * A_UNROLL + i
                    if nblk == 1:
                        base = pl.multiple_of(u * length, QBLK)
                        qrows = pl.ds(base, QBLK)
                        keys = pl.ds(base, QBLK)
                        bias = b_ref[par]
                        kw = QBLK
                    else:
                        r = u // nblk
                        blk = u % nblk
                        q0 = blk * QBLK
                        ws = jnp.clip(q0 - DIL_RADIUS, 0, length - 2 * QBLK)
                        tid = jnp.where(blk == 0, 0, jnp.where(blk == nblk - 1, 2, 1))
                        qrows = pl.ds(pl.multiple_of(r * length + q0, QBLK), QBLK)
                        keys = pl.ds(pl.multiple_of(r * length + ws, DIL_RADIUS), 2 * QBLK)
                        bias = b_ref[par, tid]
                        kw = 2 * QBLK
                    s_scr[i, :, 0:kw] = lax.dot_general(q_src[qrows, :], kd[keys, :], (((1,), (1,)), ((), ())),
                                                        preferred_element_type=F32)
                    units.append((qrows, keys, bias, kw))
                for i, (qrows, keys, bias, kw) in enumerate(units):
                    p = jnp.exp2(s_scr[i, :, 0:kw] + bias).astype(BF16)
                    out[qrows, :] = jnp.dot(p, vd[keys, :], preferred_element_type=F32)
                return carry
            lax.fori_loop(0, s_len // QBLK // A_UNROLL, blocks, 0)

        run_blocks(q_ref, s_len, b1_ref, tot.at[par])

        split_mid(((ks, km, kd), (vs, vm, vd)))
        run_blocks(q4, len_mid, b4_ref, accm)
        split_far(((km, kd), (vm, vd)))
        run_blocks(q16, len_far, b16_ref, accd)

        def merge_far(c, carry):
            mid, far = far_rows(c)
            accm[mid, :] = accm[mid, :] + accd[far, :]
            return carry
        lax.fori_loop(0, d_far, merge_far, 0)

        def merge_mid(r, carry, par=par):
            tok, mid = mid_rows(r)
            tot[par, tok, :] = tot[par, tok, :] + accm[mid, :]
            return carry
        lax.fori_loop(0, d_mid, merge_mid, 0)

    def fin(c, carry):
        rows = chunk_rows(c)
        o0 = _normalise(tot[0, rows, :])
        o1 = _normalise(tot[1, rows, :])
        o_ref[rows, :] = jnp.where(lane_p < HEAD_DIM, o0, o1).astype(BF16)
        return carry
    lax.fori_loop(0, n_prep, fin, 0)


def _attn_a(proj, s, b1, b4, b16):
    b = proj.shape[0] // s
    n_pairs = WIDTH_A // LANES

    def slab(off):
        return pl.BlockSpec((s, LANES), lambda i, p: (i, off + p))
    return pl.pallas_call(
        _attn_a_kernel, out_shape=jax.ShapeDtypeStruct((b * s, WIDTH_A), BF16), grid=(b, n_pairs),
        in_specs=[slab(0), slab(n_pairs), slab(2 * n_pairs),
                  pl.BlockSpec((2, 3, QBLK, 2 * QBLK), lambda i, p: (p, 0, 0, 0)),
                  pl.BlockSpec((2, 3, QBLK, 2 * QBLK), lambda i, p: (p, 0, 0, 0)),
                  pl.BlockSpec((2, QBLK, QBLK), lambda i, p: (p, 0, 0))],
        out_specs=pl.BlockSpec((s, LANES), lambda i, p: (i, p)),
        scratch_shapes=[pltpu.VMEM((s, LANES), F32)] * 6 + [pltpu.VMEM((s, LANES), BF16)] * 4
        + [pltpu.VMEM((s, LANES), F32)] * 2
        + [pltpu.VMEM((2, s, LANES), F32), pltpu.VMEM((A_UNROLL, QBLK, 2 * QBLK), F32)],
        compiler_params=_cparams(("parallel", "parallel")), name="attn_a",
    )(proj, proj, proj, b1, b4, b16)


NB_GROUP = 4
NB_UNROLL = 4


def _attn_b_kernel(q_ref, k_ref, v_ref, bias_ref, o_ref, kb, vb, s_scr):
    s_len = q_ref.shape[0]
    n_prep = s_len // PREP_ROWS
    rows_total = s_len // GRID_W
    kh = min(NA_ROWS, rows_total)
    n_groups = rows_total // NB_GROUP
    q_len = NB_GROUP * GRID_W
    k_len = (NB_GROUP + kh) * GRID_W
    lane_p = lax.broadcasted_iota(jnp.int32, (PREP_ROWS, LANES), 1)
    lane_q = lax.broadcasted_iota(jnp.int32, (q_len, LANES), 1)

    def prep(c, carry):
        rows = pl.ds(pl.multiple_of(c * PREP_ROWS, PREP_ROWS), PREP_ROWS)
        kf = k_ref[rows, :].astype(F32)
        vf = v_ref[rows, :].astype(F32)
        for par in (0, 1):
            dm = _data_mask(lane_p, par)
            kb[par, rows, :] = jnp.where(dm, kf, 0.0).astype(BF16)
            vb[par, rows, :] = _with_ones(vf, dm).astype(BF16)
        return carry
    lax.fori_loop(0, n_prep, prep, 0)

    def groups(o, carry):
        units = []
        for i in range(NB_UNROLL):
            g = o * NB_UNROLL + i
            ws = jnp.clip(g * NB_GROUP - kh // 2, 0, rows_total - (NB_GROUP + kh))
            typ = jnp.where(g == 0, 0, jnp.where(g == n_groups - 1, 2, 1))
            qrows = pl.ds(pl.multiple_of(g * q_len, q_len), q_len)
            keys = pl.ds(pl.multiple_of(ws * GRID_W, GRID_W), k_len)
            for par in (0, 1):
                s_scr[2 * i + par] = lax.dot_general(q_ref[qrows, :], kb[par, keys, :], (((1,), (1,)), ((), ())),
                                                     preferred_element_type=F32)
            units.append((qrows, keys, typ))
        for i, (qrows, keys, typ) in enumerate(units):
            outs = []
            for par in (0, 1):
                p = jnp.exp2(s_scr[2 * i + par] + bias_ref[par, typ]).astype(BF16)
                acc = jnp.dot(p, vb[par, keys, :], preferred_element_type=F32)
                outs.append(_normalise(acc))
            o_ref[qrows, :] = jnp.where(lane_q < HEAD_DIM, outs[0], outs[1]).astype(BF16)
        return carry
    lax.fori_loop(0, n_groups // NB_UNROLL, groups, 0)


def _attn_b(proj, s, bias):
    b = proj.shape[0] // s
    n_pairs = WIDTH_B // LANES
    first = 3 * WIDTH_A // LANES
    n_types, q_len, k_len = bias.shape[1:]

    def slab(off):
        return pl.BlockSpec((s, LANES), lambda i, p: (i, first + off + p))
    return pl.pallas_call(
        _attn_b_kernel, out_shape=jax.ShapeDtypeStruct((b * s, WIDTH_B), BF16), grid=(b, n_pairs),
        in_specs=[slab(0), slab(n_pairs), slab(2 * n_pairs),
                  pl.BlockSpec((2, n_types, q_len, k_len), lambda i, p: (p, 0, 0, 0))],
        out_specs=pl.BlockSpec((s, LANES), lambda i, p: (i, p)),
        scratch_shapes=[pltpu.VMEM((2, s, LANES), BF16)] * 2 + [pltpu.VMEM((2 * NB_UNROLL, q_len, k_len), F32)],
        compiler_params=_cparams(("parallel", "parallel")), name="attn_b",
    )(proj, proj, proj, bias)


MEM_QROWS = 256
MEM_UNROLL = 4


def _attn_m_kernel(q_ref, k_ref, v_ref, shift_ref, o_ref, s_scr):
    s_len = q_ref.shape[0]
    lane = lax.broadcasted_iota(jnp.int32, (MEM_QROWS, LANES), 1)

    def chunks(o, carry):
        all_rows = []
        for i in range(MEM_UNROLL):
            rows = pl.ds(pl.multiple_of((o * MEM_UNROLL + i) * MEM_QROWS, MEM_QROWS), MEM_QROWS)
            for par in (0, 1):
                s_scr[2 * i + par] = lax.dot_general(q_ref[rows, :], k_ref[par], (((1,), (1,)), ((), ())),
                                                     preferred_element_type=F32)
            all_rows.append(rows)
        for i, rows in enumerate(all_rows):
            outs = []
            for par in (0, 1):
                p = jnp.exp2(s_scr[2 * i + par] - shift_ref[...]).astype(BF16)
                acc = jnp.dot(p, v_ref[par], preferred_element_type=F32)
                outs.append(_normalise(acc))
            o_ref[rows, :] = jnp.where(lane < HEAD_DIM, outs[0], outs[1]).astype(BF16)
        return carry
    lax.fori_loop(0, s_len // MEM_QROWS // MEM_UNROLL, chunks, 0)


def _attn_m(proj, s, kaug, vaug, shift):
    b = proj.shape[0] // s
    m = kaug.shape[2]
    n_pairs = WIDTH_M // LANES
    first = 3 * (WIDTH_A + WIDTH_B) // LANES
    kv_spec = pl.BlockSpec((None, 2, m, LANES), lambda i, p: (i, p, 0, 0))
    return pl.pallas_call(
        _attn_m_kernel, out_shape=jax.ShapeDtypeStruct((b * s, WIDTH_M), BF16), grid=(b, n_pairs),
        in_specs=[pl.BlockSpec((s, LANES), lambda i, p: (i, first + p)), kv_spec, kv_spec,
                  pl.BlockSpec((1, m), lambda i, p: (0, 0))],
        out_specs=pl.BlockSpec((s, LANES), lambda i, p: (i, p)),
        scratch_shapes=[pltpu.VMEM((2 * MEM_UNROLL, MEM_QROWS, m), F32)],
        compiler_params=_cparams(("parallel", "parallel")), name="attn_m",
    )(proj, kaug, vaug, shift)


def _mix_kernel(oa_ref, ob_ref, om_ref, x_ref, og_ref, wo_ref, gf_ref, wr_ref, br_ref,
                x1_ref, h3_ref, route_ref):
    rows = x_ref.shape[0]
    y = None
    off = 0
    for o_ref in (oa_ref, ob_ref, om_ref):
        width = o_ref.shape[1]
        o = o_ref[...].astype(F32)
        ms = jnp.mean(o * o, axis=-1, keepdims=True)
        mixed = (o * lax.rsqrt(ms + EPS) * og_ref[:, off:off + width]).astype(BF16)
        part = jnp.dot(mixed, wo_ref[off:off + width, :], preferred_element_type=F32)
        y = part if y is None else y + part
        off += width
    x1 = x_ref[...] + y
    x1_ref[...] = x1
    ms = jnp.mean(x1 * x1, axis=-1, keepdims=True)
    h = x1 * lax.rsqrt(ms + EPS) * gf_ref[...]
    for j in range(ROW_TILES):
        h3_ref[pl.ds(j, rows, stride=ROW_TILES), :] = h[:, j * LANES:(j + 1) * LANES]

    h_hi = h.astype(BF16)
    h_lo = (h - h_hi.astype(F32)).astype(BF16)
    logits = (jnp.dot(h_hi, wr_ref[0], preferred_element_type=F32)
              + jnp.dot(h_lo, wr_ref[0], preferred_element_type=F32)
              + jnp.dot(h_hi, wr_ref[1], preferred_element_type=F32)) + br_ref[...]
    lane = lax.broadcasted_iota(jnp.int32, (rows, LANES), 1)
    lane_f = lane.astype(F32)
    neg = -jnp.inf
    far = float(LANES)
    is_g = (lane >> 2) == (N_EXPERTS // N_GROUPS)
    gl = jnp.where(is_g, logits, neg)
    gmax = jnp.max(gl, axis=-1, keepdims=True)
    gidx = jnp.min(jnp.where(gl == gmax, lane_f, far), axis=-1, keepdims=True) - float(N_EXPERTS)
    gsum = jnp.sum(jnp.where(is_g, jnp.exp(gl - gmax), 0.0), axis=-1, keepdims=True)
    g_top = 1.0 / gsum
    in_grp = (lane >> 3).astype(F32) == gidx
    el = jnp.where(in_grp, logits, neg)
    v1 = jnp.max(el, axis=-1, keepdims=True)
    i1 = jnp.min(jnp.where(el == v1, lane_f, far), axis=-1, keepdims=True)
    el2 = jnp.where(lane_f == i1, neg, el)
    v2 = jnp.max(el2, axis=-1, keepdims=True)
    i2 = jnp.min(jnp.where(el2 == v2, lane_f, far), axis=-1, keepdims=True)
    ev = jnp.exp(v2 - v1)
    w1 = g_top / (1.0 + ev)
    w2 = g_top * ev / (1.0 + ev)
    route = jnp.where(lane == 0, i1, jnp.where(lane == 1, i2, jnp.where(lane == 2, w1,
                      jnp.where(lane == 3, w2, 0.0))))
    per_token = jnp.transpose(route)[0:4, :]
    chosen = jnp.where(lane_f == i1, 1.0, jnp.where(lane_f == i2, 1.0, 0.0))
    counts = jnp.sum(chosen, axis=0, keepdims=True)
    route_ref[...] = jnp.zeros_like(route_ref)
    route_ref[0:4, :] = per_token
    route_ref[4:5, 0:LANES] = counts


def _mix(oa, ob, om, x2, out_gain, w_out, gain_ffn, w_route, b_route):
    t = x2.shape[0]

    def rows(width):
        return pl.BlockSpec((PROJ_ROWS, width), lambda i: (i, 0))

    def whole(*shape):
        return pl.BlockSpec(shape, lambda i: (0,) * len(shape))
    return pl.pallas_call(
        _mix_kernel,
        out_shape=(jax.ShapeDtypeStruct((t, D_MODEL), F32),
                   jax.ShapeDtypeStruct((t * ROW_TILES, LANES), F32),
                   jax.ShapeDtypeStruct((t // PROJ_ROWS * ROUTE_ROWS, PROJ_ROWS), F32)),
        grid=(t // PROJ_ROWS,),
        in_specs=[rows(WIDTH_A), rows(WIDTH_B), rows(WIDTH_M), rows(D_MODEL),
                  whole(1, D_MODEL), whole(D_MODEL, D_MODEL), whole(1, D_MODEL),
                  whole(2, D_MODEL, LANES), whole(1, LANES)],
        out_specs=(rows(D_MODEL), pl.BlockSpec((PROJ_ROWS * ROW_TILES, LANES), lambda i: (i, 0)),
                   pl.BlockSpec((ROUTE_ROWS, PROJ_ROWS), lambda i: (i, 0))),
        compiler_params=_cparams(("parallel",)), name="mix",
    )(oa, ob, om, x2, out_gain, w_out, gain_ffn, w_route, b_route)


SCATTER_UNROLL = 8
W_SLOTS = 4
W_CHUNK_ROWS = 512


def _moe_weight_copies(wg_hbm, wu_hbm, wd_hbm, wgu_buf, wd_buf, sems, first_expert, e, slot):
    copies = []
    targets = ((wg_hbm, wgu_buf, pl.ds(0, D_EXPERT)), (wu_hbm, wgu_buf, pl.ds(D_EXPERT, D_EXPERT)),
               (wd_hbm, wd_buf, pl.ds(0, D_MODEL)))
    for hbm, buf, cols in targets:
        for c in range(hbm.shape[1] // W_CHUNK_ROWS):
            rows = pl.ds(c * W_CHUNK_ROWS, W_CHUNK_ROWS)
            copies.append(pltpu.make_async_copy(hbm.at[first_expert + e, rows], buf.at[slot, rows, cols],
                                                sems.at[slot]))
    return copies


def _moe_kernel(rstart_ref, tslot_ref, tfirst_ref, tnext_ref, npair_ref, act_ref,
                src_ref, dst_ref, gate_ref, h3_ref, wg_hbm, wu_hbm, wd_hbm, o_hbm,
                wgu_buf, wd_buf, sems, out_sem, xt0, xt1, lhs0, lhs1, yt0, yt1, acc, *, first_expert):
    b = pl.program_id(0)
    n_rows = h3_ref.shape[0]
    tiles_per_block = 2 * (n_rows // ROW_TILES) // MOE_TM + N_EXPERTS + 2
    pad_rows = src_ref.shape[1] - MOE_TM
    base = b * tiles_per_block
    bufs = ((xt0, lhs0, yt0), (xt1, lhs1, yt1))
    copies = functools.partial(_moe_weight_copies, wg_hbm, wu_hbm, wd_hbm, wgu_buf, wd_buf, sems, first_expert)

    def gather(rs, xt):
        for i in range(MOE_TM):
            off = pl.multiple_of(src_ref[0, rs + i], ROW_TILES)
            xt[i * ROW_TILES:(i + 1) * ROW_TILES, :] = h3_ref[pl.ds(off, ROW_TILES), :]

    def scatter(rs, yt):
        for i0 in range(0, MOE_TM, SCATTER_UNROLL):
            vals = []
            for i in range(i0, i0 + SCATTER_UNROLL):
                rows = pl.ds(pl.multiple_of(dst_ref[0, rs + i], ROW_TILES), ROW_TILES)
                vals.append((rows, acc[rows, :] + gate_ref[0, rs + i] * yt[i * ROW_TILES:(i + 1) * ROW_TILES, :]))
            for rows, val in vals:
                acc[rows, :] = val

    for k in range(W_SLOTS - 1):
        e0 = act_ref[b * (W_SLOTS - 1) + k]

        @pl.when(e0 >= 0)
        def _(e0=e0, k=k):
            for c in copies(e0, k):
                c.start()

    def out_copy(block):
        return pltpu.make_async_copy(acc.at[pl.ds(0, n_rows)],
                                     o_hbm.at[pl.ds(pl.multiple_of(block * n_rows, ROW_TILES), n_rows)], out_sem)

    @pl.when(b > 0)
    def _():
        out_copy(b - 1).wait()
    acc[...] = jnp.zeros_like(acc)
    for half, (xt, _, yt) in enumerate(bufs):
        yt[...] = jnp.zeros_like(yt)
        gather(rstart_ref[base + half], xt)

    def pair(p, carry):
        tiles = []
        for half in (0, 1):
            idx = base + 2 * p + half
            slot = tslot_ref[idx]

            @pl.when(tfirst_ref[idx] == 1)
            def _(slot=slot):
                for c in copies(0, slot):
                    c.wait()
            prev_rs = jnp.where(p > 0, rstart_ref[jnp.maximum(idx - 2, 0)], pad_rows)
            tiles.append((prev_rs, rstart_ref[idx + 2], slot, tnext_ref[idx]))

        for (prev_rs, _, _, _), (_, _, yt) in zip(tiles, bufs):
            scatter(prev_rs, yt)
        for xt, lhs, _ in bufs:
            for k in range(ROW_TILES):
                lhs[:, k * LANES:(k + 1) * LANES] = xt[pl.ds(k, MOE_TM, stride=ROW_TILES), :].astype(BF16)
        for (_, next_rs, _, _), (xt, _, _) in zip(tiles, bufs):
            gather(next_rs, xt)
        for (_, _, slot, _), (_, lhs, yt) in zip(tiles, bufs):
            gu = jnp.dot(lhs[...], wgu_buf[slot], preferred_element_type=F32)
            g = gu[:, :D_EXPERT]
            hid = (g * (1.0 / (1.0 + jnp.exp(-g)))) * gu[:, D_EXPERT:]
            y = jnp.dot(hid.astype(BF16), wd_buf[slot], preferred_element_type=F32)
            for k in range(ROW_TILES):
                yt[pl.ds(k, MOE_TM, stride=ROW_TILES), :] = y[:, k * LANES:(k + 1) * LANES]

        for _, _, slot, nxt in tiles:
            @pl.when(nxt >= 0)
            def _(slot=slot, nxt=nxt):
                for c in copies(nxt, (slot + W_SLOTS - 1) % W_SLOTS):
                    c.start()
        return carry
    n_pairs = npair_ref[b]
    lax.fori_loop(0, n_pairs, pair, 0)
    for half, (_, _, yt) in enumerate(bufs):
        scatter(rstart_ref[base + 2 * (n_pairs - 1) + half], yt)

    out_copy(b).start()

    @pl.when(b == pl.num_programs(0) - 1)
    def _():
        out_copy(b).wait()


def _pick(table, index):
    cols = jnp.arange(table.shape[1], dtype=jnp.int32)
    return jnp.sum(jnp.where(index[:, :, None] == cols[None, None, :], table[:, None, :], 0), axis=2)


def _moe(h3, route_t, wg, wu, wd, first_expert, block_tokens):
    t = h3.shape[0] // ROW_TILES
    nblk = t // block_tokens
    n_assign = 2 * block_tokens
    nt_max = n_assign // MOE_TM + N_EXPERTS + 2

    n_pad = MOE_TM - 1
    stride = 2 * n_assign
    n_list = n_assign + N_EXPERTS * MOE_TM
    rec = route_t.reshape(nblk, block_tokens // PROJ_ROWS, ROUTE_ROWS, PROJ_ROWS)
    e = rec[:, :, 0:2, :].astype(jnp.int32).reshape(nblk, n_assign)
    w = rec[:, :, 2:4, :].reshape(nblk, n_assign)
    cnt = jnp.sum(rec[:, :, 4, :N_EXPERTS], axis=1).astype(jnp.int32)
    experts = jnp.arange(N_EXPERTS, dtype=jnp.int32)
    key = e * stride + jnp.arange(n_assign, dtype=jnp.int32)[None, :]
    pad_key = (experts[:, None] * stride + n_assign + jnp.arange(n_pad, dtype=jnp.int32)[None, :]).reshape(1, -1)
    tail_key = (N_EXPERTS * stride + n_assign + experts)[None, :]
    extra_key = jnp.broadcast_to(jnp.concatenate([pad_key, tail_key], axis=1), (nblk, n_list - n_assign))
    key_s, w_s = lax.sort((jnp.concatenate([key, extra_key], axis=1),
                           jnp.concatenate([w, jnp.zeros((nblk, n_list - n_assign), F32)], axis=1)),
                          dimension=1, num_keys=1)
    idx_s = key_s & (stride - 1)
    real = idx_s < n_assign
    row_off = ((idx_s // (2 * PROJ_ROWS)) * PROJ_ROWS + idx_s % PROJ_ROWS) * ROW_TILES
    src_s = jnp.where(real, row_off, 0)
    dst_s = jnp.where(real, row_off, block_tokens * ROW_TILES)
    cstart = jnp.cumsum(cnt, axis=1) - cnt + experts[None, :] * n_pad
    ntile_e = (cnt + MOE_TM - 1) // MOE_TM
    tend = jnp.cumsum(ntile_e, axis=1)
    ntiles = tend[:, -1:]
    active = (ntile_e > 0).astype(jnp.int32)
    order_e = jnp.cumsum(active, axis=1) - active
    ranks = jnp.arange(N_EXPERTS + W_SLOTS, dtype=jnp.int32)
    hit = (order_e[:, None, :] == ranks[None, :, None]) & (active[:, None, :] == 1)
    act = jnp.sum(jnp.where(hit, experts[None, None, :] + 1, 0), axis=2) - 1
    tile_ids = jnp.arange(nt_max, dtype=jnp.int32)[None, :]
    live = tile_ids < ntiles
    texp = jnp.minimum(jnp.sum((tile_ids[:, :, None] >= tend[:, None, :]).astype(jnp.int32), axis=2), N_EXPERTS - 1)
    texp = jnp.where(live, texp, _pick(texp, jnp.maximum(ntiles - 1, 0)))
    within = tile_ids - _pick(tend - ntile_e, texp)
    rstart = jnp.where(live, _pick(cstart, texp) + within * MOE_TM, n_list - MOE_TM)
    torder = _pick(order_e, texp)
    tfirst = (live & (within == 0)).astype(jnp.int32)
    tnext = jnp.where(tfirst == 1, _pick(act, torder + (W_SLOTS - 1)), -1)
    npairs = (ntiles[:, 0] + 1) // 2

    def flat(x):
        return x.reshape(-1).astype(jnp.int32)

    def smem_list():
        return pl.BlockSpec((None, 1, n_list), lambda b, *_: (b, 0, 0), memory_space=pltpu.SMEM)
    tile_buf = pltpu.VMEM((MOE_TM * ROW_TILES, LANES), F32)
    grid_spec = pltpu.PrefetchScalarGridSpec(
        num_scalar_prefetch=6, grid=(nblk,),
        in_specs=[smem_list(), smem_list(), smem_list(),
                  pl.BlockSpec((block_tokens * ROW_TILES, LANES), lambda b, *_: (b, 0),
                               pipeline_mode=pl.Buffered(1)),
                  pl.BlockSpec(memory_space=pl.ANY), pl.BlockSpec(memory_space=pl.ANY),
                  pl.BlockSpec(memory_space=pl.ANY)],
        out_specs=pl.BlockSpec(memory_space=pl.ANY),
        scratch_shapes=[pltpu.VMEM((W_SLOTS, D_MODEL, 2 * D_EXPERT), BF16),
                        pltpu.VMEM((W_SLOTS, D_EXPERT, D_MODEL), BF16),
                        pltpu.SemaphoreType.DMA((W_SLOTS,)), pltpu.SemaphoreType.DMA(()),
                        tile_buf, tile_buf,
                        pltpu.VMEM((MOE_TM, D_MODEL), BF16), pltpu.VMEM((MOE_TM, D_MODEL), BF16),
                        tile_buf, tile_buf,
                        pltpu.VMEM(((block_tokens + 1) * ROW_TILES, LANES), F32)])
    return pl.pallas_call(
        functools.partial(_moe_kernel, first_expert=first_expert),
        out_shape=jax.ShapeDtypeStruct((t * ROW_TILES, LANES), F32), grid_spec=grid_spec,
        compiler_params=_cparams(("arbitrary",)), name="moe",
    )(flat(rstart), flat(torder % W_SLOTS), flat(tfirst), flat(tnext), flat(npairs), flat(act[:, :W_SLOTS - 1]),
      src_s.reshape(nblk, 1, n_list), dst_s.reshape(nblk, 1, n_list), w_s.reshape(nblk, 1, n_list), h3, wg, wu, wd)


def _add_kernel(a_ref, b_ref, o_ref):
    o_ref[...] = a_ref[...] + _from_token_tiles(b_ref)


def _add(a, b_tiles):
    t = a.shape[0]
    spec = pl.BlockSpec((PROJ_ROWS, D_MODEL), lambda i: (i, 0))
    tile_spec = pl.BlockSpec((PROJ_ROWS * ROW_TILES, LANES), lambda i: (i, 0))
    return pl.pallas_call(_add_kernel, out_shape=jax.ShapeDtypeStruct(a.shape, a.dtype), grid=(t // PROJ_ROWS,),
                          in_specs=[spec, tile_spec], out_specs=spec, compiler_params=_cparams(("parallel",)),
                          name="residual_add")(a, b_tiles)


def _score_bound(gq, gk):
    return (HEAD_DIM * ATTN_SCALE * LOG2E) * jnp.max(jnp.abs(gq)) * jnp.max(jnp.abs(gk))


def _alibi_tables(shift):
    slopes = (2.0 ** (-8.0 * np.arange(1, N_HEADS_A + 1) / N_HEADS_A)).astype(np.float32)
    tables = []
    for _, dil in DILATED_PATTERNS:
        if dil == DILATED_PATTERNS[-1][1]:
            offs, kw = (0,), QBLK
        else:
            offs, kw = (0, DIL_RADIUS, 2 * DIL_RADIUS), 2 * QBLK
        tiles = []
        for off in offs:
            rel = np.abs((off + np.arange(QBLK))[:, None] - np.arange(kw)[None, :])
            tiles.append(np.where(rel <= DIL_RADIUS, rel * dil, np.inf).astype(np.float32))
        dist = np.stack(tiles)
        tab = -(slopes * LOG2E)[:, None, None, None] * dist[None]
        tab = jnp.asarray(tab) - shift
        tables.append(tab[:, 0] if len(offs) == 1 else tab)
    return tables


def _nb_window_start(g, rows_total):
    kh = min(NA_ROWS, rows_total)
    return int(np.clip(g * NB_GROUP - kh // 2, 0, rows_total - (NB_GROUP + kh)))


def _rpb_table(rpb, shift, rows_total):
    kh = min(NA_ROWS, rows_total)
    n_groups = rows_total // NB_GROUP
    win = NB_GROUP + kh
    assert n_groups >= 3 and all(_nb_window_start(g, rows_total) == g * NB_GROUP - kh // 2 for g in range(1, n_groups - 1))
    c = np.arange(GRID_W)
    c0 = np.clip(c - NA_COLS // 2, 0, GRID_W - NA_COLS)
    col_ok = (c[None, :] >= c0[:, None]) & (c[None, :] < c0[:, None] + NA_COLS)
    dc = np.clip(c[None, :] - c[:, None], -(NA_COLS - 1), NA_COLS - 1) + NA_COLS - 1
    n_dc = 2 * NA_COLS - 1
    pick = dc[None, None, :, :, None] == np.arange(n_dc)[None, None, None, None, :]
    cols = jnp.sum(jnp.where(pick, rpb[:, :, None, None, :], 0.0), axis=-1)
    cols = jnp.where(col_ok[None, None], cols, -jnp.inf)
    masked = jnp.full((rpb.shape[0], GRID_W, GRID_W), -jnp.inf, F32)
    types = []
    for g in (0, 1, n_groups - 1):
        ws = _nb_window_start(g, rows_total)
        q_rows = []
        for i in range(NB_GROUP):
            r = g * NB_GROUP + i
            r0 = int(np.clip(r - kh // 2, 0, rows_total - kh))
            blocks = [cols[:, ws + j - r + NA_ROWS - 1] if r0 <= ws + j < r0 + kh else masked for j in range(win)]
            q_rows.append(jnp.concatenate(blocks, axis=2))
        types.append(jnp.concatenate(q_rows, axis=1))
    tab = jnp.stack(types, axis=1) * LOG2E
    return tab - shift[:, None, None, None]


def kernel(x, mem, norm_mix, w_in, qk_gain, rpb, norm_mem, w_mem_kv, out_gain, w_out, norm_ffn, w_group,
           b_group, w_router, b_router, w_gate, w_up, w_down):
    bsz, seq, d = x.shape
    t = bsz * seq
    depth = w_in.shape[0]
    rows_total = seq // GRID_W

    x2 = x.reshape(t, d)
    wg_all = w_gate.astype(BF16).reshape(depth * N_EXPERTS, d, D_EXPERT)
    wu_all = w_up.astype(BF16).reshape(depth * N_EXPERTS, d, D_EXPERT)
    wd_all = w_down.astype(BF16).reshape(depth * N_EXPERTS, D_EXPERT, d)
    moe = None
    for l in range(depth):
        gains = jnp.tile(qk_gain[l], (1, 2))
        col_scale = jnp.concatenate([
            jnp.tile(gains[0] * QSCALE, WIDTH_A // LANES), jnp.tile(gains[1], WIDTH_A // LANES), jnp.ones((WIDTH_A,), F32),
            jnp.tile(gains[2] * QSCALE, WIDTH_B // LANES), jnp.tile(gains[3], WIDTH_B // LANES), jnp.ones((WIDTH_B,), F32),
            jnp.tile(gains[4] * QSCALE, WIDTH_M // LANES)])[None, :]
        w_in_l = w_in[l].astype(BF16)
        if moe is None:
            proj = _proj(x2, None, norm_mix[l][None, :], w_in_l, col_scale)
        else:
            x2, proj = _proj(x2, moe, norm_mix[l][None, :], w_in_l, col_scale)

        shift_a = _score_bound(qk_gain[l, 0], qk_gain[l, 1])
        b1, b4, b16 = _alibi_tables(shift_a)
        oa = _attn_a(proj, seq, b1, b4, b16)

        shift_b = _score_bound(qk_gain[l, 2], qk_gain[l, 3]) + LOG2E * jnp.max(rpb[l], axis=(1, 2))
        ob = _attn_b(proj, seq, _rpb_table(rpb[l], shift_b, rows_total))

        kaug, vaug = _memkv(mem, norm_mem[l][None, :], w_mem_kv[l].astype(BF16), gains[5:6])
        shift_m = jnp.full((1, mem.shape[1]), _score_bound(qk_gain[l, 4], qk_gain[l, 5]), F32)
        om = _attn_m(proj, seq, kaug, vaug, shift_m)

        w_route = jnp.zeros((d, LANES), F32).at[:, :N_EXPERTS].set(w_router[l])
        w_route = w_route.at[:, N_EXPERTS:N_EXPERTS + N_GROUPS].set(w_group[l])
        w_route_hi = w_route.astype(BF16)
        w_route_lo = (w_route - w_route_hi.astype(F32)).astype(BF16)
        b_route = jnp.zeros((1, LANES), F32).at[0, :N_EXPERTS].set(b_router[l])
        b_route = b_route.at[0, N_EXPERTS:N_EXPERTS + N_GROUPS].set(b_group[l])
        x2, h3, route = _mix(oa, ob, om, x2,
                             out_gain[l][None, :], w_out[l].astype(BF16), norm_ffn[l][None, :],
                             jnp.stack([w_route_hi, w_route_lo]), b_route)

        moe = _moe(h3, route, wg_all, wu_all, wd_all, l * N_EXPERTS, MOE_BLOCK_TOKENS)
    return _add(x2, moe).reshape(bsz, seq, d)
```

```python
import functools

import numpy as np
import jax
import jax.numpy as jnp
from jax import lax
from jax.experimental import pallas as pl
from jax.experimental.pallas import tpu as pltpu

F32 = jnp.float32
BF16 = jnp.bfloat16

D_MODEL = 1024
HEAD_DIM = 64
N_HEADS_A = 6
N_HEADS_B = 6
N_HEADS_M = 4
WIDTH_A = N_HEADS_A * HEAD_DIM
WIDTH_B = N_HEADS_B * HEAD_DIM
WIDTH_M = N_HEADS_M * HEAD_DIM
IN_WIDTH = 3 * WIDTH_A + 3 * WIDTH_B + WIDTH_M
DILATED_PATTERNS = ((128, 1), (512, 4), (2048, 16))
DIL_RADIUS = 64
GRID_W = 64
NA_ROWS = 8
NA_COLS = 16
N_GROUPS = 4
EXPERTS_PER_GROUP = 8
N_EXPERTS = N_GROUPS * EXPERTS_PER_GROUP
D_EXPERT = 512
EPS = 1e-6
ATTN_SCALE = HEAD_DIM ** -0.5
LOG2E = float(np.log2(np.e))
QSCALE = ATTN_SCALE * LOG2E

LANES = 128
ROW_TILES = D_MODEL // LANES
QBLK = 128
PROJ_ROWS = 1024
ROUTE_ROWS = 8
MOE_TM = 128
MOE_BLOCK_TOKENS = 4096
VMEM_LIMIT = 56 * 1024 * 1024


def _cparams(sem):
    return pltpu.CompilerParams(dimension_semantics=sem, vmem_limit_bytes=VMEM_LIMIT)


def _from_token_tiles(t_ref):
    rows = t_ref.shape[0] // ROW_TILES
    return jnp.concatenate([t_ref[pl.ds(j, rows, stride=ROW_TILES), :] for j in range(ROW_TILES)], axis=1)


def _qk_slabs():
    a, b, m = WIDTH_A // LANES, WIDTH_B // LANES, WIDTH_M // LANES
    return tuple(range(0, 2 * a)) + tuple(range(3 * a, 3 * a + 2 * b)) + tuple(range(3 * (a + b), 3 * (a + b) + m))


def _proj_kernel(*refs, add):
    if add:
        x_ref, m_ref, g_ref, w_ref, cs_ref, xo_ref, o_ref = refs
        x = x_ref[...] + _from_token_tiles(m_ref)
        xo_ref[...] = x
    else:
        x_ref, g_ref, w_ref, cs_ref, o_ref = refs
        x = x_ref[...]
    ms = jnp.mean(x * x, axis=-1, keepdims=True)
    h = (x * lax.rsqrt(ms + EPS) * g_ref[...]).astype(BF16)
    acc = jnp.dot(h, w_ref[...], preferred_element_type=F32)
    lo = lax.broadcasted_iota(jnp.int32, (x.shape[0], LANES), 1) < HEAD_DIM
    qk = _qk_slabs()
    for s in range(IN_WIDTH // LANES):
        cols = slice(s * LANES, (s + 1) * LANES)
        y = acc[:, cols]
        if s in qk:
            sq = y * y
            ss_lo = jnp.sum(jnp.where(lo, sq, 0.0), axis=-1, keepdims=True)
            ss_hi = jnp.sum(jnp.where(lo, 0.0, sq), axis=-1, keepdims=True)
            inv = jnp.where(lo, lax.rsqrt(ss_lo * (1.0 / HEAD_DIM) + EPS), lax.rsqrt(ss_hi * (1.0 / HEAD_DIM) + EPS))
            y = y * inv * cs_ref[:, cols]
        o_ref[:, cols] = y.astype(BF16)


def _proj(x2, moe2, gain, w, col_scale):
    t = x2.shape[0]
    add = moe2 is not None
    row_spec = pl.BlockSpec((PROJ_ROWS, D_MODEL), lambda i: (i, 0))
    tile_spec = pl.BlockSpec((PROJ_ROWS * ROW_TILES, LANES), lambda i: (i, 0))
    in_specs = [row_spec] + ([tile_spec] if add else []) + [
        pl.BlockSpec((1, D_MODEL), lambda i: (0, 0)),
        pl.BlockSpec((D_MODEL, IN_WIDTH), lambda i: (0, 0)),
        pl.BlockSpec((1, IN_WIDTH), lambda i: (0, 0)),
    ]
    proj_spec = pl.BlockSpec((PROJ_ROWS, IN_WIDTH), lambda i: (i, 0))
    proj_shape = jax.ShapeDtypeStruct((t, IN_WIDTH), BF16)
    if add:
        out_shape = (jax.ShapeDtypeStruct((t, D_MODEL), F32), proj_shape)
        out_specs = (row_spec, proj_spec)
        args = (x2, moe2, gain, w, col_scale)
    else:
        out_shape, out_specs, args = proj_shape, proj_spec, (x2, gain, w, col_scale)
    return pl.pallas_call(
        functools.partial(_proj_kernel, add=add),
        out_shape=out_shape, grid=(t // PROJ_ROWS,), in_specs=in_specs, out_specs=out_specs,
        compiler_params=_cparams(("parallel",)), name="proj_add" if add else "proj",
    )(*args)


def _data_mask(lane, par):
    return (lane < HEAD_DIM) if par == 0 else (lane >= HEAD_DIM)


def _with_ones(v, dm):
    return jnp.where(dm, v, 1.0)


def _head_norm(x, dm, gain):
    ss = jnp.sum(jnp.where(dm, x * x, 0.0), axis=-1, keepdims=True)
    return x * lax.rsqrt(ss * (1.0 / HEAD_DIM) + EPS) * gain


def _normalise(tot):
    return tot / pltpu.roll(tot, HEAD_DIM, axis=1)


def _memkv_kernel(m_ref, g_ref, w_ref, kg_ref, k_ref, v_ref):
    x = m_ref[...]
    ms = jnp.mean(x * x, axis=-1, keepdims=True)
    h = (x * lax.rsqrt(ms + EPS) * g_ref[...]).astype(BF16)
    kv = jnp.dot(h, w_ref[...], preferred_element_type=F32)
    lane = lax.broadcasted_iota(jnp.int32, (x.shape[0], LANES), 1)
    for slab in range(WIDTH_M // LANES):
        kf = kv[:, slab * LANES:(slab + 1) * LANES]
        vf = kv[:, WIDTH_M + slab * LANES:WIDTH_M + (slab + 1) * LANES]
        for par in (0, 1):
            dm = _data_mask(lane, par)
            kn = _head_norm(kf, dm, kg_ref[...])
            k_ref[2 * slab + par] = jnp.where(dm, kn, 0.0).astype(BF16)
            v_ref[2 * slab + par] = _with_ones(vf, dm).astype(BF16)


def _memkv(mem, gain, w, kgain):
    b, m, _ = mem.shape
    kv_shape = jax.ShapeDtypeStruct((b, N_HEADS_M, m, LANES), BF16)
    kv_spec = pl.BlockSpec((None, N_HEADS_M, m, LANES), lambda i: (i, 0, 0, 0))
    return pl.pallas_call(
        _memkv_kernel, out_shape=(kv_shape, kv_shape), grid=(b,),
        in_specs=[pl.BlockSpec((None, m, D_MODEL), lambda i: (i, 0, 0)),
                  pl.BlockSpec((1, D_MODEL), lambda i: (0, 0)),
                  pl.BlockSpec((D_MODEL, 2 * WIDTH_M), lambda i: (0, 0)),
                  pl.BlockSpec((1, LANES), lambda i: (0, 0))],
        out_specs=(kv_spec, kv_spec), compiler_params=_cparams(("parallel",)), name="memkv",
    )(mem, gain, w, kgain)


PREP_ROWS = 256
A_UNROLL = 16


def _attn_a_kernel(q_ref, k_ref, v_ref, b1_ref, b4_ref, b16_ref, o_ref,
                   qs, ks, vs, qm, km, vm, q4, q16, kd, vd, accd, accm, tot, s_scr):
    s_len = q_ref.shape[0]
    n_prep = s_len // PREP_ROWS
    lane_p = lax.broadcasted_iota(jnp.int32, (PREP_ROWS, LANES), 1)
    d_mid, d_far = DILATED_PATTERNS[1][1], DILATED_PATTERNS[2][1]
    len_mid, len_far = s_len // d_mid, s_len // d_far
    sub = d_far // d_mid

    def chunk_rows(c):
        return pl.ds(pl.multiple_of(c * PREP_ROWS, PREP_ROWS), PREP_ROWS)

    def mid_rows(r):
        return pl.ds(r, len_mid, stride=d_mid), pl.ds(pl.multiple_of(r * len_mid, QBLK), len_mid)

    def far_rows(c):
        return (pl.ds((c // sub) * len_mid + c % sub, len_far, stride=sub),
                pl.ds(pl.multiple_of(c * len_far, QBLK), len_far))

    def split_mid(pairs):
        def body(r, carry):
            tok, mid = mid_rows(r)
            for x_tok, x_mid, x_bf in pairs:
                x = x_tok[tok, :]
                x_mid[mid, :] = x
                x_bf[mid, :] = x.astype(BF16)
            return carry
        lax.fori_loop(0, d_mid, body, 0)

    def split_far(pairs):
        def body(c, carry):
            mid, far = far_rows(c)
            for x_mid, x_bf in pairs:
                x_bf[far, :] = x_mid[mid, :].astype(BF16)
            return carry
        lax.fori_loop(0, d_far, body, 0)

    def q_cast(c, carry):
        qs[chunk_rows(c), :] = q_ref[chunk_rows(c), :].astype(F32)
        return carry
    lax.fori_loop(0, n_prep, q_cast, 0)
    split_mid(((qs, qm, q4),))
    split_far(((qm, q16),))

    for par in (0, 1):
        def prep(c, carry, par=par):
            rows = chunk_rows(c)
            dm = _data_mask(lane_p, par)
            kf = jnp.where(dm, k_ref[rows, :].astype(F32), 0.0)
            vf = _with_ones(v_ref[rows, :].astype(F32), dm)
            ks[rows, :] = kf
            vs[rows, :] = vf
            kd[rows, :] = kf.astype(BF16)
            vd[rows, :] = vf.astype(BF16)
            return carry
        lax.fori_loop(0, n_prep, prep, 0)

        def run_blocks(q_src, length, b_ref, out, par=par):
            nblk = length // QBLK

            def blocks(o, carry):
                units = []
                for i in range(A_UNROLL):
                    u = o * A_UNROLL + i
                    if nblk == 1:
                        base = pl.multiple_of(u * length, QBLK)
                        qrows = pl.ds(base, QBLK)
                        keys = pl.ds(base, QBLK)
                        bias = b_ref[par]
                        kw = QBLK
                    else:
                        r = u // nblk
                        blk = u % nblk
                        q0 = blk * QBLK
                        ws = jnp.clip(q0 - DIL_RADIUS, 0, length - 2 * QBLK)
                        tid = jnp.where(blk == 0, 0, jnp.where(blk == nblk - 1, 2, 1))
                        qrows = pl.ds(pl.multiple_of(r * length + q0, QBLK), QBLK)
                        keys = pl.ds(pl.multiple_of(r * length + ws, DIL_RADIUS), 2 * QBLK)
                        bias = b_ref[par, tid]
                        kw = 2 * QBLK
                    s_scr[i, :, 0:kw] = lax.dot_general(q_src[qrows, :], kd[keys, :], (((1,), (1,)), ((), ())),
                                                        preferred_element_type=F32)
                    units.append((qrows, keys, bias, kw))
                for i, (qrows, keys, bias, kw) in enumerate(units):
                    p = jnp.exp2(s_scr[i, :, 0:kw] + bias).astype(BF16)
                    out[qrows, :] = jnp.dot(p, vd[keys, :], preferred_element_type=F32)
                return carry
            lax.fori_loop(0, s_len // QBLK // A_UNROLL, blocks, 0)

        run_blocks(q_ref, s_len, b1_ref, tot.at[par])

        split_mid(((ks, km, kd), (vs, vm, vd)))
        run_blocks(q4, len_mid, b4_ref, accm)
        split_far(((km, kd), (vm, vd)))
        run_blocks(q16, len_far, b16_ref, accd)

        def merge_far(c, carry):
            mid, far = far_rows(c)
            accm[mid, :] = accm[mid, :] + accd[far, :]
            return carry
        lax.fori_loop(0, d_far, merge_far, 0)

        def merge_mid(r, carry, par=par):
            tok, mid = mid_rows(r)
            tot[par, tok, :] = tot[par, tok, :] + accm[mid, :]
            return carry
        lax.fori_loop(0, d_mid, merge_mid, 0)

    def fin(c, carry):
        rows = chunk_rows(c)
        o0 = _normalise(tot[0, rows, :])
        o1 = _normalise(tot[1, rows, :])
        o_ref[rows, :] = jnp.where(lane_p < HEAD_DIM, o0, o1).astype(BF16)
        return carry
    lax.fori_loop(0, n_prep, fin, 0)


def _attn_a(proj, s, b1, b4, b16):
    b = proj.shape[0] // s
    n_pairs = WIDTH_A // LANES

    def slab(off):
        return pl.BlockSpec((s, LANES), lambda i, p: (i, off + p))
    return pl.pallas_call(
        _attn_a_kernel, out_shape=jax.ShapeDtypeStruct((b * s, WIDTH_A), BF16), grid=(b, n_pairs),
        in_specs=[slab(0), slab(n_pairs), slab(2 * n_pairs),
                  pl.BlockSpec((2, 3, QBLK, 2 * QBLK), lambda i, p: (p, 0, 0, 0)),
                  pl.BlockSpec((2, 3, QBLK, 2 * QBLK), lambda i, p: (p, 0, 0, 0)),
                  pl.BlockSpec((2, QBLK, QBLK), lambda i, p: (p, 0, 0))],
        out_specs=pl.BlockSpec((s, LANES), lambda i, p: (i, p)),
        scratch_shapes=[pltpu.VMEM((s, LANES), F32)] * 6 + [pltpu.VMEM((s, LANES), BF16)] * 4
        + [pltpu.VMEM((s, LANES), F32)] * 2
        + [pltpu.VMEM((2, s, LANES), F32), pltpu.VMEM((A_UNROLL, QBLK, 2 * QBLK), F32)],
        compiler_params=_cparams(("parallel", "parallel")), name="attn_a",
    )(proj, proj, proj, b1, b4, b16)


NB_GROUP = 4
NB_UNROLL = 8


def _attn_b_kernel(q_ref, k_ref, v_ref, bias_ref, o_ref, kb, vb, s_scr):
    s_len = q_ref.shape[0]
    n_prep = s_len // PREP_ROWS
    rows_total = s_len // GRID_W
    kh = min(NA_ROWS, rows_total)
    n_groups = rows_total // NB_GROUP
    q_len = NB_GROUP * GRID_W
    k_len = (NB_GROUP + kh) * GRID_W
    lane_p = lax.broadcasted_iota(jnp.int32, (PREP_ROWS, LANES), 1)
    lane_q = lax.broadcasted_iota(jnp.int32, (q_len, LANES), 1)

    def prep(c, carry):
        rows = pl.ds(pl.multiple_of(c * PREP_ROWS, PREP_ROWS), PREP_ROWS)
        kf = k_ref[rows, :].astype(F32)
        vf = v_ref[rows, :].astype(F32)
        for par in (0, 1):
            dm = _data_mask(lane_p, par)
            kb[par, rows, :] = jnp.where(dm, kf, 0.0).astype(BF16)
            vb[par, rows, :] = _with_ones(vf, dm).astype(BF16)
        return carry
    lax.fori_loop(0, n_prep, prep, 0)

    def groups(o, carry):
        units = []
        for i in range(NB_UNROLL):
            g = o * NB_UNROLL + i
            ws = jnp.clip(g * NB_GROUP - kh // 2, 0, rows_total - (NB_GROUP + kh))
            typ = jnp.where(g == 0, 0, jnp.where(g == n_groups - 1, 2, 1))
            qrows = pl.ds(pl.multiple_of(g * q_len, q_len), q_len)
            keys = pl.ds(pl.multiple_of(ws * GRID_W, GRID_W), k_len)
            for par in (0, 1):
                s_scr[2 * i + par] = lax.dot_general(q_ref[qrows, :], kb[par, keys, :], (((1,), (1,)), ((), ())),
                                                     preferred_element_type=F32)
            units.append((qrows, keys, typ))
        for i, (qrows, keys, typ) in enumerate(units):
            outs = []
            for par in (0, 1):
                p = jnp.exp2(s_scr[2 * i + par] + bias_ref[par, typ]).astype(BF16)
                acc = jnp.dot(p, vb[par, keys, :], preferred_element_type=F32)
                outs.append(_normalise(acc))
            o_ref[qrows, :] = jnp.where(lane_q < HEAD_DIM, outs[0], outs[1]).astype(BF16)
        return carry
    lax.fori_loop(0, n_groups // NB_UNROLL, groups, 0)


def _attn_b(proj, s, bias):
    b = proj.shape[0] // s
    n_pairs = WIDTH_B // LANES
    first = 3 * WIDTH_A // LANES
    n_types, q_len, k_len = bias.shape[1:]

    def slab(off):
        return pl.BlockSpec((s, LANES), lambda i, p: (i, first + off + p))
    return pl.pallas_call(
        _attn_b_kernel, out_shape=jax.ShapeDtypeStruct((b * s, WIDTH_B), BF16), grid=(b, n_pairs),
        in_specs=[slab(0), slab(n_pairs), slab(2 * n_pairs),
                  pl.BlockSpec((2, n_types, q_len, k_len), lambda i, p: (p, 0, 0, 0))],
        out_specs=pl.BlockSpec((s, LANES), lambda i, p: (i, p)),
        scratch_shapes=[pltpu.VMEM((2, s, LANES), BF16)] * 2 + [pltpu.VMEM((2 * NB_UNROLL, q_len, k_len), F32)],
        compiler_params=_cparams(("parallel", "parallel")), name="attn_b",
    )(proj, proj, proj, bias)


MEM_QROWS = 256
MEM_UNROLL = 8


def _attn_m_kernel(q_ref, k_ref, v_ref, shift_ref, o_ref, s_scr):
    s_len = q_ref.shape[0]
    lane = lax.broadcasted_iota(jnp.int32, (MEM_QROWS, LANES), 1)

    def chunks(o, carry):
        all_rows = []
        for i in range(MEM_UNROLL):
            rows = pl.ds(pl.multiple_of((o * MEM_UNROLL + i) * MEM_QROWS, MEM_QROWS), MEM_QROWS)
            for par in (0, 1):
                s_scr[2 * i + par] = lax.dot_general(q_ref[rows, :], k_ref[par], (((1,), (1,)), ((), ())),
                                                     preferred_element_type=F32)
            all_rows.append(rows)
        for i, rows in enumerate(all_rows):
            outs = []
            for par in (0, 1):
                p = jnp.exp2(s_scr[2 * i + par] - shift_ref[...]).astype(BF16)
                acc = jnp.dot(p, v_ref[par], preferred_element_type=F32)
                outs.append(_normalise(acc))
            o_ref[rows, :] = jnp.where(lane < HEAD_DIM, outs[0], outs[1]).astype(BF16)
        return carry
    lax.fori_loop(0, s_len // MEM_QROWS // MEM_UNROLL, chunks, 0)


def _attn_m(proj, s, kaug, vaug, shift):
    b = proj.shape[0] // s
    m = kaug.shape[2]
    n_pairs = WIDTH_M // LANES
    first = 3 * (WIDTH_A + WIDTH_B) // LANES
    kv_spec = pl.BlockSpec((None, 2, m, LANES), lambda i, p: (i, p, 0, 0))
    return pl.pallas_call(
        _attn_m_kernel, out_shape=jax.ShapeDtypeStruct((b * s, WIDTH_M), BF16), grid=(b, n_pairs),
        in_specs=[pl.BlockSpec((s, LANES), lambda i, p: (i, first + p)), kv_spec, kv_spec,
                  pl.BlockSpec((1, m), lambda i, p: (0, 0))],
        out_specs=pl.BlockSpec((s, LANES), lambda i, p: (i, p)),
        scratch_shapes=[pltpu.VMEM((2 * MEM_UNROLL, MEM_QROWS, m), F32)],
        compiler_params=_cparams(("parallel", "parallel")), name="attn_m",
    )(proj, kaug, vaug, shift)


def _mix_kernel(oa_ref, ob_ref, om_ref, x_ref, og_ref, wo_ref, gf_ref, wr_ref, br_ref,
                x1_ref, h3_ref, route_ref):
    rows = x_ref.shape[0]
    y = None
    off = 0
    for o_ref in (oa_ref, ob_ref, om_ref):
        width = o_ref.shape[1]
        o = o_ref[...].astype(F32)
        ms = jnp.mean(o * o, axis=-1, keepdims=True)
        mixed = (o * lax.rsqrt(ms + EPS) * og_ref[:, off:off + width]).astype(BF16)
        part = jnp.dot(mixed, wo_ref[off:off + width, :], preferred_element_type=F32)
        y = part if y is None else y + part
        off += width
    x1 = x_ref[...] + y
    x1_ref[...] = x1
    ms = jnp.mean(x1 * x1, axis=-1, keepdims=True)
    h = x1 * lax.rsqrt(ms + EPS) * gf_ref[...]
    for j in range(ROW_TILES):
        h3_ref[pl.ds(j, rows, stride=ROW_TILES), :] = h[:, j * LANES:(j + 1) * LANES]

    h_hi = h.astype(BF16)
    h_lo = (h - h_hi.astype(F32)).astype(BF16)
    both = jnp.dot(h_hi, wr_ref[...], preferred_element_type=F32)
    logits = (both[:, 0:LANES] + both[:, LANES:2 * LANES]
              + jnp.dot(h_lo, wr_ref[:, 0:LANES], preferred_element_type=F32)) + br_ref[...]
    lane = lax.broadcasted_iota(jnp.int32, (rows, LANES), 1)
    lane_f = lane.astype(F32)
    neg = -jnp.inf
    far = float(LANES)
    group_shift = N_GROUPS.bit_length() - 1
    member_shift = EXPERTS_PER_GROUP.bit_length() - 1
    is_g = (lane >> group_shift) == (N_EXPERTS >> group_shift)
    gl = jnp.where(is_g, logits, neg)
    gmax = jnp.max(gl, axis=-1, keepdims=True)
    gidx = jnp.min(jnp.where(gl == gmax, lane_f, far), axis=-1, keepdims=True) - float(N_EXPERTS)
    gsum = jnp.sum(jnp.where(is_g, jnp.exp(gl - gmax), 0.0), axis=-1, keepdims=True)
    g_top = 1.0 / gsum
    in_grp = (lane >> member_shift).astype(F32) == gidx
    el = jnp.where(in_grp, logits, neg)
    v1 = jnp.max(el, axis=-1, keepdims=True)
    i1 = jnp.min(jnp.where(el == v1, lane_f, far), axis=-1, keepdims=True)
    el2 = jnp.where(lane_f == i1, neg, el)
    v2 = jnp.max(el2, axis=-1, keepdims=True)
    i2 = jnp.min(jnp.where(el2 == v2, lane_f, far), axis=-1, keepdims=True)
    ev = jnp.exp(v2 - v1)
    w1 = g_top / (1.0 + ev)
    w2 = g_top * ev / (1.0 + ev)
    route = jnp.where(lane == 0, i1, jnp.where(lane == 1, i2, jnp.where(lane == 2, w1,
                      jnp.where(lane == 3, w2, 0.0))))
    per_token = jnp.transpose(route)[0:4, :]
    chosen = jnp.where(lane_f == i1, 1.0, jnp.where(lane_f == i2, 1.0, 0.0))
    counts = jnp.sum(chosen, axis=0, keepdims=True)
    route_ref[...] = jnp.zeros_like(route_ref)
    route_ref[0:4, :] = per_token
    route_ref[4:5, 0:LANES] = counts


def _mix(oa, ob, om, x2, out_gain, w_out, gain_ffn, w_route, b_route):
    t = x2.shape[0]

    def rows(width):
        return pl.BlockSpec((PROJ_ROWS, width), lambda i: (i, 0))

    def whole(*shape):
        return pl.BlockSpec(shape, lambda i: (0,) * len(shape))
    return pl.pallas_call(
        _mix_kernel,
        out_shape=(jax.ShapeDtypeStruct((t, D_MODEL), F32),
                   jax.ShapeDtypeStruct((t * ROW_TILES, LANES), F32),
                   jax.ShapeDtypeStruct((t // PROJ_ROWS * ROUTE_ROWS, PROJ_ROWS), F32)),
        grid=(t // PROJ_ROWS,),
        in_specs=[rows(WIDTH_A), rows(WIDTH_B), rows(WIDTH_M), rows(D_MODEL),
                  whole(1, D_MODEL), whole(D_MODEL, D_MODEL), whole(1, D_MODEL),
                  whole(D_MODEL, 2 * LANES), whole(1, LANES)],
        out_specs=(rows(D_MODEL), pl.BlockSpec((PROJ_ROWS * ROW_TILES, LANES), lambda i: (i, 0)),
                   pl.BlockSpec((ROUTE_ROWS, PROJ_ROWS), lambda i: (i, 0))),
        compiler_params=_cparams(("parallel",)), name="mix",
    )(oa, ob, om, x2, out_gain, w_out, gain_ffn, w_route, b_route)


SCATTER_UNROLL = 8
W_SLOTS = 4
W_CHUNK_ROWS = 512


def _moe_weight_copies(wg_hbm, wu_hbm, wd_hbm, wgu_buf, wd_buf, sems, first_expert, e, slot):
    copies = []
    targets = ((wg_hbm, wgu_buf, pl.ds(0, D_EXPERT)), (wu_hbm, wgu_buf, pl.ds(D_EXPERT, D_EXPERT)),
               (wd_hbm, wd_buf, pl.ds(0, D_MODEL)))
    for hbm, buf, cols in targets:
        for c in range(hbm.shape[1] // W_CHUNK_ROWS):
            rows = pl.ds(c * W_CHUNK_ROWS, W_CHUNK_ROWS)
            copies.append(pltpu.make_async_copy(hbm.at[first_expert + e, rows], buf.at[slot, rows, cols],
                                                sems.at[slot]))
    return copies


def _moe_kernel(rstart_ref, tslot_ref, tfirst_ref, tnext_ref, npair_ref, act_ref,
                src_ref, dst_ref, gate_ref, h3_ref, wg_hbm, wu_hbm, wd_hbm, o_hbm,
                wgu_buf, wd_buf, sems, out_sem, xt0, xt1, lhs0, lhs1, yt0, yt1, acc, *, first_expert):
    b = pl.program_id(0)
    n_rows = h3_ref.shape[0]
    tiles_per_block = 2 * (n_rows // ROW_TILES) // MOE_TM + N_EXPERTS + 2
    pad_rows = src_ref.shape[1] - MOE_TM
    base = b * tiles_per_block
    bufs = ((xt0, lhs0, yt0), (xt1, lhs1, yt1))
    copies = functools.partial(_moe_weight_copies, wg_hbm, wu_hbm, wd_hbm, wgu_buf, wd_buf, sems, first_expert)

    def gather(rs, xt):
        for i in range(MOE_TM):
            off = pl.multiple_of(src_ref[0, rs + i], ROW_TILES)
            xt[i * ROW_TILES:(i + 1) * ROW_TILES, :] = h3_ref[pl.ds(off, ROW_TILES), :]

    def scatter(rs, yt):
        for i0 in range(0, MOE_TM, SCATTER_UNROLL):
            vals = []
            for i in range(i0, i0 + SCATTER_UNROLL):
                rows = pl.ds(pl.multiple_of(dst_ref[0, rs + i], ROW_TILES), ROW_TILES)
                vals.append((rows, acc[rows, :] + gate_ref[0, rs + i] * yt[i * ROW_TILES:(i + 1) * ROW_TILES, :]))
            for rows, val in vals:
                acc[rows, :] = val

    for k in range(W_SLOTS - 1):
        e0 = act_ref[b * (W_SLOTS - 1) + k]

        @pl.when(e0 >= 0)
        def _(e0=e0, k=k):
            for c in copies(e0, k):
                c.start()

    def out_copy(block):
        return pltpu.make_async_copy(acc.at[pl.ds(0, n_rows)],
                                     o_hbm.at[pl.ds(pl.multiple_of(block * n_rows, ROW_TILES), n_rows)], out_sem)

    @pl.when(b > 0)
    def _():
        out_copy(b - 1).wait()
    acc[...] = jnp.zeros_like(acc)
    for half, (xt, _, yt) in enumerate(bufs):
        yt[...] = jnp.zeros_like(yt)
        gather(rstart_ref[base + half], xt)

    def pair(p, carry):
        tiles = []
        for half in (0, 1):
            idx = base + 2 * p + half
            slot = tslot_ref[idx]

            @pl.when(tfirst_ref[idx] == 1)
            def _(slot=slot):
                for c in copies(0, slot):
                    c.wait()
            prev_rs = jnp.where(p > 0, rstart_ref[jnp.maximum(idx - 2, 0)], pad_rows)
            tiles.append((prev_rs, rstart_ref[idx + 2], slot, tnext_ref[idx]))

        for (prev_rs, _, _, _), (_, _, yt) in zip(tiles, bufs):
            scatter(prev_rs, yt)
        for xt, lhs, _ in bufs:
            for k in range(ROW_TILES):
                lhs[:, k * LANES:(k + 1) * LANES] = xt[pl.ds(k, MOE_TM, stride=ROW_TILES), :].astype(BF16)
        for (_, next_rs, _, _), (xt, _, _) in zip(tiles, bufs):
            gather(next_rs, xt)
        for (_, _, slot, _), (_, lhs, yt) in zip(tiles, bufs):
            gu = jnp.dot(lhs[...], wgu_buf[slot], preferred_element_type=F32)
            g = gu[:, :D_EXPERT]
            hid = (g * (1.0 / (1.0 + jnp.exp(-g)))) * gu[:, D_EXPERT:]
            y = jnp.dot(hid.astype(BF16), wd_buf[slot], preferred_element_type=F32)
            for k in range(ROW_TILES):
                yt[pl.ds(k, MOE_TM, stride=ROW_TILES), :] = y[:, k * LANES:(k + 1) * LANES]

        for _, _, slot, nxt in tiles:
            @pl.when(nxt >= 0)
            def _(slot=slot, nxt=nxt):
                for c in copies(nxt, (slot + W_SLOTS - 1) % W_SLOTS):
                    c.start()
        return carry
    n_pairs = npair_ref[b]
    lax.fori_loop(0, n_pairs, pair, 0)
    for half, (_, _, yt) in enumerate(bufs):
        scatter(rstart_ref[base + 2 * (n_pairs - 1) + half], yt)

    out_copy(b).start()

    @pl.when(b == pl.num_programs(0) - 1)
    def _():
        out_copy(b).wait()


def _pick(table, index):
    cols = jnp.arange(table.shape[1], dtype=jnp.int32)
    return jnp.sum(jnp.where(index[:, :, None] == cols[None, None, :], table[:, None, :], 0), axis=2)


def _moe(h3, route_t, wg, wu, wd, first_expert, block_tokens):
    t = h3.shape[0] // ROW_TILES
    nblk = t // block_tokens
    n_assign = 2 * block_tokens
    nt_max = n_assign // MOE_TM + N_EXPERTS + 2

    n_pad = MOE_TM - 1
    stride = 2 * n_assign
    n_list = n_assign + N_EXPERTS * MOE_TM
    rec = route_t.reshape(nblk, block_tokens // PROJ_ROWS, ROUTE_ROWS, PROJ_ROWS)
    e = rec[:, :, 0:2, :].astype(jnp.int32).reshape(nblk, n_assign)
    w = rec[:, :, 2:4, :].reshape(nblk, n_assign)
    cnt = jnp.sum(rec[:, :, 4, :N_EXPERTS], axis=1).astype(jnp.int32)
    experts = jnp.arange(N_EXPERTS, dtype=jnp.int32)
    key = e * stride + jnp.arange(n_assign, dtype=jnp.int32)[None, :]
    pad_key = (experts[:, None] * stride + n_assign + jnp.arange(n_pad, dtype=jnp.int32)[None, :]).reshape(1, -1)
    tail_key = (N_EXPERTS * stride + n_assign + experts)[None, :]
    extra_key = jnp.broadcast_to(jnp.concatenate([pad_key, tail_key], axis=1), (nblk, n_list - n_assign))
    key_s, w_s = lax.sort((jnp.concatenate([key, extra_key], axis=1),
                           jnp.concatenate([w, jnp.zeros((nblk, n_list - n_assign), F32)], axis=1)),
                          dimension=1, num_keys=1)
    idx_s = key_s & (stride - 1)
    real = idx_s < n_assign
    row_off = ((idx_s // (2 * PROJ_ROWS)) * PROJ_ROWS + idx_s % PROJ_ROWS) * ROW_TILES
    src_s = jnp.where(real, row_off, 0)
    dst_s = jnp.where(real, row_off, block_tokens * ROW_TILES)
    cstart = jnp.cumsum(cnt, axis=1) - cnt + experts[None, :] * n_pad
    ntile_e = (cnt + MOE_TM - 1) // MOE_TM
    tend = jnp.cumsum(ntile_e, axis=1)
    ntiles = tend[:, -1:]
    active = (ntile_e > 0).astype(jnp.int32)
    order_e = jnp.cumsum(active, axis=1) - active
    ranks = jnp.arange(N_EXPERTS + W_SLOTS, dtype=jnp.int32)
    hit = (order_e[:, None, :] == ranks[None, :, None]) & (active[:, None, :] == 1)
    act = jnp.sum(jnp.where(hit, experts[None, None, :] + 1, 0), axis=2) - 1
    tile_ids = jnp.arange(nt_max, dtype=jnp.int32)[None, :]
    live = tile_ids < ntiles
    texp = jnp.minimum(jnp.sum((tile_ids[:, :, None] >= tend[:, None, :]).astype(jnp.int32), axis=2), N_EXPERTS - 1)
    texp = jnp.where(live, texp, _pick(texp, jnp.maximum(ntiles - 1, 0)))
    within = tile_ids - _pick(tend - ntile_e, texp)
    rstart = jnp.where(live, _pick(cstart, texp) + within * MOE_TM, n_list - MOE_TM)
    torder = _pick(order_e, texp)
    tfirst = (live & (within == 0)).astype(jnp.int32)
    tnext = jnp.where(tfirst == 1, _pick(act, torder + (W_SLOTS - 1)), -1)
    npairs = (ntiles[:, 0] + 1) // 2

    def flat(x):
        return x.reshape(-1).astype(jnp.int32)

    def smem_list():
        return pl.BlockSpec((None, 1, n_list), lambda b, *_: (b, 0, 0), memory_space=pltpu.SMEM)
    tile_buf = pltpu.VMEM((MOE_TM * ROW_TILES, LANES), F32)
    grid_spec = pltpu.PrefetchScalarGridSpec(
        num_scalar_prefetch=6, grid=(nblk,),
        in_specs=[smem_list(), smem_list(), smem_list(),
                  pl.BlockSpec((block_tokens * ROW_TILES, LANES), lambda b, *_: (b, 0),
                               pipeline_mode=pl.Buffered(1)),
                  pl.BlockSpec(memory_space=pl.ANY), pl.BlockSpec(memory_space=pl.ANY),
                  pl.BlockSpec(memory_space=pl.ANY)],
        out_specs=pl.BlockSpec(memory_space=pl.ANY),
        scratch_shapes=[pltpu.VMEM((W_SLOTS, D_MODEL, 2 * D_EXPERT), BF16),
                        pltpu.VMEM((W_SLOTS, D_EXPERT, D_MODEL), BF16),
                        pltpu.SemaphoreType.DMA((W_SLOTS,)), pltpu.SemaphoreType.DMA(()),
                        tile_buf, tile_buf,
                        pltpu.VMEM((MOE_TM, D_MODEL), BF16), pltpu.VMEM((MOE_TM, D_MODEL), BF16),
                        tile_buf, tile_buf,
                        pltpu.VMEM(((block_tokens + 1) * ROW_TILES, LANES), F32)])
    return pl.pallas_call(
        functools.partial(_moe_kernel, first_expert=first_expert),
        out_shape=jax.ShapeDtypeStruct((t * ROW_TILES, LANES), F32), grid_spec=grid_spec,
        compiler_params=_cparams(("arbitrary",)), name="moe",
    )(flat(rstart), flat(torder % W_SLOTS), flat(tfirst), flat(tnext), flat(npairs), flat(act[:, :W_SLOTS - 1]),
      src_s.reshape(nblk, 1, n_list), dst_s.reshape(nblk, 1, n_list), w_s.reshape(nblk, 1, n_list), h3, wg, wu, wd)


def _add_kernel(a_ref, b_ref, o_ref):
    o_ref[...] = a_ref[...] + _from_token_tiles(b_ref)


def _add(a, b_tiles):
    t = a.shape[0]
    spec = pl.BlockSpec((PROJ_ROWS, D_MODEL), lambda i: (i, 0))
    tile_spec = pl.BlockSpec((PROJ_ROWS * ROW_TILES, LANES), lambda i: (i, 0))
    return pl.pallas_call(_add_kernel, out_shape=jax.ShapeDtypeStruct(a.shape, a.dtype), grid=(t // PROJ_ROWS,),
                          in_specs=[spec, tile_spec], out_specs=spec, compiler_params=_cparams(("parallel",)),
                          name="residual_add")(a, b_tiles)


def _score_bound(gq, gk):
    return (HEAD_DIM * ATTN_SCALE * LOG2E) * jnp.max(jnp.abs(gq)) * jnp.max(jnp.abs(gk))


def _alibi_tables(shift):
    slopes = (2.0 ** (-8.0 * np.arange(1, N_HEADS_A + 1) / N_HEADS_A)).astype(np.float32)
    tables = []
    for _, dil in DILATED_PATTERNS:
        if dil == DILATED_PATTERNS[-1][1]:
            offs, kw = (0,), QBLK
        else:
            offs, kw = (0, DIL_RADIUS, 2 * DIL_RADIUS), 2 * QBLK
        tiles = []
        for off in offs:
            rel = np.abs((off + np.arange(QBLK))[:, None] - np.arange(kw)[None, :])
            tiles.append(np.where(rel <= DIL_RADIUS, rel * dil, np.inf).astype(np.float32))
        dist = np.stack(tiles)
        tab = -(slopes * LOG2E)[:, None, None, None] * dist[None]
        tab = jnp.asarray(tab) - shift
        tables.append(tab[:, 0] if len(offs) == 1 else tab)
    return tables


def _nb_window_start(g, rows_total):
    kh = min(NA_ROWS, rows_total)
    return int(np.clip(g * NB_GROUP - kh // 2, 0, rows_total - (NB_GROUP + kh)))


def _rpb_table(rpb, shift, rows_total):
    kh = min(NA_ROWS, rows_total)
    n_groups = rows_total // NB_GROUP
    win = NB_GROUP + kh
    assert n_groups >= 3 and all(_nb_window_start(g, rows_total) == g * NB_GROUP - kh // 2 for g in range(1, n_groups - 1))
    c = np.arange(GRID_W)
    c0 = np.clip(c - NA_COLS // 2, 0, GRID_W - NA_COLS)
    col_ok = (c[None, :] >= c0[:, None]) & (c[None, :] < c0[:, None] + NA_COLS)
    dc = np.clip(c[None, :] - c[:, None], -(NA_COLS - 1), NA_COLS - 1) + NA_COLS - 1
    n_dc = 2 * NA_COLS - 1
    pick = dc[None, None, :, :, None] == np.arange(n_dc)[None, None, None, None, :]
    cols = jnp.sum(jnp.where(pick, rpb[:, :, None, None, :], 0.0), axis=-1)
    cols = jnp.where(col_ok[None, None], cols, -jnp.inf)
    masked = jnp.full((rpb.shape[0], GRID_W, GRID_W), -jnp.inf, F32)
    types = []
    for g in (0, 1, n_groups - 1):
        ws = _nb_window_start(g, rows_total)
        q_rows = []
        for i in range(NB_GROUP):
            r = g * NB_GROUP + i
            r0 = int(np.clip(r - kh // 2, 0, rows_total - kh))
            blocks = [cols[:, ws + j - r + NA_ROWS - 1] if r0 <= ws + j < r0 + kh else masked for j in range(win)]
            q_rows.append(jnp.concatenate(blocks, axis=2))
        types.append(jnp.concatenate(q_rows, axis=1))
    tab = jnp.stack(types, axis=1) * LOG2E
    return tab - shift[:, None, None, None]


def kernel(x, mem, norm_mix, w_in, qk_gain, rpb, norm_mem, w_mem_kv, out_gain, w_out, norm_ffn, w_group,
           b_group, w_router, b_router, w_gate, w_up, w_down):
    bsz, seq, d = x.shape
    t = bsz * seq
    depth = w_in.shape[0]
    rows_total = seq // GRID_W

    x2 = x.reshape(t, d)
    wg_all = w_gate.astype(BF16).reshape(depth * N_EXPERTS, d, D_EXPERT)
    wu_all = w_up.astype(BF16).reshape(depth * N_EXPERTS, d, D_EXPERT)
    wd_all = w_down.astype(BF16).reshape(depth * N_EXPERTS, D_EXPERT, d)
    moe = None
    for l in range(depth):
        gains = jnp.tile(qk_gain[l], (1, 2))
        col_scale = jnp.concatenate([
            jnp.tile(gains[0] * QSCALE, WIDTH_A // LANES), jnp.tile(gains[1], WIDTH_A // LANES), jnp.ones((WIDTH_A,), F32),
            jnp.tile(gains[2] * QSCALE, WIDTH_B // LANES), jnp.tile(gains[3], WIDTH_B // LANES), jnp.ones((WIDTH_B,), F32),
            jnp.tile(gains[4] * QSCALE, WIDTH_M // LANES)])[None, :]
        w_in_l = w_in[l].astype(BF16)
        if moe is None:
            proj = _proj(x2, None, norm_mix[l][None, :], w_in_l, col_scale)
        else:
            x2, proj = _proj(x2, moe, norm_mix[l][None, :], w_in_l, col_scale)

        shift_a = _score_bound(qk_gain[l, 0], qk_gain[l, 1])
        b1, b4, b16 = _alibi_tables(shift_a)
        oa = _attn_a(proj, seq, b1, b4, b16)

        shift_b = _score_bound(qk_gain[l, 2], qk_gain[l, 3]) + LOG2E * jnp.max(rpb[l], axis=(1, 2))
        ob = _attn_b(proj, seq, _rpb_table(rpb[l], shift_b, rows_total))

        kaug, vaug = _memkv(mem, norm_mem[l][None, :], w_mem_kv[l].astype(BF16), gains[5:6])
        shift_m = jnp.full((1, mem.shape[1]), _score_bound(qk_gain[l, 4], qk_gain[l, 5]), F32)
        om = _attn_m(proj, seq, kaug, vaug, shift_m)

        w_route = jnp.zeros((d, LANES), F32).at[:, :N_EXPERTS].set(w_router[l])
        w_route = w_route.at[:, N_EXPERTS:N_EXPERTS + N_GROUPS].set(w_group[l])
        w_route_hi = w_route.astype(BF16)
        w_route_lo = (w_route - w_route_hi.astype(F32)).astype(BF16)
        b_route = jnp.zeros((1, LANES), F32).at[0, :N_EXPERTS].set(b_router[l])
        b_route = b_route.at[0, N_EXPERTS:N_EXPERTS + N_GROUPS].set(b_group[l])
        x2, h3, route = _mix(oa, ob, om, x2,
                             out_gain[l][None, :], w_out[l].astype(BF16), norm_ffn[l][None, :],
                             jnp.concatenate([w_route_hi, w_route_lo], axis=1), b_route)

        moe = _moe(h3, route, wg_all, wu_all, wd_all, l * N_EXPERTS, MOE_BLOCK_TOKENS)
    return _add(x2, moe).reshape(bsz, seq, d)
```

```python
import functools

import numpy as np
import jax
import jax.numpy as jnp
from jax import lax
from jax.experimental import pallas as pl
from jax.experimental.pallas import tpu as pltpu

F32 = jnp.float32
BF16 = jnp.bfloat16

D_MODEL = 1024
HEAD_DIM = 64
N_HEADS_A = 6
N_HEADS_B = 6
N_HEADS_M = 4
WIDTH_A = N_HEADS_A * HEAD_DIM
WIDTH_B = N_HEADS_B * HEAD_DIM
WIDTH_M = N_HEADS_M * HEAD_DIM
IN_WIDTH = 3 * WIDTH_A + 3 * WIDTH_B + WIDTH_M
DILATED_PATTERNS = ((128, 1), (512, 4), (2048, 16))
DIL_RADIUS = 64
GRID_W = 64
NA_ROWS = 8
NA_COLS = 16
N_GROUPS = 4
EXPERTS_PER_GROUP = 8
N_EXPERTS = N_GROUPS * EXPERTS_PER_GROUP
D_EXPERT = 512
EPS = 1e-6
ATTN_SCALE = HEAD_DIM ** -0.5
LOG2E = float(np.log2(np.e))
QSCALE = ATTN_SCALE * LOG2E

LANES = 128
ROW_TILES = D_MODEL // LANES
QBLK = 128
PROJ_ROWS = 1024
ROUTE_ROWS = 8
MOE_TM = 128
MOE_BLOCK_TOKENS = 4096
VMEM_LIMIT = 56 * 1024 * 1024


def _cparams(sem):
    return pltpu.CompilerParams(dimension_semantics=sem, vmem_limit_bytes=VMEM_LIMIT)


def _from_token_tiles(t_ref):
    rows = t_ref.shape[0] // ROW_TILES
    return jnp.concatenate([t_ref[pl.ds(j, rows, stride=ROW_TILES), :] for j in range(ROW_TILES)], axis=1)


def _qk_slabs():
    a, b, m = WIDTH_A // LANES, WIDTH_B // LANES, WIDTH_M // LANES
    return tuple(range(0, 2 * a)) + tuple(range(3 * a, 3 * a + 2 * b)) + tuple(range(3 * (a + b), 3 * (a + b) + m))


def _proj_kernel(*refs, add):
    if add:
        x_ref, m_ref, g_ref, w_ref, cs_ref, xo_ref, o_ref = refs
        x = x_ref[...] + _from_token_tiles(m_ref)
        xo_ref[...] = x
    else:
        x_ref, g_ref, w_ref, cs_ref, o_ref = refs
        x = x_ref[...]
    ms = jnp.mean(x * x, axis=-1, keepdims=True)
    h = (x * lax.rsqrt(ms + EPS) * g_ref[...]).astype(BF16)
    acc = jnp.dot(h, w_ref[...], preferred_element_type=F32)
    lo = lax.broadcasted_iota(jnp.int32, (x.shape[0], LANES), 1) < HEAD_DIM
    qk = _qk_slabs()
    for s in range(IN_WIDTH // LANES):
        cols = slice(s * LANES, (s + 1) * LANES)
        y = acc[:, cols]
        if s in qk:
            sq = y * y
            ss_lo = jnp.sum(jnp.where(lo, sq, 0.0), axis=-1, keepdims=True)
            ss_hi = jnp.sum(jnp.where(lo, 0.0, sq), axis=-1, keepdims=True)
            inv = jnp.where(lo, lax.rsqrt(ss_lo * (1.0 / HEAD_DIM) + EPS), lax.rsqrt(ss_hi * (1.0 / HEAD_DIM) + EPS))
            y = y * inv * cs_ref[:, cols]
        o_ref[:, cols] = y.astype(BF16)


def _proj(x2, moe2, gain, w, col_scale):
    t = x2.shape[0]
    add = moe2 is not None
    row_spec = pl.BlockSpec((PROJ_ROWS, D_MODEL), lambda i: (i, 0))
    tile_spec = pl.BlockSpec((PROJ_ROWS * ROW_TILES, LANES), lambda i: (i, 0))
    in_specs = [row_spec] + ([tile_spec] if add else []) + [
        pl.BlockSpec((1, D_MODEL), lambda i: (0, 0)),
        pl.BlockSpec((D_MODEL, IN_WIDTH), lambda i: (0, 0)),
        pl.BlockSpec((1, IN_WIDTH), lambda i: (0, 0)),
    ]
    proj_spec = pl.BlockSpec((PROJ_ROWS, IN_WIDTH), lambda i: (i, 0))
    proj_shape = jax.ShapeDtypeStruct((t, IN_WIDTH), BF16)
    if add:
        out_shape = (jax.ShapeDtypeStruct((t, D_MODEL), F32), proj_shape)
        out_specs = (row_spec, proj_spec)
        args = (x2, moe2, gain, w, col_scale)
    else:
        out_shape, out_specs, args = proj_shape, proj_spec, (x2, gain, w, col_scale)
    return pl.pallas_call(
        functools.partial(_proj_kernel, add=add),
        out_shape=out_shape, grid=(t // PROJ_ROWS,), in_specs=in_specs, out_specs=out_specs,
        compiler_params=_cparams(("parallel",)), name="proj_add" if add else "proj",
    )(*args)


def _data_mask(lane, par):
    return (lane < HEAD_DIM) if par == 0 else (lane >= HEAD_DIM)


def _with_ones(v, dm):
    return jnp.where(dm, v, 1.0)


def _head_norm(x, dm, gain):
    ss = jnp.sum(jnp.where(dm, x * x, 0.0), axis=-1, keepdims=True)
    return x * lax.rsqrt(ss * (1.0 / HEAD_DIM) + EPS) * gain


def _normalise_pair(tot_even, tot_odd, lane):
    even = lane < HEAD_DIM
    num = jnp.where(even, tot_even, tot_odd)
    den = pltpu.roll(jnp.where(even, tot_odd, tot_even), HEAD_DIM, axis=1)
    return num / den


def _memkv_kernel(m_ref, g_ref, w_ref, kg_ref, k_ref, v_ref):
    x = m_ref[...]
    ms = jnp.mean(x * x, axis=-1, keepdims=True)
    h = (x * lax.rsqrt(ms + EPS) * g_ref[...]).astype(BF16)
    kv = jnp.dot(h, w_ref[...], preferred_element_type=F32)
    lane = lax.broadcasted_iota(jnp.int32, (x.shape[0], LANES), 1)
    for slab in range(WIDTH_M // LANES):
        kf = kv[:, slab * LANES:(slab + 1) * LANES]
        vf = kv[:, WIDTH_M + slab * LANES:WIDTH_M + (slab + 1) * LANES]
        for par in (0, 1):
            dm = _data_mask(lane, par)
            kn = _head_norm(kf, dm, kg_ref[...])
            k_ref[2 * slab + par] = jnp.where(dm, kn, 0.0).astype(BF16)
            v_ref[2 * slab + par] = _with_ones(vf, dm).astype(BF16)


def _memkv(mem, gain, w, kgain):
    b, m, _ = mem.shape
    kv_shape = jax.ShapeDtypeStruct((b, N_HEADS_M, m, LANES), BF16)
    kv_spec = pl.BlockSpec((None, N_HEADS_M, m, LANES), lambda i: (i, 0, 0, 0))
    return pl.pallas_call(
        _memkv_kernel, out_shape=(kv_shape, kv_shape), grid=(b,),
        in_specs=[pl.BlockSpec((None, m, D_MODEL), lambda i: (i, 0, 0)),
                  pl.BlockSpec((1, D_MODEL), lambda i: (0, 0)),
                  pl.BlockSpec((D_MODEL, 2 * WIDTH_M), lambda i: (0, 0)),
                  pl.BlockSpec((1, LANES), lambda i: (0, 0))],
        out_specs=(kv_spec, kv_spec), compiler_params=_cparams(("parallel",)), name="memkv",
    )(mem, gain, w, kgain)


PREP_ROWS = 256
A_UNROLL = 16


def _attn_a_kernel(q_ref, k_ref, v_ref, b1_ref, b4_ref, b16_ref, o_ref,
                   qs, ks, vs, qm, km, vm, q4, q16, kd, vd, accd, accm, tot, s_scr):
    s_len = q_ref.shape[0]
    n_prep = s_len // PREP_ROWS
    lane_p = lax.broadcasted_iota(jnp.int32, (PREP_ROWS, LANES), 1)
    d_mid, d_far = DILATED_PATTERNS[1][1], DILATED_PATTERNS[2][1]
    len_mid, len_far = s_len // d_mid, s_len // d_far
    sub = d_far // d_mid

    def chunk_rows(c):
        return pl.ds(pl.multiple_of(c * PREP_ROWS, PREP_ROWS), PREP_ROWS)

    def mid_rows(r):
        return pl.ds(r, len_mid, stride=d_mid), pl.ds(pl.multiple_of(r * len_mid, QBLK), len_mid)

    def far_rows(c):
        return (pl.ds((c // sub) * len_mid + c % sub, len_far, stride=sub),
                pl.ds(pl.multiple_of(c * len_far, QBLK), len_far))

    def split_mid(pairs):
        def body(r, carry):
            tok, mid = mid_rows(r)
            for x_tok, x_mid, x_bf in pairs:
                x = x_tok[tok, :]
                x_mid[mid, :] = x
                x_bf[mid, :] = x.astype(BF16)
            return carry
        lax.fori_loop(0, d_mid, body, 0)

    def split_far(pairs):
        def body(c, carry):
            mid, far = far_rows(c)
            for x_mid, x_bf in pairs:
                x_bf[far, :] = x_mid[mid, :].astype(BF16)
            return carry
        lax.fori_loop(0, d_far, body, 0)

    def q_cast(c, carry):
        qs[chunk_rows(c), :] = q_ref[chunk_rows(c), :].astype(F32)
        return carry
    lax.fori_loop(0, n_prep, q_cast, 0)
    split_mid(((qs, qm, q4),))
    split_far(((qm, q16),))

    for par in (0, 1):
        def prep(c, carry, par=par):
            rows = chunk_rows(c)
            dm = _data_mask(lane_p, par)
            kf = jnp.where(dm, k_ref[rows, :].astype(F32), 0.0)
            vf = _with_ones(v_ref[rows, :].astype(F32), dm)
            ks[rows, :] = kf
            vs[rows, :] = vf
            kd[rows, :] = kf.astype(BF16)
            vd[rows, :] = vf.astype(BF16)
            return carry
        lax.fori_loop(0, n_prep, prep, 0)

        def run_blocks(q_src, length, b_ref, out, par=par):
            nblk = length // QBLK

            def blocks(o, carry):
                units = []
                for i in range(A_UNROLL):
                    u = o * A_UNROLL + i
                    if nblk == 1:
                        base = pl.multiple_of(u * length, QBLK)
                        qrows = pl.ds(base, QBLK)
                        keys = pl.ds(base, QBLK)
                        bias = b_ref[par]
                        kw = QBLK
                    else:
                        r = u // nblk
                        blk = u % nblk
                        q0 = blk * QBLK
                        ws = jnp.clip(q0 - DIL_RADIUS, 0, length - 2 * QBLK)
                        tid = jnp.where(blk == 0, 0, jnp.where(blk == nblk - 1, 2, 1))
                        qrows = pl.ds(pl.multiple_of(r * length + q0, QBLK), QBLK)
                        keys = pl.ds(pl.multiple_of(r * length + ws, DIL_RADIUS), 2 * QBLK)
                        bias = b_ref[par, tid]
                        kw = 2 * QBLK
                    s_scr[i, :, 0:kw] = lax.dot_general(q_src[qrows, :], kd[keys, :], (((1,), (1,)), ((), ())),
                                                        preferred_element_type=F32)
                    units.append((qrows, keys, bias, kw))
                for i, (qrows, keys, bias, kw) in enumerate(units):
                    p = jnp.exp2(s_scr[i, :, 0:kw] + bias).astype(BF16)
                    out[qrows, :] = jnp.dot(p, vd[keys, :], preferred_element_type=F32)
                return carry
            lax.fori_loop(0, s_len // QBLK // A_UNROLL, blocks, 0)

        run_blocks(q_ref, s_len, b1_ref, tot.at[par])

        split_mid(((ks, km, kd), (vs, vm, vd)))
        run_blocks(q4, len_mid, b4_ref, accm)
        split_far(((km, kd), (vm, vd)))
        run_blocks(q16, len_far, b16_ref, accd)

        def merge_far(c, carry):
            mid, far = far_rows(c)
            accm[mid, :] = accm[mid, :] + accd[far, :]
            return carry
        lax.fori_loop(0, d_far, merge_far, 0)

        def merge_mid(r, carry, par=par):
            tok, mid = mid_rows(r)
            tot[par, tok, :] = tot[par, tok, :] + accm[mid, :]
            return carry
        lax.fori_loop(0, d_mid, merge_mid, 0)

    def fin(c, carry):
        rows = chunk_rows(c)
        o_ref[rows, :] = _normalise_pair(tot[0, rows, :], tot[1, rows, :], lane_p).astype(BF16)
        return carry
    lax.fori_loop(0, n_prep, fin, 0)


def _attn_a(proj, s, b1, b4, b16):
    b = proj.shape[0] // s
    n_pairs = WIDTH_A // LANES

    def slab(off):
        return pl.BlockSpec((s, LANES), lambda i, p: (i, off + p))
    return pl.pallas_call(
        _attn_a_kernel, out_shape=jax.ShapeDtypeStruct((b * s, WIDTH_A), BF16), grid=(b, n_pairs),
        in_specs=[slab(0), slab(n_pairs), slab(2 * n_pairs),
                  pl.BlockSpec((2, 3, QBLK, 2 * QBLK), lambda i, p: (p, 0, 0, 0)),
                  pl.BlockSpec((2, 3, QBLK, 2 * QBLK), lambda i, p: (p, 0, 0, 0)),
                  pl.BlockSpec((2, QBLK, QBLK), lambda i, p: (p, 0, 0))],
        out_specs=pl.BlockSpec((s, LANES), lambda i, p: (i, p)),
        scratch_shapes=[pltpu.VMEM((s, LANES), F32)] * 6 + [pltpu.VMEM((s, LANES), BF16)] * 4
        + [pltpu.VMEM((s, LANES), F32)] * 2
        + [pltpu.VMEM((2, s, LANES), F32), pltpu.VMEM((A_UNROLL, QBLK, 2 * QBLK), F32)],
        compiler_params=_cparams(("parallel", "parallel")), name="attn_a",
    )(proj, proj, proj, b1, b4, b16)


NB_GROUP = 4
NB_UNROLL = 8


def _attn_b_kernel(q_ref, k_ref, v_ref, bias_ref, o_ref, kb, vb, s_scr):
    s_len = q_ref.shape[0]
    n_prep = s_len // PREP_ROWS
    rows_total = s_len // GRID_W
    kh = min(NA_ROWS, rows_total)
    n_groups = rows_total // NB_GROUP
    q_len = NB_GROUP * GRID_W
    k_len = (NB_GROUP + kh) * GRID_W
    lane_p = lax.broadcasted_iota(jnp.int32, (PREP_ROWS, LANES), 1)
    lane_q = lax.broadcasted_iota(jnp.int32, (q_len, LANES), 1)

    def prep(c, carry):
        rows = pl.ds(pl.multiple_of(c * PREP_ROWS, PREP_ROWS), PREP_ROWS)
        kf = k_ref[rows, :].astype(F32)
        vf = v_ref[rows, :].astype(F32)
        for par in (0, 1):
            dm = _data_mask(lane_p, par)
            kb[par, rows, :] = jnp.where(dm, kf, 0.0).astype(BF16)
            vb[par, rows, :] = _with_ones(vf, dm).astype(BF16)
        return carry
    lax.fori_loop(0, n_prep, prep, 0)

    def groups(o, carry):
        units = []
        for i in range(NB_UNROLL):
            g = o * NB_UNROLL + i
            ws = jnp.clip(g * NB_GROUP - kh // 2, 0, rows_total - (NB_GROUP + kh))
            typ = jnp.where(g == 0, 0, jnp.where(g == n_groups - 1, 2, 1))
            qrows = pl.ds(pl.multiple_of(g * q_len, q_len), q_len)
            keys = pl.ds(pl.multiple_of(ws * GRID_W, GRID_W), k_len)
            for par in (0, 1):
                s_scr[2 * i + par] = lax.dot_general(q_ref[qrows, :], kb[par, keys, :], (((1,), (1,)), ((), ())),
                                                     preferred_element_type=F32)
            units.append((qrows, keys, typ))
        for i, (qrows, keys, typ) in enumerate(units):
            outs = []
            for par in (0, 1):
                p = jnp.exp2(s_scr[2 * i + par] + bias_ref[par, typ]).astype(BF16)
                outs.append(jnp.dot(p, vb[par, keys, :], preferred_element_type=F32))
            o_ref[qrows, :] = _normalise_pair(outs[0], outs[1], lane_q).astype(BF16)
        return carry
    lax.fori_loop(0, n_groups // NB_UNROLL, groups, 0)


def _attn_b(proj, s, bias):
    b = proj.shape[0] // s
    n_pairs = WIDTH_B // LANES
    first = 3 * WIDTH_A // LANES
    n_types, q_len, k_len = bias.shape[1:]

    def slab(off):
        return pl.BlockSpec((s, LANES), lambda i, p: (i, first + off + p))
    return pl.pallas_call(
        _attn_b_kernel, out_shape=jax.ShapeDtypeStruct((b * s, WIDTH_B), BF16), grid=(b, n_pairs),
        in_specs=[slab(0), slab(n_pairs), slab(2 * n_pairs),
                  pl.BlockSpec((2, n_types, q_len, k_len), lambda i, p: (p, 0, 0, 0))],
        out_specs=pl.BlockSpec((s, LANES), lambda i, p: (i, p)),
        scratch_shapes=[pltpu.VMEM((2, s, LANES), BF16)] * 2 + [pltpu.VMEM((2 * NB_UNROLL, q_len, k_len), F32)],
        compiler_params=_cparams(("parallel", "parallel")), name="attn_b",
    )(proj, proj, proj, bias)


MEM_QROWS = 256
MEM_UNROLL = 8


def _attn_m_kernel(q_ref, k_ref, v_ref, shift_ref, o_ref, s_scr):
    s_len = q_ref.shape[0]
    lane = lax.broadcasted_iota(jnp.int32, (MEM_QROWS, LANES), 1)

    def chunks(o, carry):
        all_rows = []
        for i in range(MEM_UNROLL):
            rows = pl.ds(pl.multiple_of((o * MEM_UNROLL + i) * MEM_QROWS, MEM_QROWS), MEM_QROWS)
            for par in (0, 1):
                s_scr[2 * i + par] = lax.dot_general(q_ref[rows, :], k_ref[par], (((1,), (1,)), ((), ())),
                                                     preferred_element_type=F32)
            all_rows.append(rows)
        for i, rows in enumerate(all_rows):
            outs = []
            for par in (0, 1):
                p = jnp.exp2(s_scr[2 * i + par] - shift_ref[...]).astype(BF16)
                outs.append(jnp.dot(p, v_ref[par], preferred_element_type=F32))
            o_ref[rows, :] = _normalise_pair(outs[0], outs[1], lane).astype(BF16)
        return carry
    lax.fori_loop(0, s_len // MEM_QROWS // MEM_UNROLL, chunks, 0)


def _attn_m(proj, s, kaug, vaug, shift):
    b = proj.shape[0] // s
    m = kaug.shape[2]
    n_pairs = WIDTH_M // LANES
    first = 3 * (WIDTH_A + WIDTH_B) // LANES
    kv_spec = pl.BlockSpec((None, 2, m, LANES), lambda i, p: (i, p, 0, 0))
    return pl.pallas_call(
        _attn_m_kernel, out_shape=jax.ShapeDtypeStruct((b * s, WIDTH_M), BF16), grid=(b, n_pairs),
        in_specs=[pl.BlockSpec((s, LANES), lambda i, p: (i, first + p)), kv_spec, kv_spec,
                  pl.BlockSpec((1, m), lambda i, p: (0, 0))],
        out_specs=pl.BlockSpec((s, LANES), lambda i, p: (i, p)),
        scratch_shapes=[pltpu.VMEM((2 * MEM_UNROLL, MEM_QROWS, m), F32)],
        compiler_params=_cparams(("parallel", "parallel")), name="attn_m",
    )(proj, kaug, vaug, shift)


def _mix_kernel(oa_ref, ob_ref, om_ref, x_ref, og_ref, wo_ref, gf_ref, wr_ref, br_ref,
                x1_ref, h3_ref, route_ref):
    rows = x_ref.shape[0]
    y = None
    off = 0
    for o_ref in (oa_ref, ob_ref, om_ref):
        width = o_ref.shape[1]
        o = o_ref[...].astype(F32)
        ms = jnp.mean(o * o, axis=-1, keepdims=True)
        mixed = (o * lax.rsqrt(ms + EPS) * og_ref[:, off:off + width]).astype(BF16)
        part = jnp.dot(mixed, wo_ref[off:off + width, :], preferred_element_type=F32)
        y = part if y is None else y + part
        off += width
    x1 = x_ref[...] + y
    x1_ref[...] = x1
    ms = jnp.mean(x1 * x1, axis=-1, keepdims=True)
    h = x1 * lax.rsqrt(ms + EPS) * gf_ref[...]
    for j in range(ROW_TILES):
        h3_ref[pl.ds(j, rows, stride=ROW_TILES), :] = h[:, j * LANES:(j + 1) * LANES]

    h_hi = h.astype(BF16)
    h_lo = (h - h_hi.astype(F32)).astype(BF16)
    both = jnp.dot(h_hi, wr_ref[...], preferred_element_type=F32)
    logits = (both[:, 0:LANES] + both[:, LANES:2 * LANES]
              + jnp.dot(h_lo, wr_ref[:, 0:LANES], preferred_element_type=F32)) + br_ref[...]
    lane = lax.broadcasted_iota(jnp.int32, (rows, LANES), 1)
    lane_f = lane.astype(F32)
    neg = -jnp.inf
    far = float(LANES)
    group_shift = N_GROUPS.bit_length() - 1
    member_shift = EXPERTS_PER_GROUP.bit_length() - 1
    is_g = (lane >> group_shift) == (N_EXPERTS >> group_shift)
    gl = jnp.where(is_g, logits, neg)
    gmax = jnp.max(gl, axis=-1, keepdims=True)
    gidx = jnp.min(jnp.where(gl == gmax, lane_f, far), axis=-1, keepdims=True) - float(N_EXPERTS)
    gsum = jnp.sum(jnp.where(is_g, jnp.exp(gl - gmax), 0.0), axis=-1, keepdims=True)
    g_top = 1.0 / gsum
    in_grp = (lane >> member_shift).astype(F32) == gidx
    el = jnp.where(in_grp, logits, neg)
    v1 = jnp.max(el, axis=-1, keepdims=True)
    i1 = jnp.min(jnp.where(el == v1, lane_f, far), axis=-1, keepdims=True)
    el2 = jnp.where(lane_f == i1, neg, el)
    v2 = jnp.max(el2, axis=-1, keepdims=True)
    i2 = jnp.min(jnp.where(el2 == v2, lane_f, far), axis=-1, keepdims=True)
    ev = jnp.exp(v2 - v1)
    w1 = g_top / (1.0 + ev)
    w2 = g_top * ev / (1.0 + ev)
    route = jnp.where(lane == 0, i1, jnp.where(lane == 1, i2, jnp.where(lane == 2, w1,
                      jnp.where(lane == 3, w2, 0.0))))
    per_token = jnp.transpose(route)[0:4, :]
    chosen = jnp.where(lane_f == i1, 1.0, jnp.where(lane_f == i2, 1.0, 0.0))
    counts = jnp.sum(chosen, axis=0, keepdims=True)
    route_ref[...] = jnp.zeros_like(route_ref)
    route_ref[0:4, :] = per_token
    route_ref[4:5, 0:LANES] = counts


def _mix(oa, ob, om, x2, out_gain, w_out, gain_ffn, w_route, b_route):
    t = x2.shape[0]

    def rows(width):
        return pl.BlockSpec((PROJ_ROWS, width), lambda i: (i, 0))

    def whole(*shape):
        return pl.BlockSpec(shape, lambda i: (0,) * len(shape))
    return pl.pallas_call(
        _mix_kernel,
        out_shape=(jax.ShapeDtypeStruct((t, D_MODEL), F32),
                   jax.ShapeDtypeStruct((t * ROW_TILES, LANES), F32),
                   jax.ShapeDtypeStruct((t // PROJ_ROWS * ROUTE_ROWS, PROJ_ROWS), F32)),
        grid=(t // PROJ_ROWS,),
        in_specs=[rows(WIDTH_A), rows(WIDTH_B), rows(WIDTH_M), rows(D_MODEL),
                  whole(1, D_MODEL), whole(D_MODEL, D_MODEL), whole(1, D_MODEL),
                  whole(D_MODEL, 2 * LANES), whole(1, LANES)],
        out_specs=(rows(D_MODEL), pl.BlockSpec((PROJ_ROWS * ROW_TILES, LANES), lambda i: (i, 0)),
                   pl.BlockSpec((ROUTE_ROWS, PROJ_ROWS), lambda i: (i, 0))),
        compiler_params=_cparams(("parallel",)), name="mix",
    )(oa, ob, om, x2, out_gain, w_out, gain_ffn, w_route, b_route)


SCATTER_UNROLL = 8
W_SLOTS = 4
W_CHUNK_ROWS = 512


def _moe_weight_copies(wg_hbm, wu_hbm, wd_hbm, wgu_buf, wd_buf, sems, first_expert, e, slot):
    copies = []
    targets = ((wg_hbm, wgu_buf, pl.ds(0, D_EXPERT)), (wu_hbm, wgu_buf, pl.ds(D_EXPERT, D_EXPERT)),
               (wd_hbm, wd_buf, pl.ds(0, D_MODEL)))
    for hbm, buf, cols in targets:
        for c in range(hbm.shape[1] // W_CHUNK_ROWS):
            rows = pl.ds(c * W_CHUNK_ROWS, W_CHUNK_ROWS)
            copies.append(pltpu.make_async_copy(hbm.at[first_expert + e, rows], buf.at[slot, rows, cols],
                                                sems.at[slot]))
    return copies


def _moe_kernel(rstart_ref, tslot_ref, tfirst_ref, tnext_ref, npair_ref, act_ref,
                src_ref, dst_ref, gate_ref, h3_ref, wg_hbm, wu_hbm, wd_hbm, o_hbm,
                wgu_buf, wd_buf, sems, out_sem, xt0, xt1, lhs0, lhs1, yt0, yt1, acc, *, first_expert):
    b = pl.program_id(0)
    n_rows = h3_ref.shape[0]
    tiles_per_block = 2 * (n_rows // ROW_TILES) // MOE_TM + N_EXPERTS + 2
    pad_rows = src_ref.shape[1] - MOE_TM
    base = b * tiles_per_block
    bufs = ((xt0, lhs0, yt0), (xt1, lhs1, yt1))
    copies = functools.partial(_moe_weight_copies, wg_hbm, wu_hbm, wd_hbm, wgu_buf, wd_buf, sems, first_expert)

    def gather(rs, xt):
        for i in range(MOE_TM):
            off = pl.multiple_of(src_ref[0, rs + i], ROW_TILES)
            xt[i * ROW_TILES:(i + 1) * ROW_TILES, :] = h3_ref[pl.ds(off, ROW_TILES), :]

    def scatter(rs, yt):
        for i0 in range(0, MOE_TM, SCATTER_UNROLL):
            vals = []
            for i in range(i0, i0 + SCATTER_UNROLL):
                rows = pl.ds(pl.multiple_of(dst_ref[0, rs + i], ROW_TILES), ROW_TILES)
                vals.append((rows, acc[rows, :] + gate_ref[0, rs + i] * yt[i * ROW_TILES:(i + 1) * ROW_TILES, :]))
            for rows, val in vals:
                acc[rows, :] = val

    for k in range(W_SLOTS - 1):
        e0 = act_ref[b * (W_SLOTS - 1) + k]

        @pl.when(e0 >= 0)
        def _(e0=e0, k=k):
            for c in copies(e0, k):
                c.start()

    def out_copy(block):
        return pltpu.make_async_copy(acc.at[pl.ds(0, n_rows)],
                                     o_hbm.at[pl.ds(pl.multiple_of(block * n_rows, ROW_TILES), n_rows)], out_sem)

    @pl.when(b > 0)
    def _():
        out_copy(b - 1).wait()
    acc[...] = jnp.zeros_like(acc)
    for half, (xt, _, yt) in enumerate(bufs):
        yt[...] = jnp.zeros_like(yt)
        gather(rstart_ref[base + half], xt)

    def pair(p, carry):
        tiles = []
        for half in (0, 1):
            idx = base + 2 * p + half
            slot = tslot_ref[idx]

            @pl.when(tfirst_ref[idx] == 1)
            def _(slot=slot):
                for c in copies(0, slot):
                    c.wait()
            prev_rs = jnp.where(p > 0, rstart_ref[jnp.maximum(idx - 2, 0)], pad_rows)
            tiles.append((prev_rs, rstart_ref[idx + 2], slot, tnext_ref[idx]))

        for (prev_rs, _, _, _), (_, _, yt) in zip(tiles, bufs):
            scatter(prev_rs, yt)
        for xt, lhs, _ in bufs:
            for k in range(ROW_TILES):
                lhs[:, k * LANES:(k + 1) * LANES] = xt[pl.ds(k, MOE_TM, stride=ROW_TILES), :].astype(BF16)
        for (_, next_rs, _, _), (xt, _, _) in zip(tiles, bufs):
            gather(next_rs, xt)
        for (_, _, slot, _), (_, lhs, yt) in zip(tiles, bufs):
            gu = jnp.dot(lhs[...], wgu_buf[slot], preferred_element_type=F32)
            g = gu[:, :D_EXPERT]
            hid = (g * (1.0 / (1.0 + jnp.exp(-g)))) * gu[:, D_EXPERT:]
            y = jnp.dot(hid.astype(BF16), wd_buf[slot], preferred_element_type=F32)
            for k in range(ROW_TILES):
                yt[pl.ds(k, MOE_TM, stride=ROW_TILES), :] = y[:, k * LANES:(k + 1) * LANES]

        for _, _, slot, nxt in tiles:
            @pl.when(nxt >= 0)
            def _(slot=slot, nxt=nxt):
                for c in copies(nxt, (slot + W_SLOTS - 1) % W_SLOTS):
                    c.start()
        return carry
    n_pairs = npair_ref[b]
    lax.fori_loop(0, n_pairs, pair, 0)
    for half, (_, _, yt) in enumerate(bufs):
        scatter(rstart_ref[base + 2 * (n_pairs - 1) + half], yt)

    out_copy(b).start()

    @pl.when(b == pl.num_programs(0) - 1)
    def _():
        out_copy(b).wait()


def _pick(table, index):
    cols = jnp.arange(table.shape[1], dtype=jnp.int32)
    return jnp.sum(jnp.where(index[:, :, None] == cols[None, None, :], table[:, None, :], 0), axis=2)


def _moe(h3, route_t, wg, wu, wd, first_expert, block_tokens):
    t = h3.shape[0] // ROW_TILES
    nblk = t // block_tokens
    n_assign = 2 * block_tokens
    nt_max = n_assign // MOE_TM + N_EXPERTS + 2

    n_pad = MOE_TM - 1
    stride = 2 * n_assign
    n_list = n_assign + N_EXPERTS * MOE_TM
    rec = route_t.reshape(nblk, block_tokens // PROJ_ROWS, ROUTE_ROWS, PROJ_ROWS)
    e = rec[:, :, 0:2, :].astype(jnp.int32).reshape(nblk, n_assign)
    w = rec[:, :, 2:4, :].reshape(nblk, n_assign)
    cnt = jnp.sum(rec[:, :, 4, :N_EXPERTS], axis=1).astype(jnp.int32)
    experts = jnp.arange(N_EXPERTS, dtype=jnp.int32)
    key = e * stride + jnp.arange(n_assign, dtype=jnp.int32)[None, :]
    pad_key = (experts[:, None] * stride + n_assign + jnp.arange(n_pad, dtype=jnp.int32)[None, :]).reshape(1, -1)
    tail_key = (N_EXPERTS * stride + n_assign + experts)[None, :]
    extra_key = jnp.broadcast_to(jnp.concatenate([pad_key, tail_key], axis=1), (nblk, n_list - n_assign))
    key_s, w_s = lax.sort((jnp.concatenate([key, extra_key], axis=1),
                           jnp.concatenate([w, jnp.zeros((nblk, n_list - n_assign), F32)], axis=1)),
                          dimension=1, num_keys=1)
    idx_s = key_s & (stride - 1)
    real = idx_s < n_assign
    row_off = ((idx_s // (2 * PROJ_ROWS)) * PROJ_ROWS + idx_s % PROJ_ROWS) * ROW_TILES
    src_s = jnp.where(real, row_off, 0)
    dst_s = jnp.where(real, row_off, block_tokens * ROW_TILES)
    cstart = jnp.cumsum(cnt, axis=1) - cnt + experts[None, :] * n_pad
    ntile_e = (cnt + MOE_TM - 1) // MOE_TM
    tend = jnp.cumsum(ntile_e, axis=1)
    ntiles = tend[:, -1:]
    active = (ntile_e > 0).astype(jnp.int32)
    order_e = jnp.cumsum(active, axis=1) - active
    ranks = jnp.arange(N_EXPERTS + W_SLOTS, dtype=jnp.int32)
    hit = (order_e[:, None, :] == ranks[None, :, None]) & (active[:, None, :] == 1)
    act = jnp.sum(jnp.where(hit, experts[None, None, :] + 1, 0), axis=2) - 1
    tile_ids = jnp.arange(nt_max, dtype=jnp.int32)[None, :]
    live = tile_ids < ntiles
    texp = jnp.minimum(jnp.sum((tile_ids[:, :, None] >= tend[:, None, :]).astype(jnp.int32), axis=2), N_EXPERTS - 1)
    texp = jnp.where(live, texp, _pick(texp, jnp.maximum(ntiles - 1, 0)))
    within = tile_ids - _pick(tend - ntile_e, texp)
    rstart = jnp.where(live, _pick(cstart, texp) + within * MOE_TM, n_list - MOE_TM)
    torder = _pick(order_e, texp)
    tfirst = (live & (within == 0)).astype(jnp.int32)
    tnext = jnp.where(tfirst == 1, _pick(act, torder + (W_SLOTS - 1)), -1)
    npairs = (ntiles[:, 0] + 1) // 2

    def flat(x):
        return x.reshape(-1).astype(jnp.int32)

    def smem_list():
        return pl.BlockSpec((None, 1, n_list), lambda b, *_: (b, 0, 0), memory_space=pltpu.SMEM)
    tile_buf = pltpu.VMEM((MOE_TM * ROW_TILES, LANES), F32)
    grid_spec = pltpu.PrefetchScalarGridSpec(
        num_scalar_prefetch=6, grid=(nblk,),
        in_specs=[smem_list(), smem_list(), smem_list(),
                  pl.BlockSpec((block_tokens * ROW_TILES, LANES), lambda b, *_: (b, 0),
                               pipeline_mode=pl.Buffered(1)),
                  pl.BlockSpec(memory_space=pl.ANY), pl.BlockSpec(memory_space=pl.ANY),
                  pl.BlockSpec(memory_space=pl.ANY)],
        out_specs=pl.BlockSpec(memory_space=pl.ANY),
        scratch_shapes=[pltpu.VMEM((W_SLOTS, D_MODEL, 2 * D_EXPERT), BF16),
                        pltpu.VMEM((W_SLOTS, D_EXPERT, D_MODEL), BF16),
                        pltpu.SemaphoreType.DMA((W_SLOTS,)), pltpu.SemaphoreType.DMA(()),
                        tile_buf, tile_buf,
                        pltpu.VMEM((MOE_TM, D_MODEL), BF16), pltpu.VMEM((MOE_TM, D_MODEL), BF16),
                        tile_buf, tile_buf,
                        pltpu.VMEM(((block_tokens + 1) * ROW_TILES, LANES), F32)])
    return pl.pallas_call(
        functools.partial(_moe_kernel, first_expert=first_expert),
        out_shape=jax.ShapeDtypeStruct((t * ROW_TILES, LANES), F32), grid_spec=grid_spec,
        compiler_params=_cparams(("arbitrary",)), name="moe",
    )(flat(rstart), flat(torder % W_SLOTS), flat(tfirst), flat(tnext), flat(npairs), flat(act[:, :W_SLOTS - 1]),
      src_s.reshape(nblk, 1, n_list), dst_s.reshape(nblk, 1, n_list), w_s.reshape(nblk, 1, n_list), h3, wg, wu, wd)


def _add_kernel(a_ref, b_ref, o_ref):
    o_ref[...] = a_ref[...] + _from_token_tiles(b_ref)


def _add(a, b_tiles):
    t = a.shape[0]
    spec = pl.BlockSpec((PROJ_ROWS, D_MODEL), lambda i: (i, 0))
    tile_spec = pl.BlockSpec((PROJ_ROWS * ROW_TILES, LANES), lambda i: (i, 0))
    return pl.pallas_call(_add_kernel, out_shape=jax.ShapeDtypeStruct(a.shape, a.dtype), grid=(t // PROJ_ROWS,),
                          in_specs=[spec, tile_spec], out_specs=spec, compiler_params=_cparams(("parallel",)),
                          name="residual_add")(a, b_tiles)


def _score_bound(gq, gk):
    return (HEAD_DIM * ATTN_SCALE * LOG2E) * jnp.max(jnp.abs(gq)) * jnp.max(jnp.abs(gk))


def _alibi_tables(shift):
    slopes = (2.0 ** (-8.0 * np.arange(1, N_HEADS_A + 1) / N_HEADS_A)).astype(np.float32)
    tables = []
    for _, dil in DILATED_PATTERNS:
        if dil == DILATED_PATTERNS[-1][1]:
            offs, kw = (0,), QBLK
        else:
            offs, kw = (0, DIL_RADIUS, 2 * DIL_RADIUS), 2 * QBLK
        tiles = []
        for off in offs:
            rel = np.abs((off + np.arange(QBLK))[:, None] - np.arange(kw)[None, :])
            tiles.append(np.where(rel <= DIL_RADIUS, rel * dil, np.inf).astype(np.float32))
        dist = np.stack(tiles)
        tab = -(slopes * LOG2E)[:, None, None, None] * dist[None]
        tab = jnp.asarray(tab) - shift
        tables.append(tab[:, 0] if len(offs) == 1 else tab)
    return tables


def _nb_window_start(g, rows_total):
    kh = min(NA_ROWS, rows_total)
    return int(np.clip(g * NB_GROUP - kh // 2, 0, rows_total - (NB_GROUP + kh)))


def _rpb_table(rpb, shift, rows_total):
    kh = min(NA_ROWS, rows_total)
    n_groups = rows_total // NB_GROUP
    win = NB_GROUP + kh
    assert n_groups >= 3 and all(_nb_window_start(g, rows_total) == g * NB_GROUP - kh // 2 for g in range(1, n_groups - 1))
    c = np.arange(GRID_W)
    c0 = np.clip(c - NA_COLS // 2, 0, GRID_W - NA_COLS)
    col_ok = (c[None, :] >= c0[:, None]) & (c[None, :] < c0[:, None] + NA_COLS)
    dc = np.clip(c[None, :] - c[:, None], -(NA_COLS - 1), NA_COLS - 1) + NA_COLS - 1
    n_dc = 2 * NA_COLS - 1
    pick = dc[None, None, :, :, None] == np.arange(n_dc)[None, None, None, None, :]
    cols = jnp.sum(jnp.where(pick, rpb[:, :, None, None, :], 0.0), axis=-1)
    cols = jnp.where(col_ok[None, None], cols, -jnp.inf)
    masked = jnp.full((rpb.shape[0], GRID_W, GRID_W), -jnp.inf, F32)
    types = []
    for g in (0, 1, n_groups - 1):
        ws = _nb_window_start(g, rows_total)
        q_rows = []
        for i in range(NB_GROUP):
            r = g * NB_GROUP + i
            r0 = int(np.clip(r - kh // 2, 0, rows_total - kh))
            blocks = [cols[:, ws + j - r + NA_ROWS - 1] if r0 <= ws + j < r0 + kh else masked for j in range(win)]
            q_rows.append(jnp.concatenate(blocks, axis=2))
        types.append(jnp.concatenate(q_rows, axis=1))
    tab = jnp.stack(types, axis=1) * LOG2E
    return tab - shift[:, None, None, None]


def kernel(x, mem, norm_mix, w_in, qk_gain, rpb, norm_mem, w_mem_kv, out_gain, w_out, norm_ffn, w_group,
           b_group, w_router, b_router, w_gate, w_up, w_down):
    bsz, seq, d = x.shape
    t = bsz * seq
    depth = w_in.shape[0]
    rows_total = seq // GRID_W

    x2 = x.reshape(t, d)
    wg_all = w_gate.astype(BF16).reshape(depth * N_EXPERTS, d, D_EXPERT)
    wu_all = w_up.astype(BF16).reshape(depth * N_EXPERTS, d, D_EXPERT)
    wd_all = w_down.astype(BF16).reshape(depth * N_EXPERTS, D_EXPERT, d)
    moe = None
    for l in range(depth):
        gains = jnp.tile(qk_gain[l], (1, 2))
        col_scale = jnp.concatenate([
            jnp.tile(gains[0] * QSCALE, WIDTH_A // LANES), jnp.tile(gains[1], WIDTH_A // LANES), jnp.ones((WIDTH_A,), F32),
            jnp.tile(gains[2] * QSCALE, WIDTH_B // LANES), jnp.tile(gains[3], WIDTH_B // LANES), jnp.ones((WIDTH_B,), F32),
            jnp.tile(gains[4] * QSCALE, WIDTH_M // LANES)])[None, :]
        w_in_l = w_in[l].astype(BF16)
        if moe is None:
            proj = _proj(x2, None, norm_mix[l][None, :], w_in_l, col_scale)
        else:
            x2, proj = _proj(x2, moe, norm_mix[l][None, :], w_in_l, col_scale)

        shift_a = _score_bound(qk_gain[l, 0], qk_gain[l, 1])
        b1, b4, b16 = _alibi_tables(shift_a)
        oa = _attn_a(proj, seq, b1, b4, b16)

        shift_b = _score_bound(qk_gain[l, 2], qk_gain[l, 3]) + LOG2E * jnp.max(rpb[l], axis=(1, 2))
        ob = _attn_b(proj, seq, _rpb_table(rpb[l], shift_b, rows_total))

        kaug, vaug = _memkv(mem, norm_mem[l][None, :], w_mem_kv[l].astype(BF16), gains[5:6])
        shift_m = jnp.full((1, mem.shape[1]), _score_bound(qk_gain[l, 4], qk_gain[l, 5]), F32)
        om = _attn_m(proj, seq, kaug, vaug, shift_m)

        w_route = jnp.zeros((d, LANES), F32).at[:, :N_EXPERTS].set(w_router[l])
        w_route = w_route.at[:, N_EXPERTS:N_EXPERTS + N_GROUPS].set(w_group[l])
        w_route_hi = w_route.astype(BF16)
        w_route_lo = (w_route - w_route_hi.astype(F32)).astype(BF16)
        b_route = jnp.zeros((1, LANES), F32).at[0, :N_EXPERTS].set(b_router[l])
        b_route = b_route.at[0, N_EXPERTS:N_EXPERTS + N_GROUPS].set(b_group[l])
        x2, h3, route = _mix(oa, ob, om, x2,
                             out_gain[l][None, :], w_out[l].astype(BF16), norm_ffn[l][None, :],
                             jnp.concatenate([w_route_hi, w_route_lo], axis=1), b_route)

        moe = _moe(h3, route, wg_all, wu_all, wd_all, l * N_EXPERTS, MOE_BLOCK_TOKENS)
    return _add(x2, moe).reshape(bsz, seq, d)
```
